```python
import jax, jax.numpy as jnp
from jax import lax
import numpy as np

D_MODEL = 1024
BATCH = 8
SEQ = 2048
DEPTH = 1
DEC_BATCH = 128
DEC_SEQ = 8
PAST_LEN = 16384
PAGE_SIZE = 128

CONV_W = 4
LRU_WIDTH = D_MODEL
LRU_BLOCKS = 8
LRU_BLOCK = LRU_WIDTH // LRU_BLOCKS
LRU_C = 8.0
GDN_HEADS = 8
GDN_DK = 128
GDN_DV = 128
GDN_KEY_W = GDN_HEADS * GDN_DK
GDN_VAL_W = GDN_HEADS * GDN_DV
GDN_QKV_W = 2 * GDN_KEY_W + GDN_VAL_W
GDN_CHUNK = 64
NORM_EPS = 1e-6
IN_SPLITS = (LRU_WIDTH, LRU_WIDTH, GDN_QKV_W, GDN_VAL_W, GDN_HEADS, GDN_HEADS, D_MODEL, D_MODEL)
IN_WIDTH = sum(IN_SPLITS)

kernel_name = "hawk_gdn_parallel_gated_decoder_step"


def rms_norm(x, gain):
    x32 = x.astype(jnp.float32)
    y = x32 * lax.rsqrt(jnp.mean(x32 * x32, axis=-1, keepdims=True) + NORM_EPS)
    return (y * gain.astype(jnp.float32)).astype(x.dtype)


def l2_normalize(x):
    return x * lax.rsqrt(jnp.sum(x * x, axis=-1, keepdims=True) + NORM_EPS)


def causal_depthwise_conv(x, buf, w):
    T = x.shape[1]
    xp = jnp.concatenate([buf.astype(x.dtype), x], axis=1)
    y = sum(xp[:, i:i + T] * w[i].astype(x.dtype) for i in range(CONV_W))
    return y, xp[:, xp.shape[1] - (CONV_W - 1):]


def rg_lru(xc, h0, reset, wa, ba, wx, bx, a_logit):
    B, T, _ = xc.shape
    x32 = xc.astype(jnp.float32)
    xb = x32.reshape(B, T, LRU_BLOCKS, LRU_BLOCK)
    r = jax.nn.sigmoid(jnp.einsum('btgi,gij->btgj', xb, wa.astype(jnp.float32)).reshape(B, T, LRU_WIDTH) + ba.astype(jnp.float32))
    ig = jax.nn.sigmoid(jnp.einsum('btgi,gij->btgj', xb, wx.astype(jnp.float32)).reshape(B, T, LRU_WIDTH) + bx.astype(jnp.float32))
    log_a = LRU_C * r * jax.nn.log_sigmoid(a_logit.astype(jnp.float32))
    rs = reset[None, :, None]
    a = jnp.where(rs, 0.0, jnp.exp(log_a))
    mult = jnp.where(rs, 1.0, jnp.sqrt(-jnp.expm1(2.0 * log_a)))
    b = mult * ig * x32
    b = b.at[:, 0].add(a[:, 0] * h0)

    def combine(left, right):
        al, bl = left
        ar, br = right
        return al * ar, ar * bl + br

    _, h = lax.associative_scan(combine, (a, b), axis=1)
    return h, h[:, -1]


def gated_delta_rule_chunked(q, k, v, g, beta, S0):
    B, T, H, DK = q.shape
    DV = v.shape[-1]
    C = min(GDN_CHUNK, T)
    n = -(-T // C)
    pad = n * C - T
    if pad:
        def padf(a):
            return jnp.pad(a, [(0, 0), (0, pad)] + [(0, 0)] * (a.ndim - 2))
        q, k, v, g, beta = padf(q), padf(k), padf(v), padf(g), padf(beta)

    def chunks(a):
        a = a.reshape((B, n, C, H) + a.shape[3:])
        return jnp.moveaxis(a, (1, 3), (0, 2))

    qc, kc, vc, bc = chunks(q), chunks(k), chunks(v), chunks(beta)
    gc = jnp.cumsum(chunks(g), axis=-1)
    idx = jnp.arange(C)
    causal = idx[:, None] >= idx[None, :]
    strict = idx[:, None] > idx[None, :]
    diff = gc[..., :, None] - gc[..., None, :]
    decay = jnp.where(causal, jnp.exp(jnp.where(causal, diff, 0.0)), 0.0)
    kb = kc * bc[..., None]
    M = jnp.where(strict, jnp.einsum('nbhik,nbhjk->nbhij', kb, kc) * decay, 0.0)
    P = -M
    Tinv = jnp.broadcast_to(jnp.eye(C, dtype=M.dtype), M.shape)
    for _ in range((C - 1).bit_length()):
        Tinv = Tinv + jnp.einsum('nbhij,nbhjk->nbhik', Tinv, P)
        P = jnp.einsum('nbhij,nbhjk->nbhik', P, P)
    u = jnp.einsum('nbhij,nbhjv->nbhiv', Tinv, vc * bc[..., None])
    w = jnp.einsum('nbhij,nbhjk->nbhik', Tinv, kb * jnp.exp(gc)[..., None])
    a_qk = jnp.where(causal, jnp.einsum('nbhik,nbhjk->nbhij', qc, kc) * decay, 0.0)
    q_dec = qc * jnp.exp(gc)[..., None]
    g_last = gc[..., -1]
    k_dec = kc * jnp.exp(g_last[..., None] - gc)[..., None]

    def step(S, xs):
        q_i, k_i, u_i, w_i, a_i, gl_i = xs
        v_new = u_i - jnp.einsum('bhck,bhkv->bhcv', w_i, S)
        o_i = jnp.einsum('bhck,bhkv->bhcv', q_i, S) + jnp.einsum('bhij,bhjv->bhiv', a_i, v_new)
        S = S * jnp.exp(gl_i)[..., None, None] + jnp.einsum('bhck,bhcv->bhkv', k_i, v_new)
        return S, o_i

    S_final, o = lax.scan(step, S0, (q_dec, k_dec, u, w, a_qk, g_last))
    o = jnp.moveaxis(o, (0, 2), (1, 3)).reshape(B, n * C, H, DV)[:, :T]
    return o, S_final


def hybrid_layer(x, lru_conv_buf, lru_h0, gdn_conv_buf, gdn_S0, start_pos,
                 norm_pre, norm_post, w_in, lru_conv_w, lru_conv_b, lru_wa, lru_ba, lru_wx, lru_bx,
                 lru_a_logit, gdn_conv_w, gdn_A_log, gdn_dt_bias, gdn_norm_w, w_br_lru, w_br_gdn, w_out):
    B, T, _ = x.shape
    u = rms_norm(x, norm_pre)
    z = jnp.einsum('btd,de->bte', u, w_in)
    offsets = np.cumsum(IN_SPLITS)[:-1].tolist()
    lru_x, lru_gate, gdn_qkv, gdn_gate, gdn_b, gdn_a, m_lru, m_gdn = jnp.split(z, offsets, axis=-1)

    lru_xc, lru_conv_new = causal_depthwise_conv(lru_x, lru_conv_buf, lru_conv_w)
    lru_xc = lru_xc + lru_conv_b.astype(x.dtype)
    reset = (jnp.arange(T) + start_pos) == 0
    h, lru_h_new = rg_lru(lru_xc, lru_h0.astype(jnp.float32), reset, lru_wa, lru_ba, lru_wx, lru_bx, lru_a_logit)
    lru_out = h.astype(x.dtype) * jax.nn.silu(lru_gate)

    qkv, gdn_conv_new = causal_depthwise_conv(gdn_qkv, gdn_conv_buf, gdn_conv_w)
    qkv = jax.nn.silu(qkv).astype(jnp.float32)
    q, k, v = jnp.split(qkv, [GDN_KEY_W, 2 * GDN_KEY_W], axis=-1)
    q = l2_normalize(q.reshape(B, T, GDN_HEADS, GDN_DK)) * (GDN_DK ** -0.5)
    k = l2_normalize(k.reshape(B, T, GDN_HEADS, GDN_DK))
    v = v.reshape(B, T, GDN_HEADS, GDN_DV)
    beta = jax.nn.sigmoid(gdn_b.astype(jnp.float32))
    g = -jnp.exp(gdn_A_log.astype(jnp.float32)) * jax.nn.softplus(gdn_a.astype(jnp.float32) + gdn_dt_bias.astype(jnp.float32))
    o, S_new = gated_delta_rule_chunked(q, k, v, g, beta, gdn_S0.astype(jnp.float32))
    o = o * lax.rsqrt(jnp.mean(o * o, axis=-1, keepdims=True) + NORM_EPS) * gdn_norm_w.astype(jnp.float32)
    o = o * jax.nn.silu(gdn_gate.astype(jnp.float32).reshape(B, T, GDN_HEADS, GDN_DV))
    gdn_out = o.reshape(B, T, GDN_VAL_W).astype(x.dtype)

    p_lru = jnp.einsum('bte,ed->btd', lru_out, w_br_lru)
    p_gdn = jnp.einsum('bte,ed->btd', gdn_out, w_br_gdn)
    merged = jax.nn.sigmoid(m_lru) * p_lru + jax.nn.sigmoid(m_gdn) * p_gdn
    y = jnp.einsum('btd,de->bte', merged, w_out)
    x_new = x + rms_norm(y, norm_post)
    return x_new, lru_conv_new, lru_h_new, gdn_conv_new, S_new


def setup_inputs(seed: int = 0) -> dict:
    key = jax.random.key(seed)
    ks = jax.random.split(key, 24)
    f32 = jnp.float32
    nrm = lambda k, shape, s: jax.random.normal(k, shape, f32) * s
    a_u = jax.random.uniform(ks[14], (DEPTH, LRU_WIDTH), f32, 0.9, 0.999)
    dt = jnp.exp(jax.random.uniform(ks[17], (DEPTH, GDN_HEADS), f32, np.log(0.001), np.log(0.1)))
    return {
        "x_prompt": nrm(ks[0], (BATCH, SEQ, D_MODEL), 1.0),
        "x_sample": nrm(ks[1], (DEC_BATCH, DEC_SEQ, D_MODEL), 1.0),
        "state_lru_conv": nrm(ks[2], (DEPTH, DEC_BATCH, CONV_W - 1, LRU_WIDTH), 1.0),
        "state_lru_h": nrm(ks[3], (DEPTH, DEC_BATCH, LRU_WIDTH), 0.5),
        "state_gdn_conv": nrm(ks[4], (DEPTH, DEC_BATCH, CONV_W - 1, GDN_QKV_W), 1.0),
        "state_gdn_S": nrm(ks[5], (DEPTH, DEC_BATCH, GDN_HEADS, GDN_DK, GDN_DV), 0.1),
        "norm_pre": 1.0 + nrm(ks[6], (DEPTH, D_MODEL), 0.05),
        "norm_post": 1.0 + nrm(ks[7], (DEPTH, D_MODEL), 0.05),
        "w_in": nrm(ks[8], (DEPTH, D_MODEL, IN_WIDTH), D_MODEL ** -0.5),
        "lru_conv_w": nrm(ks[9], (DEPTH, CONV_W, LRU_WIDTH), CONV_W ** -0.5),
        "lru_conv_b": nrm(ks[10], (DEPTH, LRU_WIDTH), 0.02),
        "lru_wa": nrm(ks[11], (DEPTH, LRU_BLOCKS, LRU_BLOCK, LRU_BLOCK), LRU_BLOCK ** -0.5),
        "lru_ba": nrm(ks[12], (DEPTH, LRU_WIDTH), 0.02),
        "lru_wx": nrm(ks[13], (DEPTH, LRU_BLOCKS, LRU_BLOCK, LRU_BLOCK), LRU_BLOCK ** -0.5),
        "lru_bx": nrm(ks[15], (DEPTH, LRU_WIDTH), 0.02),
        "lru_a_logit": jnp.log(a_u) - jnp.log1p(-a_u),
        "gdn_conv_w": nrm(ks[16], (DEPTH, CONV_W, GDN_QKV_W), CONV_W ** -0.5),
        "gdn_A_log": jnp.log(jax.random.uniform(ks[18], (DEPTH, GDN_HEADS), f32, 1.0, 16.0)),
        "gdn_dt_bias": dt + jnp.log(-jnp.expm1(-dt)),
        "gdn_norm_w": 1.0 + nrm(ks[19], (DEPTH, GDN_DV), 0.05),
        "w_br_lru": nrm(ks[20], (DEPTH, LRU_WIDTH, D_MODEL), LRU_WIDTH ** -0.5),
        "w_br_gdn": nrm(ks[21], (DEPTH, GDN_VAL_W, D_MODEL), GDN_VAL_W ** -0.5),
        "w_out": nrm(ks[22], (DEPTH, D_MODEL, D_MODEL), D_MODEL ** -0.5),
    }


def reference(x_prompt, x_sample, state_lru_conv, state_lru_h, state_gdn_conv, state_gdn_S,
              norm_pre, norm_post, w_in, lru_conv_w, lru_conv_b, lru_wa, lru_ba, lru_wx, lru_bx,
              lru_a_logit, gdn_conv_w, gdn_A_log, gdn_dt_bias, gdn_norm_w, w_br_lru, w_br_gdn, w_out):
    yp, ys = x_prompt, x_sample
    B = x_prompt.shape[0]
    p_lc, p_lh, p_gc, p_gs = [], [], [], []
    s_lc, s_lh, s_gc, s_gs = [], [], [], []
    for l in range(DEPTH):
        w_l = (norm_pre[l], norm_post[l], w_in[l], lru_conv_w[l], lru_conv_b[l], lru_wa[l], lru_ba[l],
               lru_wx[l], lru_bx[l], lru_a_logit[l], gdn_conv_w[l], gdn_A_log[l], gdn_dt_bias[l],
               gdn_norm_w[l], w_br_lru[l], w_br_gdn[l], w_out[l])
        yp, lc, lh, gc, gs = hybrid_layer(
            yp,
            jnp.zeros((B, CONV_W - 1, LRU_WIDTH), x_prompt.dtype),
            jnp.zeros((B, LRU_WIDTH), jnp.float32),
            jnp.zeros((B, CONV_W - 1, GDN_QKV_W), x_prompt.dtype),
            jnp.zeros((B, GDN_HEADS, GDN_DK, GDN_DV), jnp.float32),
            0, *w_l)
        p_lc.append(lc.astype(state_lru_conv.dtype))
        p_lh.append(lh.astype(state_lru_h.dtype))
        p_gc.append(gc.astype(state_gdn_conv.dtype))
        p_gs.append(gs.astype(state_gdn_S.dtype))
        ys, lc, lh, gc, gs = hybrid_layer(
            ys, state_lru_conv[l], state_lru_h[l], state_gdn_conv[l], state_gdn_S[l], PAST_LEN, *w_l)
        s_lc.append(lc.astype(state_lru_conv.dtype))
        s_lh.append(lh.astype(state_lru_h.dtype))
        s_gc.append(gc.astype(state_gdn_conv.dtype))
        s_gs.append(gs.astype(state_gdn_S.dtype))
    return (yp, ys,
            jnp.stack(p_lc), jnp.stack(p_lh), jnp.stack(p_gc), jnp.stack(p_gs),
            jnp.stack(s_lc), jnp.stack(s_lh), jnp.stack(s_gc), jnp.stack(s_gs))
```

```python
import functools

import jax
import jax.numpy as jnp
from jax import lax
from jax.experimental import pallas as pl
from jax.experimental.pallas import tpu as pltpu

F32 = jnp.float32
BF16 = jnp.bfloat16

D_MODEL = 1024
CONV_W = 4
LRU_WIDTH = 1024
LRU_BLOCKS = 8
LRU_BLOCK = LRU_WIDTH // LRU_BLOCKS
LRU_C = 8.0
GDN_HEADS = 8
GDN_DK = 128
GDN_DV = 128
GDN_KEY_W = GDN_HEADS * GDN_DK
GDN_VAL_W = GDN_HEADS * GDN_DV
GDN_QKV_W = 2 * GDN_KEY_W + GDN_VAL_W
GDN_CHUNK = 64
NORM_EPS = 1e-6
PAST_LEN = 16384

LANES = 128
SUBLANES = 8
Z_WIDTH = 8192
COL_LRU_X, COL_LRU_GATE, COL_Q, COL_K, COL_V, COL_GDN_GATE, COL_M_LRU, COL_M_GDN = range(8)
BETA_LANE = 0
ALPHA_LANE = GDN_HEADS
VMEM_LIMIT = 48 * 1024 * 1024


def _softplus(x):
    return jnp.maximum(x, 0.0) + jnp.log1p(jnp.exp(-jnp.abs(x)))


def _silu(x):
    return x * jax.nn.sigmoid(x)


def _bdot(a, b):
    return jnp.dot(a.astype(BF16), b.astype(BF16), preferred_element_type=F32)


def _in_proj_body(x_ref, gain_ref, w_ref, wba_ref, z_ref, ba_ref, u_scr):
    @pl.when(pl.program_id(1) == 0)
    def _():
        x = x_ref[...]
        ms = jnp.mean(x * x, axis=-1, keepdims=True)
        u = ((x * lax.rsqrt(ms + NORM_EPS)) * gain_ref[...]).astype(BF16)
        u_scr[...] = u
        ba_ref[...] = jnp.dot(u, wba_ref[...], preferred_element_type=F32)

    z_ref[...] = jnp.dot(u_scr[...], w_ref[...], preferred_element_type=F32)


def _in_proj(x2, gain, w_main, w_ba, *, tm, tn):
    n = x2.shape[0]
    return pl.pallas_call(
        _in_proj_body,
        grid=(n // tm, Z_WIDTH // tn),
        in_specs=[
            pl.BlockSpec((tm, D_MODEL), lambda i, j: (i, 0)),
            pl.BlockSpec((1, D_MODEL), lambda i, j: (0, 0)),
            pl.BlockSpec((D_MODEL, tn), lambda i, j: (0, j)),
            pl.BlockSpec((D_MODEL, LANES), lambda i, j: (0, 0)),
        ],
        out_specs=[
            pl.BlockSpec((tm, tn), lambda i, j: (i, j)),
            pl.BlockSpec((tm, LANES), lambda i, j: (i, 0)),
        ],
        out_shape=[
            jax.ShapeDtypeStruct((n, Z_WIDTH), F32),
            jax.ShapeDtypeStruct((n, LANES), F32),
        ],
        scratch_shapes=[pltpu.VMEM((tm, D_MODEL), BF16)],
        compiler_params=pltpu.CompilerParams(
            dimension_semantics=("arbitrary", "arbitrary"), vmem_limit_bytes=VMEM_LIMIT),
        name="in_proj",
    )(x2, gain, w_main, w_ba)


def _lru_body(x_ref, gate_ref, cs_ref, h0_ref, cw_ref, cb_ref, wax_ref, ba_ref, bx_ref, alog_ref,
              out_ref, hout_ref, xp_scr, a_scr, b_scr, h_scr, *, bt, tt, reset_first):
    j = pl.program_id(1)
    rows = bt * tt
    width = LRU_WIDTH

    @pl.when(j == 0)
    def _():
        xp_scr[:, 5:8, :] = cs_ref[...]
        h_scr[...] = h0_ref[...]

    xp_scr[:, 8:8 + tt, :] = x_ref[...].reshape(bt, tt, width)
    cw = cw_ref[...]
    xc = xp_scr[:, 5:5 + tt, :] * cw[0].reshape(1, 1, width)
    for i in range(1, CONV_W):
        xc = xc + xp_scr[:, 5 + i:5 + i + tt, :] * cw[i].reshape(1, 1, width)
    xp_scr[:, 5:8, :] = xp_scr[:, 5 + tt:8 + tt, :]
    xc = (xc + cb_ref[...].reshape(1, 1, width)).reshape(rows, width)

    a_l = alog_ref[...]
    log_sig_a = jnp.minimum(a_l, 0.0) - jnp.log1p(jnp.exp(-jnp.abs(a_l)))

    row = lax.broadcasted_iota(jnp.int32, (rows, LRU_BLOCK), 0)
    row_in_group = row % SUBLANES
    if reset_first:
        is_reset = jnp.logical_and(row % tt == 0, j == 0)

    for g in range(LRU_BLOCKS):
        gs = slice(g * LRU_BLOCK, (g + 1) * LRU_BLOCK)
        xg = xc[:, gs]
        pre = jnp.dot(xg.astype(BF16), wax_ref[g], preferred_element_type=F32)
        r = jax.nn.sigmoid(pre[:, :LRU_BLOCK] + ba_ref[:, gs])
        ig = jax.nn.sigmoid(pre[:, LRU_BLOCK:] + bx_ref[:, gs])
        log_a = (LRU_C * r) * log_sig_a[:, gs]
        a = jnp.exp(log_a)
        t = jnp.tanh(log_a)
        mult = jnp.sqrt((-2.0 * t) / (1.0 - t))
        if reset_first:
            a = jnp.where(is_reset, 0.0, a)
            mult = jnp.where(is_reset, 1.0, mult)
        b = mult * ig * xg
        s = 1
        while s < SUBLANES:
            keep = row_in_group >= s
            a_sh = jnp.where(keep, pltpu.roll(a, s, axis=0), 1.0)
            b_sh = jnp.where(keep, pltpu.roll(b, s, axis=0), 0.0)
            b = a * b_sh + b
            a = a * a_sh
            s *= 2
        a_scr[:, :, gs] = a.reshape(bt, tt, LRU_BLOCK)
        b_scr[:, :, gs] = b.reshape(bt, tt, LRU_BLOCK)

    def group_step(i, carry):
        r0 = pl.multiple_of(i * SUBLANES, SUBLANES)
        h = b_scr[:, pl.ds(r0, SUBLANES), :] + a_scr[:, pl.ds(r0, SUBLANES), :] * carry
        b_scr[:, pl.ds(r0, SUBLANES), :] = h
        return h[:, SUBLANES - 1:SUBLANES, :]

    n_groups = tt // SUBLANES
    if n_groups == 1:
        carry = group_step(0, h_scr[...])
    else:
        carry = lax.fori_loop(0, n_groups, group_step, h_scr[...])
    h_scr[...] = carry

    h = b_scr[...].reshape(rows, width)
    out_ref[...] = (h * _silu(gate_ref[...])).astype(out_ref.dtype)

    @pl.when(j == pl.num_programs(1) - 1)
    def _():
        hout_ref[...] = carry


def _lru(z, conv_state, h0, cw, cb, wax, ba, bx, alog, *, nb, t, bt, tt, reset_first):
    nt = t // tt
    rows = bt * tt
    body = functools.partial(_lru_body, bt=bt, tt=tt, reset_first=reset_first)
    vec = lambda: pl.BlockSpec((1, LRU_WIDTH), lambda i, j: (0, 0))
    return pl.pallas_call(
        body,
        grid=(nb // bt, nt),
        in_specs=[
            pl.BlockSpec((rows, LRU_WIDTH), lambda i, j: (i * nt + j, COL_LRU_X)),
            pl.BlockSpec((rows, LRU_WIDTH), lambda i, j: (i * nt + j, COL_LRU_GATE)),
            pl.BlockSpec((bt, CONV_W - 1, LRU_WIDTH), lambda i, j: (i, 0, 0)),
            pl.BlockSpec((bt, 1, LRU_WIDTH), lambda i, j: (i, 0, 0)),
            pl.BlockSpec((CONV_W, LRU_WIDTH), lambda i, j: (0, 0)),
            vec(),
            pl.BlockSpec((LRU_BLOCKS, LRU_BLOCK, 2 * LRU_BLOCK), lambda i, j: (0, 0, 0)),
            vec(), vec(), vec(),
        ],
        out_specs=[
            pl.BlockSpec((rows, LRU_WIDTH), lambda i, j: (i * nt + j, 0)),
            pl.BlockSpec((bt, 1, LRU_WIDTH), lambda i, j: (i, 0, 0)),
        ],
        out_shape=[
            jax.ShapeDtypeStruct((nb * t, LRU_WIDTH), BF16),
            jax.ShapeDtypeStruct((nb, 1, LRU_WIDTH), F32),
        ],
        scratch_shapes=[
            pltpu.VMEM((bt, SUBLANES + tt, LRU_WIDTH), F32),
            pltpu.VMEM((bt, tt, LRU_WIDTH), F32),
            pltpu.VMEM((bt, tt, LRU_WIDTH), F32),
            pltpu.VMEM((bt, 1, LRU_WIDTH), F32),
        ],
        compiler_params=pltpu.CompilerParams(
            dimension_semantics=("arbitrary", "arbitrary"), vmem_limit_bytes=VMEM_LIMIT),
        name="lru",
    )(z, z, conv_state, h0, cw, cb, wax, ba, bx, alog)


def _gdn_body(zq_ref, zk_ref, zv_ref, zg_ref, ba_ref, csq_ref, csk_ref, csv_ref, s0_ref,
              cwq_ref, cwk_ref, cwv_ref, alog_ref, dtb_ref, nw_ref,
              out_ref, sout_ref, xp_scr, s_scr, *, bt, tt, chunk, hg):
    h_grp = pl.program_id(1)
    j = pl.program_id(2)
    rows = bt * tt
    width = LANES * hg
    n_chunks = tt // chunk

    @pl.when(j == 0)
    def _():
        xp_scr[0, :, 5:8, :] = csq_ref[...]
        xp_scr[1, :, 5:8, :] = csk_ref[...]
        xp_scr[2, :, 5:8, :] = csv_ref[...]
        s_scr[...] = s0_ref[...]

    def conv_silu(idx, z_ref, cw_ref):
        xp_scr[idx, :, 8:8 + tt, :] = z_ref[...].reshape(bt, tt, width)
        cw = cw_ref[...]
        y = xp_scr[idx, :, 5:5 + tt, :] * cw[0].reshape(1, 1, width)
        for i in range(1, CONV_W):
            y = y + xp_scr[idx, :, 5 + i:5 + i + tt, :] * cw[i].reshape(1, 1, width)
        xp_scr[idx, :, 5:8, :] = xp_scr[idx, :, 5 + tt:8 + tt, :]
        return _silu(y.reshape(rows, width))

    q_all = conv_silu(0, zq_ref, cwq_ref)
    k_all = conv_silu(1, zk_ref, cwk_ref)
    v_all = conv_silu(2, zv_ref, cwv_ref)

    ba = ba_ref[...]
    beta_all = jax.nn.sigmoid(ba)
    g_all = -jnp.exp(alog_ref[...]) * _softplus(ba + dtb_ref[...])
    row = lax.broadcasted_iota(jnp.int32, (rows, LANES), 0)
    row_in_chunk = row % chunk
    gc_all = g_all
    s = 1
    while s < chunk:
        gc_all = gc_all + jnp.where(row_in_chunk >= s, pltpu.roll(gc_all, s, axis=0), 0.0)
        s *= 2
    gct_all = gc_all.T

    lane = lax.broadcasted_iota(jnp.int32, (rows, LANES), 1)
    sub = lax.broadcasted_iota(jnp.int32, (LANES, rows), 0)
    ri = lax.broadcasted_iota(jnp.int32, (rows, rows), 0)
    ci = lax.broadcasted_iota(jnp.int32, (rows, rows), 1)
    same_chunk = (ri // chunk) == (ci // chunk)
    causal = jnp.logical_and(same_chunk, ri >= ci)
    strict = jnp.logical_and(same_chunk, ri > ci)
    n_steps = (chunk - 1).bit_length()

    for hh in range(hg):
        head = h_grp * hg + hh
        ls = slice(hh * LANES, (hh + 1) * LANES)
        qh, kh, vh = q_all[:, ls], k_all[:, ls], v_all[:, ls]
        qh = qh * lax.rsqrt(jnp.sum(qh * qh, axis=-1, keepdims=True) + NORM_EPS) * (GDN_DK ** -0.5)
        kh = kh * lax.rsqrt(jnp.sum(kh * kh, axis=-1, keepdims=True) + NORM_EPS)

        beta = jnp.sum(jnp.where(lane == BETA_LANE + head, beta_all, 0.0), axis=1, keepdims=True)
        gc = jnp.sum(jnp.where(lane == ALPHA_LANE + head, gc_all, 0.0), axis=1, keepdims=True)
        gc_row = jnp.sum(jnp.where(sub == ALPHA_LANE + head, gct_all, 0.0), axis=0, keepdims=True)

        decay = jnp.where(causal, jnp.exp(jnp.where(causal, gc - gc_row, 0.0)), 0.0)
        kb = kh * beta
        gram = lax.dot_general(jnp.concatenate([kb, qh], axis=0).astype(BF16), kh.astype(BF16),
                               (((1,), (1,)), ((), ())), preferred_element_type=F32)
        p = jnp.where(strict, -(gram[:rows] * decay), 0.0)
        a_qk = gram[rows:] * decay
        e_gc = jnp.exp(gc)
        x = jnp.concatenate([vh * beta, kb * e_gc], axis=1)
        for st in range(n_steps):
            pb = p.astype(BF16)
            if st < n_steps - 1:
                res = jnp.dot(pb, jnp.concatenate([x.astype(BF16), pb], axis=1),
                              preferred_element_type=F32)
                x = x + res[:, :2 * LANES]
                p = res[:, 2 * LANES:]
            else:
                x = x + jnp.dot(pb, x.astype(BF16), preferred_element_type=F32)
        u = x[:, :LANES]
        w = x[:, LANES:]
        q_dec = qh * e_gc

        v_new_parts = []
        qs_parts = []
        for sq in range(bt):
            state = s_scr[sq, hh]
            for c in range(n_chunks):
                r0 = sq * tt + c * chunk
                rs = slice(r0, r0 + chunk)
                ws = _bdot(jnp.concatenate([w[rs], q_dec[rs]], axis=0), state)
                v_new = u[rs] - ws[:chunk]
                g_last = gc[r0 + chunk - 1:r0 + chunk]
                k_dec = kh[rs] * jnp.exp(g_last - gc[rs])
                state = state * jnp.exp(g_last) + lax.dot_general(
                    k_dec.astype(BF16), v_new.astype(BF16), (((0,), (0,)), ((), ())),
                    preferred_element_type=F32)
                v_new_parts.append(v_new)
                qs_parts.append(ws[chunk:])
            s_scr[sq, hh] = state
        v_new_all = jnp.concatenate(v_new_parts, axis=0) if len(v_new_parts) > 1 else v_new_parts[0]
        qs_all = jnp.concatenate(qs_parts, axis=0) if len(qs_parts) > 1 else qs_parts[0]
        o = qs_all + _bdot(a_qk, v_new_all)

        o = o * lax.rsqrt(jnp.mean(o * o, axis=-1, keepdims=True) + NORM_EPS) * nw_ref[...]
        o = o * _silu(zg_ref[:, ls])
        out_ref[:, ls] = o.astype(out_ref.dtype)

    @pl.when(j == pl.num_programs(2) - 1)
    def _():
        sout_ref[...] = s_scr[...]


def _gdn(z, ba, conv_state, s0, cw, alog_pad, dtb_pad, nw, *, nb, t, bt, tt, chunk, hg):
    nt = t // tt
    rows = bt * tt
    width = LANES * hg
    per = GDN_KEY_W // width
    body = functools.partial(_gdn_body, bt=bt, tt=tt, chunk=chunk, hg=hg)

    def zspec(col):
        return pl.BlockSpec((rows, width), lambda i, h, j: (i * nt + j, col * per + h))

    def cs_spec(seg):
        return pl.BlockSpec((bt, CONV_W - 1, width), lambda i, h, j: (i, 0, seg * per + h))

    def cw_spec(seg):
        return pl.BlockSpec((CONV_W, width), lambda i, h, j: (0, seg * per + h))

    vec = lambda: pl.BlockSpec((1, LANES), lambda i, h, j: (0, 0))
    return pl.pallas_call(
        body,
        grid=(nb // bt, GDN_HEADS // hg, nt),
        in_specs=[
            zspec(COL_Q), zspec(COL_K), zspec(COL_V), zspec(COL_GDN_GATE),
            pl.BlockSpec((rows, LANES), lambda i, h, j: (i * nt + j, 0)),
            cs_spec(0), cs_spec(1), cs_spec(2),
            pl.BlockSpec((bt, hg, GDN_DK, GDN_DV), lambda i, h, j: (i, h, 0, 0)),
            cw_spec(0), cw_spec(1), cw_spec(2),
            vec(), vec(), vec(),
        ],
        out_specs=[
            pl.BlockSpec((rows, width), lambda i, h, j: (i * nt + j, h)),
            pl.BlockSpec((bt, hg, GDN_DK, GDN_DV), lambda i, h, j: (i, h, 0, 0)),
        ],
        out_shape=[
            jax.ShapeDtypeStruct((nb * t, GDN_VAL_W), BF16),
            jax.ShapeDtypeStruct((nb, GDN_HEADS, GDN_DK, GDN_DV), F32),
        ],
        scratch_shapes=[
            pltpu.VMEM((3, bt, SUBLANES + tt, width), F32),
            pltpu.VMEM((bt, hg, GDN_DK, GDN_DV), F32),
        ],
        compiler_params=pltpu.CompilerParams(
            dimension_semantics=("arbitrary", "arbitrary", "arbitrary"), vmem_limit_bytes=VMEM_LIMIT),
        name="gdn",
    )(z, z, z, z, ba, conv_state, conv_state, conv_state, s0, cw, cw, cw, alog_pad, dtb_pad, nw)


def _out_body(x_ref, lru_ref, gdn_ref, ml_ref, mg_ref, wl_ref, wg_ref, wo_ref, gain_ref, y_ref):
    p_lru = jnp.dot(lru_ref[...], wl_ref[...], preferred_element_type=F32)
    p_gdn = jnp.dot(gdn_ref[...], wg_ref[...], preferred_element_type=F32)
    merged = jax.nn.sigmoid(ml_ref[...]) * p_lru + jax.nn.sigmoid(mg_ref[...]) * p_gdn
    y = jnp.dot(merged.astype(BF16), wo_ref[...], preferred_element_type=F32)
    ms = jnp.mean(y * y, axis=-1, keepdims=True)
    y_ref[...] = x_ref[...] + (y * lax.rsqrt(ms + NORM_EPS)) * gain_ref[...]


def _out_proj(x2, lru_out, gdn_out, z, wl, wg, wo, gain, *, tm):
    n = x2.shape[0]
    wspec = lambda: pl.BlockSpec((D_MODEL, D_MODEL), lambda i: (0, 0))
    return pl.pallas_call(
        _out_body,
        grid=(n // tm,),
        in_specs=[
            pl.BlockSpec((tm, D_MODEL), lambda i: (i, 0)),
            pl.BlockSpec((tm, LRU_WIDTH), lambda i: (i, 0)),
            pl.BlockSpec((tm, GDN_VAL_W), lambda i: (i, 0)),
            pl.BlockSpec((tm, D_MODEL), lambda i: (i, COL_M_LRU)),
            pl.BlockSpec((tm, D_MODEL), lambda i: (i, COL_M_GDN)),
            wspec(), wspec(), wspec(),
            pl.BlockSpec((1, D_MODEL), lambda i: (0, 0)),
        ],
        out_specs=pl.BlockSpec((tm, D_MODEL), lambda i: (i, 0)),
        out_shape=jax.ShapeDtypeStruct((n, D_MODEL), F32),
        compiler_params=pltpu.CompilerParams(
            dimension_semantics=("arbitrary",), vmem_limit_bytes=VMEM_LIMIT),
        name="out_proj",
    )(x2, lru_out, gdn_out, z, z, wl, wg, wo, gain)


def _prep_weights(norm_pre, norm_post, w_in, lru_conv_w, lru_conv_b, lru_wa, lru_ba, lru_wx, lru_bx,
                  lru_a_logit, gdn_conv_w, gdn_A_log, gdn_dt_bias, gdn_norm_w, w_br_lru, w_br_gdn, w_out):
    c_ba = 6 * 1024
    w_main = jnp.concatenate([w_in[:, :c_ba], w_in[:, c_ba + 2 * GDN_HEADS:]], axis=1).astype(BF16)
    w_ba = jnp.pad(w_in[:, c_ba:c_ba + 2 * GDN_HEADS], ((0, 0), (0, LANES - 2 * GDN_HEADS))).astype(BF16)
    wax = jnp.concatenate([lru_wa, lru_wx], axis=-1).astype(BF16)
    pad_alpha = lambda v: jnp.pad(v.reshape(1, GDN_HEADS), ((0, 0), (ALPHA_LANE, LANES - ALPHA_LANE - GDN_HEADS)))
    row = lambda v: v.reshape(1, -1)
    return dict(
        norm_pre=row(norm_pre), norm_post=row(norm_post), w_main=w_main, w_ba=w_ba,
        lru_cw=lru_conv_w, lru_cb=row(lru_conv_b), wax=wax, lru_ba=row(lru_ba), lru_bx=row(lru_bx),
        lru_alog=row(lru_a_logit), gdn_cw=gdn_conv_w, alog_pad=pad_alpha(gdn_A_log),
        dtb_pad=pad_alpha(gdn_dt_bias), gdn_nw=row(gdn_norm_w),
        wl=w_br_lru.astype(BF16), wg=w_br_gdn.astype(BF16), wo=w_out.astype(BF16))


def _layer(x, lru_conv, lru_h, gdn_conv, gdn_s, p, *, reset_first, tm, lru_bt, lru_tt, gdn_bt, gdn_tt,
           gdn_hg):
    nb, t, _ = x.shape
    x2 = x.reshape(nb * t, D_MODEL)
    z, ba = _in_proj(x2, p["norm_pre"], p["w_main"], p["w_ba"], tm=tm, tn=1024)
    lru_out, h_last = _lru(z, lru_conv, lru_h.reshape(nb, 1, LRU_WIDTH), p["lru_cw"], p["lru_cb"], p["wax"],
                           p["lru_ba"], p["lru_bx"], p["lru_alog"], nb=nb, t=t, bt=lru_bt, tt=lru_tt,
                           reset_first=reset_first)
    gdn_out, s_new = _gdn(z, ba, gdn_conv, gdn_s, p["gdn_cw"], p["alog_pad"], p["dtb_pad"], p["gdn_nw"],
                          nb=nb, t=t, bt=gdn_bt, tt=gdn_tt, chunk=min(GDN_CHUNK, t), hg=gdn_hg)
    y = _out_proj(x2, lru_out, gdn_out, z, p["wl"], p["wg"], p["wo"], p["norm_post"], tm=min(tm, 512))
    z3 = z.reshape(nb, t, Z_WIDTH)
    lru_conv_new = z3[:, t - (CONV_W - 1):, COL_LRU_X * 1024:(COL_LRU_X + 1) * 1024]
    gdn_conv_new = z3[:, t - (CONV_W - 1):, COL_Q * 1024:(COL_V + 1) * 1024]
    return y.reshape(nb, t, D_MODEL), lru_conv_new, h_last.reshape(nb, LRU_WIDTH), gdn_conv_new, s_new


def kernel(x_prompt, x_sample, state_lru_conv, state_lru_h, state_gdn_conv, state_gdn_S, norm_pre, norm_post, w_in, lru_conv_w, lru_conv_b, lru_wa, lru_ba, lru_wx, lru_bx, lru_a_logit, gdn_conv_w, gdn_A_log, gdn_dt_bias, gdn_norm_w, w_br_lru, w_br_gdn, w_out):
    depth = w_in.shape[0]
    assert depth == 1
    nb = x_prompt.shape[0]
    p = _prep_weights(norm_pre[0], norm_post[0], w_in[0], lru_conv_w[0], lru_conv_b[0], lru_wa[0], lru_ba[0],
                      lru_wx[0], lru_bx[0], lru_a_logit[0], gdn_conv_w[0], gdn_A_log[0], gdn_dt_bias[0],
                      gdn_norm_w[0], w_br_lru[0], w_br_gdn[0], w_out[0])
    yp, p_lc, p_lh, p_gc, p_gs = _layer(
        x_prompt,
        jnp.zeros((nb, CONV_W - 1, LRU_WIDTH), F32), jnp.zeros((nb, LRU_WIDTH), F32),
        jnp.zeros((nb, CONV_W - 1, GDN_QKV_W), F32), jnp.zeros((nb, GDN_HEADS, GDN_DK, GDN_DV), F32),
        p, reset_first=True, tm=1024, lru_bt=1, lru_tt=256, gdn_bt=1, gdn_tt=256, gdn_hg=2)
    ys, s_lc, s_lh, s_gc, s_gs = _layer(
        x_sample, state_lru_conv[0], state_lru_h[0], state_gdn_conv[0], state_gdn_S[0],
        p, reset_first=False, tm=1024, lru_bt=32, lru_tt=8, gdn_bt=8, gdn_tt=8, gdn_hg=2)
    return (yp, ys, p_lc[None], p_lh[None], p_gc[None], p_gs[None],
            s_lc[None], s_lh[None], s_gc[None], s_gs[None])
```

```python
import functools

import jax
import jax.numpy as jnp
from jax import lax
from jax.experimental import pallas as pl
from jax.experimental.pallas import tpu as pltpu

F32 = jnp.float32
BF16 = jnp.bfloat16

D_MODEL = 1024
CONV_W = 4
LRU_WIDTH = 1024
LRU_BLOCKS = 8
LRU_BLOCK = LRU_WIDTH // LRU_BLOCKS
LRU_C = 8.0
GDN_HEADS = 8
GDN_DK = 128
GDN_DV = 128
GDN_KEY_W = GDN_HEADS * GDN_DK
GDN_VAL_W = GDN_HEADS * GDN_DV
GDN_QKV_W = 2 * GDN_KEY_W + GDN_VAL_W
GDN_CHUNK = 64
NORM_EPS = 1e-6
PAST_LEN = 16384

LANES = 128
SUBLANES = 8
Z_WIDTH = 8192
COL_LRU_X, COL_LRU_GATE, COL_Q, COL_K, COL_V, COL_GDN_GATE, COL_M_LRU, COL_M_GDN = range(8)
BETA_LANE = 0
ALPHA_LANE = GDN_HEADS
VMEM_LIMIT = 48 * 1024 * 1024


def _softplus(x):
    return jnp.maximum(x, 0.0) + jnp.log1p(jnp.exp(-jnp.abs(x)))


def _silu(x):
    return x * jax.nn.sigmoid(x)


def _bdot(a, b):
    return jnp.dot(a, b, preferred_element_type=F32)


def _in_proj_body(x_ref, gain_ref, w_ref, wba_ref, z_ref, ba_ref, u_scr):
    @pl.when(pl.program_id(1) == 0)
    def _():
        x = x_ref[...]
        ms = jnp.mean(x * x, axis=-1, keepdims=True)
        u = ((x * lax.rsqrt(ms + NORM_EPS)) * gain_ref[...]).astype(BF16)
        u_scr[...] = u
        ba_ref[...] = jnp.dot(u, wba_ref[...], preferred_element_type=F32)

    z_ref[...] = jnp.dot(u_scr[...], w_ref[...], preferred_element_type=F32)


def _in_proj(x2, gain, w_main, w_ba, *, tm, tn):
    n = x2.shape[0]
    return pl.pallas_call(
        _in_proj_body,
        grid=(n // tm, Z_WIDTH // tn),
        in_specs=[
            pl.BlockSpec((tm, D_MODEL), lambda i, j: (i, 0)),
            pl.BlockSpec((1, D_MODEL), lambda i, j: (0, 0)),
            pl.BlockSpec((D_MODEL, tn), lambda i, j: (0, j)),
            pl.BlockSpec((D_MODEL, LANES), lambda i, j: (0, 0)),
        ],
        out_specs=[
            pl.BlockSpec((tm, tn), lambda i, j: (i, j)),
            pl.BlockSpec((tm, LANES), lambda i, j: (i, 0)),
        ],
        out_shape=[
            jax.ShapeDtypeStruct((n, Z_WIDTH), F32),
            jax.ShapeDtypeStruct((n, LANES), F32),
        ],
        scratch_shapes=[pltpu.VMEM((tm, D_MODEL), BF16)],
        compiler_params=pltpu.CompilerParams(
            dimension_semantics=("arbitrary", "arbitrary"), vmem_limit_bytes=VMEM_LIMIT),
        name="in_proj",
    )(x2, gain, w_main, w_ba)


def _lru_body(x_ref, gate_ref, cs_ref, h0_ref, cw_ref, cb_ref, wax_ref, ba_ref, bx_ref, alog_ref,
              out_ref, hout_ref, xp_scr, a_scr, b_scr, h_scr, *, bt, tt, reset_first):
    j = pl.program_id(1)
    rows = bt * tt
    width = LRU_WIDTH

    @pl.when(j == 0)
    def _():
        xp_scr[:, 5:8, :] = cs_ref[...]
        h_scr[...] = h0_ref[...]

    cur = x_ref[...].reshape(bt, tt, width)
    xp_scr[:, 8:8 + tt, :] = cur
    cw = cw_ref[...]
    xc = cur * cw[CONV_W - 1].reshape(1, 1, width)
    for i in range(CONV_W - 2, -1, -1):
        xc = xc + xp_scr[:, 5 + i:5 + i + tt, :] * cw[i].reshape(1, 1, width)
    xp_scr[:, 5:8, :] = xp_scr[:, 5 + tt:8 + tt, :]
    xc = (xc + cb_ref[...].reshape(1, 1, width)).reshape(rows, width)

    a_l = alog_ref[...]
    log_sig_a = jnp.minimum(a_l, 0.0) - jnp.log1p(jnp.exp(-jnp.abs(a_l)))

    row = lax.broadcasted_iota(jnp.int32, (rows, LRU_BLOCK), 0)
    row_in_group = row % SUBLANES
    if reset_first:
        is_reset = jnp.logical_and(row % tt == 0, j == 0)

    for g in range(LRU_BLOCKS):
        gs = slice(g * LRU_BLOCK, (g + 1) * LRU_BLOCK)
        xg = xc[:, gs]
        pre = jnp.dot(xg.astype(BF16), wax_ref[g], preferred_element_type=F32)
        r = jax.nn.sigmoid(pre[:, :LRU_BLOCK] + ba_ref[:, gs])
        ig = jax.nn.sigmoid(pre[:, LRU_BLOCK:] + bx_ref[:, gs])
        log_a = (LRU_C * r) * log_sig_a[:, gs]
        a = jnp.exp(log_a)
        t = jnp.tanh(log_a)
        mult = jnp.sqrt((-2.0 * t) / (1.0 - t))
        if reset_first:
            a = jnp.where(is_reset, 0.0, a)
            mult = jnp.where(is_reset, 1.0, mult)
        b = mult * ig * xg
        s = 1
        while s < SUBLANES:
            keep = row_in_group >= s
            a_sh = jnp.where(keep, pltpu.roll(a, s, axis=0), 1.0)
            b_sh = jnp.where(keep, pltpu.roll(b, s, axis=0), 0.0)
            b = a * b_sh + b
            a = a * a_sh
            s *= 2
        a_scr[:, :, gs] = a.reshape(bt, tt, LRU_BLOCK)
        b_scr[:, :, gs] = b.reshape(bt, tt, LRU_BLOCK)

    def group_step(i, carry):
        r0 = pl.multiple_of(i * SUBLANES, SUBLANES)
        h = b_scr[:, pl.ds(r0, SUBLANES), :] + a_scr[:, pl.ds(r0, SUBLANES), :] * carry
        b_scr[:, pl.ds(r0, SUBLANES), :] = h
        return h[:, SUBLANES - 1:SUBLANES, :]

    n_groups = tt // SUBLANES
    if n_groups == 1:
        carry = group_step(0, h_scr[...])
    else:
        carry = lax.fori_loop(0, n_groups, group_step, h_scr[...])
    h_scr[...] = carry

    h = b_scr[...].reshape(rows, width)
    out_ref[...] = (h * _silu(gate_ref[...])).astype(out_ref.dtype)

    @pl.when(j == pl.num_programs(1) - 1)
    def _():
        hout_ref[...] = carry


def _lru(z, conv_state, h0, cw, cb, wax, ba, bx, alog, *, nb, t, bt, tt, reset_first):
    nt = t // tt
    rows = bt * tt
    body = functools.partial(_lru_body, bt=bt, tt=tt, reset_first=reset_first)
    vec = lambda: pl.BlockSpec((1, LRU_WIDTH), lambda i, j: (0, 0))
    return pl.pallas_call(
        body,
        grid=(nb // bt, nt),
        in_specs=[
            pl.BlockSpec((rows, LRU_WIDTH), lambda i, j: (i * nt + j, COL_LRU_X)),
            pl.BlockSpec((rows, LRU_WIDTH), lambda i, j: (i * nt + j, COL_LRU_GATE)),
            pl.BlockSpec((bt, CONV_W - 1, LRU_WIDTH), lambda i, j: (i, 0, 0)),
            pl.BlockSpec((bt, 1, LRU_WIDTH), lambda i, j: (i, 0, 0)),
            pl.BlockSpec((CONV_W, LRU_WIDTH), lambda i, j: (0, 0)),
            vec(),
            pl.BlockSpec((LRU_BLOCKS, LRU_BLOCK, 2 * LRU_BLOCK), lambda i, j: (0, 0, 0)),
            vec(), vec(), vec(),
        ],
        out_specs=[
            pl.BlockSpec((rows, LRU_WIDTH), lambda i, j: (i * nt + j, 0)),
            pl.BlockSpec((bt, 1, LRU_WIDTH), lambda i, j: (i, 0, 0)),
        ],
        out_shape=[
            jax.ShapeDtypeStruct((nb * t, LRU_WIDTH), BF16),
            jax.ShapeDtypeStruct((nb, 1, LRU_WIDTH), F32),
        ],
        scratch_shapes=[
            pltpu.VMEM((bt, SUBLANES + tt, LRU_WIDTH), F32),
            pltpu.VMEM((bt, tt, LRU_WIDTH), F32),
            pltpu.VMEM((bt, tt, LRU_WIDTH), F32),
            pltpu.VMEM((bt, 1, LRU_WIDTH), F32),
        ],
        compiler_params=pltpu.CompilerParams(
            dimension_semantics=("arbitrary", "arbitrary"), vmem_limit_bytes=VMEM_LIMIT),
        name="lru",
    )(z, z, conv_state, h0, cw, cb, wax, ba, bx, alog)


def _gdn_body(zq_ref, zk_ref, zv_ref, zg_ref, ba_ref, csq_ref, csk_ref, csv_ref, s0_ref,
              cwq_ref, cwk_ref, cwv_ref, alog_ref, dtb_ref, nw_ref,
              out_ref, sout_ref, xp_scr, s_scr, *, bt, tt, chunk, hg):
    h_grp = pl.program_id(1)
    j = pl.program_id(2)
    rows = bt * tt
    width = LANES * hg
    n_chunks = tt // chunk

    @pl.when(j == 0)
    def _():
        xp_scr[0, :, 5:8, :] = csq_ref[...]
        xp_scr[1, :, 5:8, :] = csk_ref[...]
        xp_scr[2, :, 5:8, :] = csv_ref[...]
        s_scr[...] = s0_ref[...]

    def conv_silu(idx, z_ref, cw_ref):
        cur = z_ref[...].reshape(bt, tt, width)
        xp_scr[idx, :, 8:8 + tt, :] = cur
        cw = cw_ref[...]
        y = cur * cw[CONV_W - 1].reshape(1, 1, width)
        for i in range(CONV_W - 2, -1, -1):
            y = y + xp_scr[idx, :, 5 + i:5 + i + tt, :] * cw[i].reshape(1, 1, width)
        xp_scr[idx, :, 5:8, :] = xp_scr[idx, :, 5 + tt:8 + tt, :]
        return _silu(y.reshape(rows, width))

    q_all = conv_silu(0, zq_ref, cwq_ref)
    k_all = conv_silu(1, zk_ref, cwk_ref)
    v_all = conv_silu(2, zv_ref, cwv_ref)

    ba = ba_ref[...]
    beta_all = jax.nn.sigmoid(ba)
    g_all = -jnp.exp(alog_ref[...]) * _softplus(ba + dtb_ref[...])
    row = lax.broadcasted_iota(jnp.int32, (rows, LANES), 0)
    row_in_chunk = row % chunk
    gc_all = g_all
    s = 1
    while s < chunk:
        gc_all = gc_all + jnp.where(row_in_chunk >= s, pltpu.roll(gc_all, s, axis=0), 0.0)
        s *= 2
    gct_all = gc_all.T

    lane = lax.broadcasted_iota(jnp.int32, (rows, LANES), 1)
    sub = lax.broadcasted_iota(jnp.int32, (LANES, rows), 0)
    ri = lax.broadcasted_iota(jnp.int32, (rows, rows), 0)
    ci = lax.broadcasted_iota(jnp.int32, (rows, rows), 1)
    same_chunk = (ri // chunk) == (ci // chunk)
    causal = jnp.logical_and(same_chunk, ri >= ci)
    strict = jnp.logical_and(same_chunk, ri > ci)
    n_steps = (chunk - 1).bit_length()

    hd = []
    for hh in range(hg):
        head = h_grp * hg + hh
        ls = slice(hh * LANES, (hh + 1) * LANES)
        qh, kh, vh = q_all[:, ls], k_all[:, ls], v_all[:, ls]
        qh = qh * lax.rsqrt(jnp.sum(qh * qh, axis=-1, keepdims=True) + NORM_EPS) * (GDN_DK ** -0.5)
        kh = kh * lax.rsqrt(jnp.sum(kh * kh, axis=-1, keepdims=True) + NORM_EPS)

        beta = jnp.sum(jnp.where(lane == BETA_LANE + head, beta_all, 0.0), axis=1, keepdims=True)
        gc = jnp.sum(jnp.where(lane == ALPHA_LANE + head, gc_all, 0.0), axis=1, keepdims=True)
        gc_row = jnp.sum(jnp.where(sub == ALPHA_LANE + head, gct_all, 0.0), axis=0, keepdims=True)

        decay = jnp.where(causal, jnp.exp(jnp.where(causal, gc - gc_row, 0.0)), 0.0)
        kb = kh * beta
        gram = lax.dot_general(jnp.concatenate([kb, qh], axis=0), kh,
                               (((1,), (1,)), ((), ())), preferred_element_type=F32)
        e_gc = jnp.exp(gc)
        hd.append(dict(
            ls=ls, kh=kh, gc=gc,
            p=jnp.where(strict, -(gram[:rows] * decay), 0.0),
            a_qk=gram[rows:] * decay,
            x=jnp.concatenate([vh * beta, kb * e_gc], axis=1),
            q_dec=qh * e_gc))

    for st in range(n_steps):
        for d in hd:
            if st < n_steps - 1:
                res = jnp.dot(d["p"], jnp.concatenate([d["x"], d["p"]], axis=1), preferred_element_type=F32)
                d["x"] = d["x"] + res[:, :2 * LANES]
                d["p"] = res[:, 2 * LANES:]
            else:
                d["x"] = d["x"] + jnp.dot(d["p"], d["x"], preferred_element_type=F32)

    for d in hd:
        d["v_new"] = []
        d["qs"] = []
    for sq in range(bt):
        states = [s_scr[sq, hh] for hh in range(hg)]
        for c in range(n_chunks):
            r0 = sq * tt + c * chunk
            rs = slice(r0, r0 + chunk)
            for hh, d in enumerate(hd):
                u, w = d["x"][rs, :LANES], d["x"][rs, LANES:]
                ws = _bdot(jnp.concatenate([w, d["q_dec"][rs]], axis=0), states[hh])
                v_new = u - ws[:chunk]
                g_last = d["gc"][r0 + chunk - 1:r0 + chunk]
                k_dec = d["kh"][rs] * jnp.exp(g_last - d["gc"][rs])
                states[hh] = states[hh] * jnp.exp(g_last) + lax.dot_general(
                    k_dec, v_new, (((0,), (0,)), ((), ())), preferred_element_type=F32)
                d["v_new"].append(v_new)
                d["qs"].append(ws[chunk:])
        for hh in range(hg):
            s_scr[sq, hh] = states[hh]

    for d in hd:
        v_new_all = jnp.concatenate(d["v_new"], axis=0) if len(d["v_new"]) > 1 else d["v_new"][0]
        qs_all = jnp.concatenate(d["qs"], axis=0) if len(d["qs"]) > 1 else d["qs"][0]
        o = qs_all + _bdot(d["a_qk"], v_new_all)
        o = o * lax.rsqrt(jnp.mean(o * o, axis=-1, keepdims=True) + NORM_EPS) * nw_ref[...]
        o = o * _silu(zg_ref[:, d["ls"]])
        out_ref[:, d["ls"]] = o.astype(out_ref.dtype)

    @pl.when(j == pl.num_programs(2) - 1)
    def _():
        sout_ref[...] = s_scr[...]


def _gdn(z, ba, conv_state, s0, cw, alog_pad, dtb_pad, nw, *, nb, t, bt, tt, chunk, hg):
    nt = t // tt
    rows = bt * tt
    width = LANES * hg
    per = GDN_KEY_W // width
    body = functools.partial(_gdn_body, bt=bt, tt=tt, chunk=chunk, hg=hg)

    def zspec(col):
        return pl.BlockSpec((rows, width), lambda i, h, j: (i * nt + j, col * per + h))

    def cs_spec(seg):
        return pl.BlockSpec((bt, CONV_W - 1, width), lambda i, h, j: (i, 0, seg * per + h))

    def cw_spec(seg):
        return pl.BlockSpec((CONV_W, width), lambda i, h, j: (0, seg * per + h))

    vec = lambda: pl.BlockSpec((1, LANES), lambda i, h, j: (0, 0))
    return pl.pallas_call(
        body,
        grid=(nb // bt, GDN_HEADS // hg, nt),
        in_specs=[
            zspec(COL_Q), zspec(COL_K), zspec(COL_V), zspec(COL_GDN_GATE),
            pl.BlockSpec((rows, LANES), lambda i, h, j: (i * nt + j, 0)),
            cs_spec(0), cs_spec(1), cs_spec(2),
            pl.BlockSpec((bt, hg, GDN_DK, GDN_DV), lambda i, h, j: (i, h, 0, 0)),
            cw_spec(0), cw_spec(1), cw_spec(2),
            vec(), vec(), vec(),
        ],
        out_specs=[
            pl.BlockSpec((rows, width), lambda i, h, j: (i * nt + j, h)),
            pl.BlockSpec((bt, hg, GDN_DK, GDN_DV), lambda i, h, j: (i, h, 0, 0)),
        ],
        out_shape=[
            jax.ShapeDtypeStruct((nb * t, GDN_VAL_W), BF16),
            jax.ShapeDtypeStruct((nb, GDN_HEADS, GDN_DK, GDN_DV), F32),
        ],
        scratch_shapes=[
            pltpu.VMEM((3, bt, SUBLANES + tt, width), F32),
            pltpu.VMEM((bt, hg, GDN_DK, GDN_DV), F32),
        ],
        compiler_params=pltpu.CompilerParams(
            dimension_semantics=("arbitrary", "arbitrary", "arbitrary"), vmem_limit_bytes=VMEM_LIMIT),
        name="gdn",
    )(z, z, z, z, ba, conv_state, conv_state, conv_state, s0, cw, cw, cw, alog_pad, dtb_pad, nw)


def _out_body(x_ref, lru_ref, gdn_ref, ml_ref, mg_ref, wl_ref, wg_ref, wo_ref, gain_ref, y_ref):
    p_lru = jnp.dot(lru_ref[...], wl_ref[...], preferred_element_type=F32)
    p_gdn = jnp.dot(gdn_ref[...], wg_ref[...], preferred_element_type=F32)
    merged = jax.nn.sigmoid(ml_ref[...]) * p_lru + jax.nn.sigmoid(mg_ref[...]) * p_gdn
    y = jnp.dot(merged.astype(BF16), wo_ref[...], preferred_element_type=F32)
    ms = jnp.mean(y * y, axis=-1, keepdims=True)
    y_ref[...] = x_ref[...] + (y * lax.rsqrt(ms + NORM_EPS)) * gain_ref[...]


def _out_proj(x2, lru_out, gdn_out, z, wl, wg, wo, gain, *, tm):
    n = x2.shape[0]
    wspec = lambda: pl.BlockSpec((D_MODEL, D_MODEL), lambda i: (0, 0))
    return pl.pallas_call(
        _out_body,
        grid=(n // tm,),
        in_specs=[
            pl.BlockSpec((tm, D_MODEL), lambda i: (i, 0)),
            pl.BlockSpec((tm, LRU_WIDTH), lambda i: (i, 0)),
            pl.BlockSpec((tm, GDN_VAL_W), lambda i: (i, 0)),
            pl.BlockSpec((tm, D_MODEL), lambda i: (i, COL_M_LRU)),
            pl.BlockSpec((tm, D_MODEL), lambda i: (i, COL_M_GDN)),
            wspec(), wspec(), wspec(),
            pl.BlockSpec((1, D_MODEL), lambda i: (0, 0)),
        ],
        out_specs=pl.BlockSpec((tm, D_MODEL), lambda i: (i, 0)),
        out_shape=jax.ShapeDtypeStruct((n, D_MODEL), F32),
        compiler_params=pltpu.CompilerParams(
            dimension_semantics=("arbitrary",), vmem_limit_bytes=VMEM_LIMIT),
        name="out_proj",
    )(x2, lru_out, gdn_out, z, z, wl, wg, wo, gain)


def _prep_weights(norm_pre, norm_post, w_in, lru_conv_w, lru_conv_b, lru_wa, lru_ba, lru_wx, lru_bx,
                  lru_a_logit, gdn_conv_w, gdn_A_log, gdn_dt_bias, gdn_norm_w, w_br_lru, w_br_gdn, w_out):
    c_ba = 6 * 1024
    w_main = jnp.concatenate([w_in[:, :c_ba], w_in[:, c_ba + 2 * GDN_HEADS:]], axis=1).astype(BF16)
    w_ba = jnp.pad(w_in[:, c_ba:c_ba + 2 * GDN_HEADS], ((0, 0), (0, LANES - 2 * GDN_HEADS))).astype(BF16)
    wax = jnp.concatenate([lru_wa, lru_wx], axis=-1).astype(BF16)
    pad_alpha = lambda v: jnp.pad(v.reshape(1, GDN_HEADS), ((0, 0), (ALPHA_LANE, LANES - ALPHA_LANE - GDN_HEADS)))
    row = lambda v: v.reshape(1, -1)
    return dict(
        norm_pre=row(norm_pre), norm_post=row(norm_post), w_main=w_main, w_ba=w_ba,
        lru_cw=lru_conv_w, lru_cb=row(lru_conv_b), wax=wax, lru_ba=row(lru_ba), lru_bx=row(lru_bx),
        lru_alog=row(lru_a_logit), gdn_cw=gdn_conv_w, alog_pad=pad_alpha(gdn_A_log),
        dtb_pad=pad_alpha(gdn_dt_bias), gdn_nw=row(gdn_norm_w),
        wl=w_br_lru.astype(BF16), wg=w_br_gdn.astype(BF16), wo=w_out.astype(BF16))


def _layer(x, lru_conv, lru_h, gdn_conv, gdn_s, p, *, reset_first, tm, lru_bt, lru_tt, gdn_bt, gdn_tt,
           gdn_hg):
    nb, t, _ = x.shape
    x2 = x.reshape(nb * t, D_MODEL)
    z, ba = _in_proj(x2, p["norm_pre"], p["w_main"], p["w_ba"], tm=tm, tn=1024)
    lru_out, h_last = _lru(z, lru_conv, lru_h.reshape(nb, 1, LRU_WIDTH), p["lru_cw"], p["lru_cb"], p["wax"],
                           p["lru_ba"], p["lru_bx"], p["lru_alog"], nb=nb, t=t, bt=lru_bt, tt=lru_tt,
                           reset_first=reset_first)
    gdn_out, s_new = _gdn(z, ba, gdn_conv, gdn_s, p["gdn_cw"], p["alog_pad"], p["dtb_pad"], p["gdn_nw"],
                          nb=nb, t=t, bt=gdn_bt, tt=gdn_tt, chunk=min(GDN_CHUNK, t), hg=gdn_hg)
    y = _out_proj(x2, lru_out, gdn_out, z, p["wl"], p["wg"], p["wo"], p["norm_post"], tm=min(tm, 512))
    z3 = z.reshape(nb, t, Z_WIDTH)
    lru_conv_new = z3[:, t - (CONV_W - 1):, COL_LRU_X * 1024:(COL_LRU_X + 1) * 1024]
    gdn_conv_new = z3[:, t - (CONV_W - 1):, COL_Q * 1024:(COL_V + 1) * 1024]
    return y.reshape(nb, t, D_MODEL), lru_conv_new, h_last.reshape(nb, LRU_WIDTH), gdn_conv_new, s_new


def kernel(x_prompt, x_sample, state_lru_conv, state_lru_h, state_gdn_conv, state_gdn_S, norm_pre, norm_post, w_in, lru_conv_w, lru_conv_b, lru_wa, lru_ba, lru_wx, lru_bx, lru_a_logit, gdn_conv_w, gdn_A_log, gdn_dt_bias, gdn_norm_w, w_br_lru, w_br_gdn, w_out):
    depth = w_in.shape[0]
    assert depth == 1
    nb = x_prompt.shape[0]
    p = _prep_weights(norm_pre[0], norm_post[0], w_in[0], lru_conv_w[0], lru_conv_b[0], lru_wa[0], lru_ba[0],
                      lru_wx[0], lru_bx[0], lru_a_logit[0], gdn_conv_w[0], gdn_A_log[0], gdn_dt_bias[0],
                      gdn_norm_w[0], w_br_lru[0], w_br_gdn[0], w_out[0])
    yp, p_lc, p_lh, p_gc, p_gs = _layer(
        x_prompt,
        jnp.zeros((nb, CONV_W - 1, LRU_WIDTH), F32), jnp.zeros((nb, LRU_WIDTH), F32),
        jnp.zeros((nb, CONV_W - 1, GDN_QKV_W), F32), jnp.zeros((nb, GDN_HEADS, GDN_DK, GDN_DV), F32),
        p, reset_first=True, tm=1024, lru_bt=1, lru_tt=256, gdn_bt=1, gdn_tt=256, gdn_hg=8)
    ys, s_lc, s_lh, s_gc, s_gs = _layer(
        x_sample, state_lru_conv[0], state_lru_h[0], state_gdn_conv[0], state_gdn_S[0],
        p, reset_first=False, tm=1024, lru_bt=32, lru_tt=8, gdn_bt=8, gdn_tt=8, gdn_hg=2)
    return (yp, ys, p_lc[None], p_lh[None], p_gc[None], p_gs[None],
            s_lc[None], s_lh[None], s_gc[None], s_gs[None])
```

```python
import functools

import jax
import jax.numpy as jnp
from jax import lax
from jax.experimental import pallas as pl
from jax.experimental.pallas import tpu as pltpu

F32 = jnp.float32
BF16 = jnp.bfloat16

D_MODEL = 1024
CONV_W = 4
LRU_WIDTH = 1024
LRU_BLOCKS = 8
LRU_BLOCK = LRU_WIDTH // LRU_BLOCKS
LRU_C = 8.0
GDN_HEADS = 8
GDN_DK = 128
GDN_DV = 128
GDN_KEY_W = GDN_HEADS * GDN_DK
GDN_VAL_W = GDN_HEADS * GDN_DV
GDN_QKV_W = 2 * GDN_KEY_W + GDN_VAL_W
GDN_CHUNK = 64
NORM_EPS = 1e-6

LANES = 128
SUBLANES = 8
SEG = 1024
GDN_Z_WIDTH = 4 * SEG
COL_Q, COL_K, COL_V, COL_GDN_GATE = range(4)
BETA_LANE = 0
ALPHA_LANE = GDN_HEADS
VMEM_LIMIT = 48 * 1024 * 1024


def _softplus(x):
    return jnp.maximum(x, 0.0) + jnp.log1p(jnp.exp(-jnp.abs(x)))


def _silu(x):
    return x * jax.nn.sigmoid(x)


def _fdot(a, b):
    return jnp.dot(a, b, preferred_element_type=F32)


def _rms_norm_bf16(x, gain):
    ms = jnp.mean(x * x, axis=-1, keepdims=True)
    return ((x * lax.rsqrt(ms + NORM_EPS)) * gain).astype(BF16)


def _in_proj_body(x_ref, gain_ref, w_ref, wba_ref, z_ref, ba_ref, u_scr):
    @pl.when(pl.program_id(1) == 0)
    def _():
        u = _rms_norm_bf16(x_ref[...], gain_ref[...])
        u_scr[...] = u
        ba_ref[...] = jnp.dot(u, wba_ref[...], preferred_element_type=F32)

    z_ref[...] = jnp.dot(u_scr[...], w_ref[...], preferred_element_type=F32)


def _in_proj(x2, gain, w, w_ba, *, tm, tn):
    n = x2.shape[0]
    width = w.shape[1]
    return pl.pallas_call(
        _in_proj_body,
        grid=(n // tm, width // tn),
        in_specs=[
            pl.BlockSpec((tm, D_MODEL), lambda i, j: (i, 0)),
            pl.BlockSpec((1, D_MODEL), lambda i, j: (0, 0)),
            pl.BlockSpec((D_MODEL, tn), lambda i, j: (0, j)),
            pl.BlockSpec((D_MODEL, LANES), lambda i, j: (0, 0)),
        ],
        out_specs=[
            pl.BlockSpec((tm, tn), lambda i, j: (i, j)),
            pl.BlockSpec((tm, LANES), lambda i, j: (i, 0)),
        ],
        out_shape=[
            jax.ShapeDtypeStruct((n, width), F32),
            jax.ShapeDtypeStruct((n, LANES), F32),
        ],
        scratch_shapes=[pltpu.VMEM((tm, D_MODEL), BF16)],
        compiler_params=pltpu.CompilerParams(
            dimension_semantics=("arbitrary", "arbitrary"), vmem_limit_bytes=VMEM_LIMIT),
        name="in_proj",
    )(x2, gain, w, w_ba)


def _lru_body(x_ref, gain_ref, w_ref, cs_ref, h0_ref, cw_ref, cb_ref, wax_ref, ba_ref, bx_ref, alog_ref,
              out_ref, hout_ref, csout_ref, xp_scr, a_scr, b_scr, h_scr, *, bt, tt, reset_first):
    j = pl.program_id(1)
    rows = bt * tt
    width = LRU_WIDTH

    @pl.when(j == 0)
    def _():
        xp_scr[:, 5:8, :] = cs_ref[...]
        h_scr[...] = h0_ref[...]

    u = _rms_norm_bf16(x_ref[...], gain_ref[...])
    z = jnp.dot(u, w_ref[...], preferred_element_type=F32)
    gate = z[:, width:]

    cur = z[:, :width].reshape(bt, tt, width)
    xp_scr[:, 8:8 + tt, :] = cur
    cw = cw_ref[...]
    xc = cur * cw[CONV_W - 1].reshape(1, 1, width)
    for i in range(CONV_W - 2, -1, -1):
        xc = xc + xp_scr[:, 5 + i:5 + i + tt, :] * cw[i].reshape(1, 1, width)
    xp_scr[:, 5:8, :] = xp_scr[:, 5 + tt:8 + tt, :]
    xc = (xc + cb_ref[...].reshape(1, 1, width)).reshape(rows, width)

    a_l = alog_ref[...]
    log_sig_a = jnp.minimum(a_l, 0.0) - jnp.log1p(jnp.exp(-jnp.abs(a_l)))

    n_vreg_rows = rows // SUBLANES
    sub3 = lax.broadcasted_iota(jnp.int32, (n_vreg_rows, SUBLANES, LRU_BLOCK), 1)
    if reset_first:
        row = lax.broadcasted_iota(jnp.int32, (rows, LRU_BLOCK), 0)
        is_reset = jnp.logical_and(row % tt == 0, j == 0)

    for g in range(LRU_BLOCKS):
        gs = slice(g * LRU_BLOCK, (g + 1) * LRU_BLOCK)
        xg = xc[:, gs]
        pre = jnp.dot(xg.astype(BF16), wax_ref[g], preferred_element_type=F32)
        r = jax.nn.sigmoid(pre[:, :LRU_BLOCK] + ba_ref[:, gs])
        ig = jax.nn.sigmoid(pre[:, LRU_BLOCK:] + bx_ref[:, gs])
        log_a = (LRU_C * r) * log_sig_a[:, gs]
        a = jnp.exp(log_a)
        t = jnp.tanh(log_a)
        m2 = (-2.0 * t) / (1.0 - t)
        mult = jnp.where(m2 > 0.0, m2 * lax.rsqrt(m2), 0.0)
        if reset_first:
            a = jnp.where(is_reset, 0.0, a)
            mult = jnp.where(is_reset, 1.0, mult)
        b = mult * ig * xg
        a = a.reshape(n_vreg_rows, SUBLANES, LRU_BLOCK)
        b = b.reshape(n_vreg_rows, SUBLANES, LRU_BLOCK)
        s = 1
        while s < SUBLANES:
            keep = sub3 >= s
            a_sh = jnp.where(keep, pltpu.roll(a, s, axis=1), 1.0)
            b_sh = jnp.where(keep, pltpu.roll(b, s, axis=1), 0.0)
            b = a * b_sh + b
            a = a * a_sh
            s *= 2
        a_scr[:, :, gs] = a.reshape(bt, tt, LRU_BLOCK)
        b_scr[:, :, gs] = b.reshape(bt, tt, LRU_BLOCK)

    def group_step(i, carry):
        r0 = pl.multiple_of(i * SUBLANES, SUBLANES)
        h = b_scr[:, pl.ds(r0, SUBLANES), :] + a_scr[:, pl.ds(r0, SUBLANES), :] * carry
        b_scr[:, pl.ds(r0, SUBLANES), :] = h
        return h[:, SUBLANES - 1:SUBLANES, :]

    n_groups = tt // SUBLANES
    if n_groups == 1:
        carry = group_step(0, h_scr[...])
    else:
        carry = lax.fori_loop(0, n_groups, group_step, h_scr[...])
    h_scr[...] = carry

    h = b_scr[...].reshape(rows, width)
    out_ref[...] = (h * _silu(gate)).astype(out_ref.dtype)

    @pl.when(j == pl.num_programs(1) - 1)
    def _():
        hout_ref[...] = carry
        csout_ref[...] = xp_scr[:, 5:8, :]


def _lru(x2, gain, w_lru, conv_state, h0, cw, cb, wax, ba, bx, alog, *, nb, t, bt, tt, reset_first):
    nt = t // tt
    rows = bt * tt
    body = functools.partial(_lru_body, bt=bt, tt=tt, reset_first=reset_first)
    vec = lambda: pl.BlockSpec((1, LRU_WIDTH), lambda i, j: (0, 0))
    return pl.pallas_call(
        body,
        grid=(nb // bt, nt),
        in_specs=[
            pl.BlockSpec((rows, D_MODEL), lambda i, j: (i * nt + j, 0)),
            vec(),
            pl.BlockSpec((D_MODEL, 2 * LRU_WIDTH), lambda i, j: (0, 0)),
            pl.BlockSpec((bt, CONV_W - 1, LRU_WIDTH), lambda i, j: (i, 0, 0)),
            pl.BlockSpec((bt, 1, LRU_WIDTH), lambda i, j: (i, 0, 0)),
            pl.BlockSpec((CONV_W, LRU_WIDTH), lambda i, j: (0, 0)),
            vec(),
            pl.BlockSpec((LRU_BLOCKS, LRU_BLOCK, 2 * LRU_BLOCK), lambda i, j: (0, 0, 0)),
            vec(), vec(), vec(),
        ],
        out_specs=[
            pl.BlockSpec((rows, LRU_WIDTH), lambda i, j: (i * nt + j, 0)),
            pl.BlockSpec((bt, 1, LRU_WIDTH), lambda i, j: (i, 0, 0)),
            pl.BlockSpec((bt, CONV_W - 1, LRU_WIDTH), lambda i, j: (i, 0, 0)),
        ],
        out_shape=[
            jax.ShapeDtypeStruct((nb * t, LRU_WIDTH), BF16),
            jax.ShapeDtypeStruct((nb, 1, LRU_WIDTH), F32),
            jax.ShapeDtypeStruct((nb, CONV_W - 1, LRU_WIDTH), F32),
        ],
        scratch_shapes=[
            pltpu.VMEM((bt, SUBLANES + tt, LRU_WIDTH), F32),
            pltpu.VMEM((bt, tt, LRU_WIDTH), F32),
            pltpu.VMEM((bt, tt, LRU_WIDTH), F32),
            pltpu.VMEM((bt, 1, LRU_WIDTH), F32),
        ],
        compiler_params=pltpu.CompilerParams(
            dimension_semantics=("arbitrary", "arbitrary"), vmem_limit_bytes=VMEM_LIMIT),
        name="lru",
    )(x2, gain, w_lru, conv_state, h0, cw, cb, wax, ba, bx, alog)


def _gdn_core(zq, zk, zv, zgate, ba, cw_refs, alog_ref, dtb_ref, nw_ref, out_ref, xp_scr, s_scr,
              h_grp, *, bt, tt, chunk, hg):
    rows = bt * tt
    width = LANES * hg
    n_chunks = tt // chunk

    def conv_silu(idx, cur2, cw_ref):
        cur = cur2.reshape(bt, tt, width)
        xp_scr[idx, :, 8:8 + tt, :] = cur
        cw = cw_ref[...]
        y = cur * cw[CONV_W - 1].reshape(1, 1, width)
        for i in range(CONV_W - 2, -1, -1):
            y = y + xp_scr[idx, :, 5 + i:5 + i + tt, :] * cw[i].reshape(1, 1, width)
        xp_scr[idx, :, 5:8, :] = xp_scr[idx, :, 5 + tt:8 + tt, :]
        return _silu(y.reshape(rows, width))

    q_all = conv_silu(0, zq, cw_refs[0])
    k_all = conv_silu(1, zk, cw_refs[1])
    v_all = conv_silu(2, zv, cw_refs[2])

    beta_all = jax.nn.sigmoid(ba)
    g_all = -jnp.exp(alog_ref[...]) * _softplus(ba + dtb_ref[...])
    row = lax.broadcasted_iota(jnp.int32, (rows, LANES), 0)
    row_in_chunk = row % chunk
    gc_all = g_all
    s = 1
    while s < chunk:
        gc_all = gc_all + jnp.where(row_in_chunk >= s, pltpu.roll(gc_all, s, axis=0), 0.0)
        s *= 2
    gct_all = gc_all.T

    lane = lax.broadcasted_iota(jnp.int32, (rows, LANES), 1)
    sub = lax.broadcasted_iota(jnp.int32, (LANES, rows), 0)
    ri = lax.broadcasted_iota(jnp.int32, (rows, rows), 0)
    ci = lax.broadcasted_iota(jnp.int32, (rows, rows), 1)
    same_chunk = (ri // chunk) == (ci // chunk)
    causal = jnp.logical_and(same_chunk, ri >= ci)
    strict = jnp.logical_and(same_chunk, ri > ci)
    n_steps = (chunk - 1).bit_length()

    hd = []
    for hh in range(hg):
        head = h_grp * hg + hh
        ls = slice(hh * LANES, (hh + 1) * LANES)
        qh, kh, vh = q_all[:, ls], k_all[:, ls], v_all[:, ls]
        qh = qh * lax.rsqrt(jnp.sum(qh * qh, axis=-1, keepdims=True) + NORM_EPS) * (GDN_DK ** -0.5)
        kh = kh * lax.rsqrt(jnp.sum(kh * kh, axis=-1, keepdims=True) + NORM_EPS)

        beta = jnp.sum(jnp.where(lane == BETA_LANE + head, beta_all, 0.0), axis=1, keepdims=True)
        gc = jnp.sum(jnp.where(lane == ALPHA_LANE + head, gc_all, 0.0), axis=1, keepdims=True)
        gc_row = jnp.sum(jnp.where(sub == ALPHA_LANE + head, gct_all, 0.0), axis=0, keepdims=True)

        decay = jnp.where(causal, jnp.exp(jnp.where(causal, gc - gc_row, 0.0)), 0.0)
        kb = kh * beta
        gram = lax.dot_general(jnp.concatenate([kb, qh], axis=0), kh,
                               (((1,), (1,)), ((), ())), preferred_element_type=F32)
        e_gc = jnp.exp(gc)
        hd.append(dict(
            ls=ls, kh=kh, gc=gc,
            p=jnp.where(strict, -(gram[:rows] * decay), 0.0),
            a_qk=gram[rows:] * decay,
            x=jnp.concatenate([vh * beta, kb * e_gc], axis=1),
            q_dec=qh * e_gc))

    for st in range(n_steps):
        for d in hd:
            if st < n_steps - 1:
                res = _fdot(d["p"], jnp.concatenate([d["x"], d["p"]], axis=1))
                d["x"] = d["x"] + res[:, :2 * LANES]
                d["p"] = res[:, 2 * LANES:]
            else:
                d["x"] = d["x"] + _fdot(d["p"], d["x"])

    for d in hd:
        d["v_new"] = []
        d["qs"] = []
    for sq in range(bt):
        states = [s_scr[sq, hh] for hh in range(hg)]
        for c in range(n_chunks):
            r0 = sq * tt + c * chunk
            rs = slice(r0, r0 + chunk)
            for hh, d in enumerate(hd):
                u, w = d["x"][rs, :LANES], d["x"][rs, LANES:]
                ws = _fdot(jnp.concatenate([w, d["q_dec"][rs]], axis=0), states[hh])
                v_new = u - ws[:chunk]
                g_last = d["gc"][r0 + chunk - 1:r0 + chunk]
                k_dec = d["kh"][rs] * jnp.exp(g_last - d["gc"][rs])
                states[hh] = states[hh] * jnp.exp(g_last) + lax.dot_general(
                    k_dec, v_new, (((0,), (0,)), ((), ())), preferred_element_type=F32)
                d["v_new"].append(v_new)
                d["qs"].append(ws[chunk:])
        for hh in range(hg):
            s_scr[sq, hh] = states[hh]

    for d in hd:
        v_new_all = jnp.concatenate(d["v_new"], axis=0) if len(d["v_new"]) > 1 else d["v_new"][0]
        qs_all = jnp.concatenate(d["qs"], axis=0) if len(d["qs"]) > 1 else d["qs"][0]
        o = qs_all + _fdot(d["a_qk"], v_new_all)
        o = o * lax.rsqrt(jnp.mean(o * o, axis=-1, keepdims=True) + NORM_EPS) * nw_ref[...]
        o = o * _silu(zgate[:, d["ls"]])
        out_ref[:, d["ls"]] = o.astype(out_ref.dtype)


def _gdn_init_state(j, cs_refs, s0_ref, xp_scr, s_scr):
    @pl.when(j == 0)
    def _():
        for idx, cs_ref in enumerate(cs_refs):
            xp_scr[idx, :, 5:8, :] = cs_ref[...]
        s_scr[...] = s0_ref[...]


def _gdn_fused_body(x_ref, gain_ref, w_ref, wba_ref, csq_ref, csk_ref, csv_ref, s0_ref,
                    cwq_ref, cwk_ref, cwv_ref, alog_ref, dtb_ref, nw_ref,
                    out_ref, sout_ref, csout_ref, xp_scr, s_scr, *, bt, tt, chunk):
    j = pl.program_id(1)
    _gdn_init_state(j, (csq_ref, csk_ref, csv_ref), s0_ref, xp_scr, s_scr)
    u = _rms_norm_bf16(x_ref[...], gain_ref[...])
    z = jnp.dot(u, w_ref[...], preferred_element_type=F32)
    ba = jnp.dot(u, wba_ref[...], preferred_element_type=F32)
    seg = lambda c: z[:, c * SEG:(c + 1) * SEG]
    _gdn_core(seg(COL_Q), seg(COL_K), seg(COL_V), seg(COL_GDN_GATE), ba,
              (cwq_ref, cwk_ref, cwv_ref), alog_ref, dtb_ref, nw_ref, out_ref, xp_scr, s_scr,
              0, bt=bt, tt=tt, chunk=chunk, hg=GDN_HEADS)

    @pl.when(j == pl.num_programs(1) - 1)
    def _():
        sout_ref[...] = s_scr[...]
        for idx in range(3):
            csout_ref[:, :, idx * SEG:(idx + 1) * SEG] = xp_scr[idx, :, 5:8, :]


def _gdn_z_body(zq_ref, zk_ref, zv_ref, zg_ref, ba_ref, csq_ref, csk_ref, csv_ref, s0_ref,
                cwq_ref, cwk_ref, cwv_ref, alog_ref, dtb_ref, nw_ref,
                out_ref, sout_ref, xp_scr, s_scr, *, bt, tt, chunk, hg):
    j = pl.program_id(2)
    _gdn_init_state(j, (csq_ref, csk_ref, csv_ref), s0_ref, xp_scr, s_scr)
    _gdn_core(zq_ref[...], zk_ref[...], zv_ref[...], zg_ref[...], ba_ref[...],
              (cwq_ref, cwk_ref, cwv_ref), alog_ref, dtb_ref, nw_ref, out_ref, xp_scr, s_scr,
              pl.program_id(1), bt=bt, tt=tt, chunk=chunk, hg=hg)

    @pl.when(j == pl.num_programs(2) - 1)
    def _():
        sout_ref[...] = s_scr[...]


def _gdn_fused(x2, gain, w_gdn, w_ba, conv_state, s0, cw, alog_pad, dtb_pad, nw, *, nb, t, bt, tt, chunk):
    nt = t // tt
    rows = bt * tt
    hg = GDN_HEADS
    body = functools.partial(_gdn_fused_body, bt=bt, tt=tt, chunk=chunk)
    cs_spec = lambda seg: pl.BlockSpec((bt, CONV_W - 1, SEG), lambda i, j: (i, 0, seg))
    cw_spec = lambda seg: pl.BlockSpec((CONV_W, SEG), lambda i, j: (0, seg))
    vec = lambda n: pl.BlockSpec((1, n), lambda i, j: (0, 0))
    return pl.pallas_call(
        body,
        grid=(nb // bt, nt),
        in_specs=[
            pl.BlockSpec((rows, D_MODEL), lambda i, j: (i * nt + j, 0)),
            vec(D_MODEL),
            pl.BlockSpec((D_MODEL, GDN_Z_WIDTH), lambda i, j: (0, 0)),
            pl.BlockSpec((D_MODEL, LANES), lambda i, j: (0, 0)),
            cs_spec(0), cs_spec(1), cs_spec(2),
            pl.BlockSpec((bt, hg, GDN_DK, GDN_DV), lambda i, j: (i, 0, 0, 0)),
            cw_spec(0), cw_spec(1), cw_spec(2),
            vec(LANES), vec(LANES), vec(LANES),
        ],
        out_specs=[
            pl.BlockSpec((rows, GDN_VAL_W), lambda i, j: (i * nt + j, 0)),
            pl.BlockSpec((bt, hg, GDN_DK, GDN_DV), lambda i, j: (i, 0, 0, 0)),
            pl.BlockSpec((bt, CONV_W - 1, GDN_QKV_W), lambda i, j: (i, 0, 0)),
        ],
        out_shape=[
            jax.ShapeDtypeStruct((nb * t, GDN_VAL_W), BF16),
            jax.ShapeDtypeStruct((nb, GDN_HEADS, GDN_DK, GDN_DV), F32),
            jax.ShapeDtypeStruct((nb, CONV_W - 1, GDN_QKV_W), F32),
        ],
        scratch_shapes=[
            pltpu.VMEM((3, bt, SUBLANES + tt, SEG), F32),
            pltpu.VMEM((bt, hg, GDN_DK, GDN_DV), F32),
        ],
        compiler_params=pltpu.CompilerParams(
            dimension_semantics=("arbitrary", "arbitrary"), vmem_limit_bytes=VMEM_LIMIT),
        name="gdn_fused",
    )(x2, gain, w_gdn, w_ba, conv_state, conv_state, conv_state, s0, cw, cw, cw, alog_pad, dtb_pad, nw)


def _gdn_z(z, ba, conv_state, s0, cw, alog_pad, dtb_pad, nw, *, nb, t, bt, tt, chunk, hg):
    nt = t // tt
    rows = bt * tt
    width = LANES * hg
    per = SEG // width
    body = functools.partial(_gdn_z_body, bt=bt, tt=tt, chunk=chunk, hg=hg)

    def zspec(col):
        return pl.BlockSpec((rows, width), lambda i, h, j: (i * nt + j, col * per + h))

    def cs_spec(seg):
        return pl.BlockSpec((bt, CONV_W - 1, width), lambda i, h, j: (i, 0, seg * per + h))

    def cw_spec(seg):
        return pl.BlockSpec((CONV_W, width), lambda i, h, j: (0, seg * per + h))

    vec = lambda: pl.BlockSpec((1, LANES), lambda i, h, j: (0, 0))
    return pl.pallas_call(
        body,
        grid=(nb // bt, GDN_HEADS // hg, nt),
        in_specs=[
            zspec(COL_Q), zspec(COL_K), zspec(COL_V), zspec(COL_GDN_GATE),
            pl.BlockSpec((rows, LANES), lambda i, h, j: (i * nt + j, 0)),
            cs_spec(0), cs_spec(1), cs_spec(2),
            pl.BlockSpec((bt, hg, GDN_DK, GDN_DV), lambda i, h, j: (i, h, 0, 0)),
            cw_spec(0), cw_spec(1), cw_spec(2),
            vec(), vec(), vec(),
        ],
        out_specs=[
            pl.BlockSpec((rows, width), lambda i, h, j: (i * nt + j, h)),
            pl.BlockSpec((bt, hg, GDN_DK, GDN_DV), lambda i, h, j: (i, h, 0, 0)),
        ],
        out_shape=[
            jax.ShapeDtypeStruct((nb * t, GDN_VAL_W), BF16),
            jax.ShapeDtypeStruct((nb, GDN_HEADS, GDN_DK, GDN_DV), F32),
        ],
        scratch_shapes=[
            pltpu.VMEM((3, bt, SUBLANES + tt, width), F32),
            pltpu.VMEM((bt, hg, GDN_DK, GDN_DV), F32),
        ],
        compiler_params=pltpu.CompilerParams(
            dimension_semantics=("arbitrary", "arbitrary", "arbitrary"), vmem_limit_bytes=VMEM_LIMIT),
        name="gdn_z",
    )(z, z, z, z, ba, conv_state, conv_state, conv_state, s0, cw, cw, cw, alog_pad, dtb_pad, nw)


def _out_body(x_ref, gpre_ref, wm_ref, lru_ref, gdn_ref, wl_ref, wg_ref, wo_ref, gpost_ref, y_ref):
    x = x_ref[...]
    u = _rms_norm_bf16(x, gpre_ref[...])
    m = jnp.dot(u, wm_ref[...], preferred_element_type=F32)
    p_lru = jnp.dot(lru_ref[...], wl_ref[...], preferred_element_type=F32)
    p_gdn = jnp.dot(gdn_ref[...], wg_ref[...], preferred_element_type=F32)
    merged = jax.nn.sigmoid(m[:, :D_MODEL]) * p_lru + jax.nn.sigmoid(m[:, D_MODEL:]) * p_gdn
    y = jnp.dot(merged.astype(BF16), wo_ref[...], preferred_element_type=F32)
    ms = jnp.mean(y * y, axis=-1, keepdims=True)
    y_ref[...] = x + (y * lax.rsqrt(ms + NORM_EPS)) * gpost_ref[...]


def _out_proj(x2, gpre, w_m, lru_out, gdn_out, wl, wg, wo, gpost, *, tm):
    n = x2.shape[0]
    wspec = lambda: pl.BlockSpec((D_MODEL, D_MODEL), lambda i: (0, 0))
    vec = lambda: pl.BlockSpec((1, D_MODEL), lambda i: (0, 0))
    rowspec = lambda: pl.BlockSpec((tm, D_MODEL), lambda i: (i, 0))
    return pl.pallas_call(
        _out_body,
        grid=(n // tm,),
        in_specs=[
            rowspec(), vec(),
            pl.BlockSpec((D_MODEL, 2 * D_MODEL), lambda i: (0, 0)),
            rowspec(), rowspec(),
            wspec(), wspec(), wspec(),
            vec(),
        ],
        out_specs=rowspec(),
        out_shape=jax.ShapeDtypeStruct((n, D_MODEL), F32),
        compiler_params=pltpu.CompilerParams(
            dimension_semantics=("arbitrary",), vmem_limit_bytes=VMEM_LIMIT),
        name="out_proj",
    )(x2, gpre, w_m, lru_out, gdn_out, wl, wg, wo, gpost)


def _prep_weights(norm_pre, norm_post, w_in, lru_conv_w, lru_conv_b, lru_wa, lru_ba, lru_wx, lru_bx,
                  lru_a_logit, gdn_conv_w, gdn_A_log, gdn_dt_bias, gdn_norm_w, w_br_lru, w_br_gdn, w_out):
    c_gdn = 2 * LRU_WIDTH
    c_ba = c_gdn + GDN_Z_WIDTH
    c_m = c_ba + 2 * GDN_HEADS
    w_ba = jnp.pad(w_in[:, c_ba:c_m], ((0, 0), (0, LANES - 2 * GDN_HEADS))).astype(BF16)
    wax = jnp.concatenate([lru_wa, lru_wx], axis=-1).astype(BF16)
    pad_alpha = lambda v: jnp.pad(v.reshape(1, GDN_HEADS), ((0, 0), (ALPHA_LANE, LANES - ALPHA_LANE - GDN_HEADS)))
    row = lambda v: v.reshape(1, -1)
    return dict(
        norm_pre=row(norm_pre), norm_post=row(norm_post),
        w_lru=w_in[:, :c_gdn].astype(BF16), w_gdn=w_in[:, c_gdn:c_ba].astype(BF16), w_ba=w_ba,
        w_m=w_in[:, c_m:].astype(BF16),
        lru_cw=lru_conv_w, lru_cb=row(lru_conv_b), wax=wax, lru_ba=row(lru_ba), lru_bx=row(lru_bx),
        lru_alog=row(lru_a_logit), gdn_cw=gdn_conv_w, alog_pad=pad_alpha(gdn_A_log),
        dtb_pad=pad_alpha(gdn_dt_bias), gdn_nw=row(gdn_norm_w),
        wl=w_br_lru.astype(BF16), wg=w_br_gdn.astype(BF16), wo=w_out.astype(BF16))


def _layer(x, lru_conv, lru_h, gdn_conv, gdn_s, p, *, reset_first, tm, lru_bt, lru_tt, gdn_bt, gdn_tt,
           gdn_hg, gdn_fused):
    nb, t, _ = x.shape
    x2 = x.reshape(nb * t, D_MODEL)
    lru_out, h_last, lru_conv_new = _lru(
        x2, p["norm_pre"], p["w_lru"], lru_conv, lru_h.reshape(nb, 1, LRU_WIDTH), p["lru_cw"], p["lru_cb"],
        p["wax"], p["lru_ba"], p["lru_bx"], p["lru_alog"], nb=nb, t=t, bt=lru_bt, tt=lru_tt,
        reset_first=reset_first)
    chunk = min(GDN_CHUNK, t)
    if gdn_fused:
        gdn_out, s_new, gdn_conv_new = _gdn_fused(
            x2, p["norm_pre"], p["w_gdn"], p["w_ba"], gdn_conv, gdn_s, p["gdn_cw"], p["alog_pad"],
            p["dtb_pad"], p["gdn_nw"], nb=nb, t=t, bt=gdn_bt, tt=gdn_tt, chunk=chunk)
    else:
        z, ba = _in_proj(x2, p["norm_pre"], p["w_gdn"], p["w_ba"], tm=tm, tn=SEG)
        gdn_out, s_new = _gdn_z(z, ba, gdn_conv, gdn_s, p["gdn_cw"], p["alog_pad"], p["dtb_pad"], p["gdn_nw"],
                                nb=nb, t=t, bt=gdn_bt, tt=gdn_tt, chunk=chunk, hg=gdn_hg)
        gdn_conv_new = z.reshape(nb, t, GDN_Z_WIDTH)[:, t - (CONV_W - 1):, :GDN_QKV_W]
    y = _out_proj(x2, p["norm_pre"], p["w_m"], lru_out, gdn_out, p["wl"], p["wg"], p["wo"], p["norm_post"],
                  tm=min(tm, 512))
    return y.reshape(nb, t, D_MODEL), lru_conv_new, h_last.reshape(nb, LRU_WIDTH), gdn_conv_new, s_new


def kernel(x_prompt, x_sample, state_lru_conv, state_lru_h, state_gdn_conv, state_gdn_S, norm_pre, norm_post, w_in, lru_conv_w, lru_conv_b, lru_wa, lru_ba, lru_wx, lru_bx, lru_a_logit, gdn_conv_w, gdn_A_log, gdn_dt_bias, gdn_norm_w, w_br_lru, w_br_gdn, w_out):
    depth = w_in.shape[0]
    assert depth == 1
    nb = x_prompt.shape[0]
    p = _prep_weights(norm_pre[0], norm_post[0], w_in[0], lru_conv_w[0], lru_conv_b[0], lru_wa[0], lru_ba[0],
                      lru_wx[0], lru_bx[0], lru_a_logit[0], gdn_conv_w[0], gdn_A_log[0], gdn_dt_bias[0],
                      gdn_norm_w[0], w_br_lru[0], w_br_gdn[0], w_out[0])
    yp, p_lc, p_lh, p_gc, p_gs = _layer(
        x_prompt,
        jnp.zeros((nb, CONV_W - 1, LRU_WIDTH), F32), jnp.zeros((nb, LRU_WIDTH), F32),
        jnp.zeros((nb, CONV_W - 1, GDN_QKV_W), F32), jnp.zeros((nb, GDN_HEADS, GDN_DK, GDN_DV), F32),
        p, reset_first=True, tm=1024, lru_bt=1, lru_tt=256, gdn_bt=1, gdn_tt=256, gdn_hg=GDN_HEADS,
        gdn_fused=True)
    ys, s_lc, s_lh, s_gc, s_gs = _layer(
        x_sample, state_lru_conv[0], state_lru_h[0], state_gdn_conv[0], state_gdn_S[0],
        p, reset_first=False, tm=1024, lru_bt=32, lru_tt=8, gdn_bt=8, gdn_tt=8, gdn_hg=2, gdn_fused=False)
    return (yp, ys, p_lc[None], p_lh[None], p_gc[None], p_gs[None],
            s_lc[None], s_lh[None], s_gc[None], s_gs[None])
```

```python
import functools

import jax
import jax.numpy as jnp
from jax import lax
from jax.experimental import pallas as pl
from jax.experimental.pallas import tpu as pltpu

F32 = jnp.float32
BF16 = jnp.bfloat16

D_MODEL = 1024
CONV_W = 4
LRU_WIDTH = 1024
LRU_BLOCKS = 8
LRU_BLOCK = LRU_WIDTH // LRU_BLOCKS
LRU_C = 8.0
GDN_HEADS = 8
GDN_DK = 128
GDN_DV = 128
GDN_KEY_W = GDN_HEADS * GDN_DK
GDN_VAL_W = GDN_HEADS * GDN_DV
GDN_QKV_W = 2 * GDN_KEY_W + GDN_VAL_W
GDN_CHUNK = 64
NORM_EPS = 1e-6

LANES = 128
SUBLANES = 8
SEG = 1024
GDN_Z_WIDTH = 4 * SEG
COL_Q, COL_K, COL_V, COL_GDN_GATE = range(4)
BETA_LANE = 0
ALPHA_LANE = GDN_HEADS
VMEM_LIMIT = 48 * 1024 * 1024
GDN_SEQ_GROUP = 2


def _softplus(x):
    return jnp.maximum(x, 0.0) + jnp.log1p(jnp.exp(-jnp.abs(x)))


def _silu(x):
    return x * jax.nn.sigmoid(x)


def _fdot(a, b):
    return jnp.dot(a, b, preferred_element_type=F32)


def _rms_norm_bf16(x, gain):
    ms = jnp.mean(x * x, axis=-1, keepdims=True)
    return ((x * lax.rsqrt(ms + NORM_EPS)) * gain).astype(BF16)


def _in_proj_body(x_ref, gain_ref, w_ref, wba_ref, z_ref, ba_ref, u_scr):
    @pl.when(pl.program_id(1) == 0)
    def _():
        u = _rms_norm_bf16(x_ref[...], gain_ref[...])
        u_scr[...] = u
        ba_ref[...] = jnp.dot(u, wba_ref[...], preferred_element_type=F32)

    z_ref[...] = jnp.dot(u_scr[...], w_ref[...], preferred_element_type=F32)


def _in_proj(x2, gain, w, w_ba, *, tm, tn):
    n = x2.shape[0]
    width = w.shape[1]
    return pl.pallas_call(
        _in_proj_body,
        grid=(n // tm, width // tn),
        in_specs=[
            pl.BlockSpec((tm, D_MODEL), lambda i, j: (i, 0)),
            pl.BlockSpec((1, D_MODEL), lambda i, j: (0, 0)),
            pl.BlockSpec((D_MODEL, tn), lambda i, j: (0, j)),
            pl.BlockSpec((D_MODEL, LANES), lambda i, j: (0, 0)),
        ],
        out_specs=[
            pl.BlockSpec((tm, tn), lambda i, j: (i, j)),
            pl.BlockSpec((tm, LANES), lambda i, j: (i, 0)),
        ],
        out_shape=[
            jax.ShapeDtypeStruct((n, width), F32),
            jax.ShapeDtypeStruct((n, LANES), F32),
        ],
        scratch_shapes=[pltpu.VMEM((tm, D_MODEL), BF16)],
        compiler_params=pltpu.CompilerParams(
            dimension_semantics=("arbitrary", "arbitrary"), vmem_limit_bytes=VMEM_LIMIT),
        name="in_proj",
    )(x2, gain, w, w_ba)


def _lru_body(x_ref, gain_ref, w_ref, cs_ref, h0_ref, cw_ref, cb_ref, wax_ref, ba_ref, bx_ref, alog_ref,
              out_ref, hout_ref, csout_ref, xp_scr, a_scr, b_scr, h_scr, *, bt, tt, reset_first):
    j = pl.program_id(1)
    rows = bt * tt
    width = LRU_WIDTH

    @pl.when(j == 0)
    def _():
        xp_scr[:, 5:8, :] = cs_ref[...]
        h_scr[...] = h0_ref[...]

    u = _rms_norm_bf16(x_ref[...], gain_ref[...])
    z = jnp.dot(u, w_ref[...], preferred_element_type=F32)
    gate = z[:, width:]

    cur = z[:, :width].reshape(bt, tt, width)
    xp_scr[:, 8:8 + tt, :] = cur
    cw = cw_ref[...]
    xc = cur * cw[CONV_W - 1].reshape(1, 1, width)
    for i in range(CONV_W - 2, -1, -1):
        xc = xc + xp_scr[:, 5 + i:5 + i + tt, :] * cw[i].reshape(1, 1, width)
    xp_scr[:, 5:8, :] = xp_scr[:, 5 + tt:8 + tt, :]
    xc = (xc + cb_ref[...].reshape(1, 1, width)).reshape(rows, width)

    a_l = alog_ref[...]
    log_sig_a = jnp.minimum(a_l, 0.0) - jnp.log1p(jnp.exp(-jnp.abs(a_l)))

    n_vreg_rows = rows // SUBLANES
    sub3 = lax.broadcasted_iota(jnp.int32, (n_vreg_rows, SUBLANES, LRU_BLOCK), 1)
    if reset_first:
        row = lax.broadcasted_iota(jnp.int32, (rows, LRU_BLOCK), 0)
        is_reset = jnp.logical_and(row % tt == 0, j == 0)

    for g in range(LRU_BLOCKS):
        gs = slice(g * LRU_BLOCK, (g + 1) * LRU_BLOCK)
        xg = xc[:, gs]
        pre = jnp.dot(xg.astype(BF16), wax_ref[g], preferred_element_type=F32)
        r = jax.nn.sigmoid(pre[:, :LRU_BLOCK] + ba_ref[:, gs])
        ig = jax.nn.sigmoid(pre[:, LRU_BLOCK:] + bx_ref[:, gs])
        log_a = (LRU_C * r) * log_sig_a[:, gs]
        a = jnp.exp(log_a)
        t = jnp.tanh(log_a)
        m2 = (-2.0 * t) / (1.0 - t)
        mult = jnp.where(m2 > 0.0, m2 * lax.rsqrt(m2), 0.0)
        if reset_first:
            a = jnp.where(is_reset, 0.0, a)
            mult = jnp.where(is_reset, 1.0, mult)
        b = mult * ig * xg
        a = a.reshape(n_vreg_rows, SUBLANES, LRU_BLOCK)
        b = b.reshape(n_vreg_rows, SUBLANES, LRU_BLOCK)
        s = 1
        while s < SUBLANES:
            keep = sub3 >= s
            a_sh = jnp.where(keep, pltpu.roll(a, s, axis=1), 1.0)
            b_sh = jnp.where(keep, pltpu.roll(b, s, axis=1), 0.0)
            b = a * b_sh + b
            a = a * a_sh
            s *= 2
        a_scr[:, :, gs] = a.reshape(bt, tt, LRU_BLOCK)
        b_scr[:, :, gs] = b.reshape(bt, tt, LRU_BLOCK)

    def group_step(i, carry):
        r0 = pl.multiple_of(i * SUBLANES, SUBLANES)
        h = b_scr[:, pl.ds(r0, SUBLANES), :] + a_scr[:, pl.ds(r0, SUBLANES), :] * carry
        b_scr[:, pl.ds(r0, SUBLANES), :] = h
        return h[:, SUBLANES - 1:SUBLANES, :]

    n_groups = tt // SUBLANES
    if n_groups == 1:
        carry = group_step(0, h_scr[...])
    else:
        carry = lax.fori_loop(0, n_groups, group_step, h_scr[...])
    h_scr[...] = carry

    h = b_scr[...].reshape(rows, width)
    out_ref[...] = (h * _silu(gate)).astype(out_ref.dtype)

    @pl.when(j == pl.num_programs(1) - 1)
    def _():
        hout_ref[...] = carry
        csout_ref[...] = xp_scr[:, 5:8, :]


def _lru(x2, gain, w_lru, conv_state, h0, cw, cb, wax, ba, bx, alog, *, nb, t, bt, tt, reset_first):
    nt = t // tt
    rows = bt * tt
    body = functools.partial(_lru_body, bt=bt, tt=tt, reset_first=reset_first)
    vec = lambda: pl.BlockSpec((1, LRU_WIDTH), lambda i, j: (0, 0))
    return pl.pallas_call(
        body,
        grid=(nb // bt, nt),
        in_specs=[
            pl.BlockSpec((rows, D_MODEL), lambda i, j: (i * nt + j, 0)),
            vec(),
            pl.BlockSpec((D_MODEL, 2 * LRU_WIDTH), lambda i, j: (0, 0)),
            pl.BlockSpec((bt, CONV_W - 1, LRU_WIDTH), lambda i, j: (i, 0, 0)),
            pl.BlockSpec((bt, 1, LRU_WIDTH), lambda i, j: (i, 0, 0)),
            pl.BlockSpec((CONV_W, LRU_WIDTH), lambda i, j: (0, 0)),
            vec(),
            pl.BlockSpec((LRU_BLOCKS, LRU_BLOCK, 2 * LRU_BLOCK), lambda i, j: (0, 0, 0)),
            vec(), vec(), vec(),
        ],
        out_specs=[
            pl.BlockSpec((rows, LRU_WIDTH), lambda i, j: (i * nt + j, 0)),
            pl.BlockSpec((bt, 1, LRU_WIDTH), lambda i, j: (i, 0, 0)),
            pl.BlockSpec((bt, CONV_W - 1, LRU_WIDTH), lambda i, j: (i, 0, 0)),
        ],
        out_shape=[
            jax.ShapeDtypeStruct((nb * t, LRU_WIDTH), BF16),
            jax.ShapeDtypeStruct((nb, 1, LRU_WIDTH), F32),
            jax.ShapeDtypeStruct((nb, CONV_W - 1, LRU_WIDTH), F32),
        ],
        scratch_shapes=[
            pltpu.VMEM((bt, SUBLANES + tt, LRU_WIDTH), F32),
            pltpu.VMEM((bt, tt, LRU_WIDTH), F32),
            pltpu.VMEM((bt, tt, LRU_WIDTH), F32),
            pltpu.VMEM((bt, 1, LRU_WIDTH), F32),
        ],
        compiler_params=pltpu.CompilerParams(
            dimension_semantics=("arbitrary", "arbitrary"), vmem_limit_bytes=VMEM_LIMIT),
        name="lru",
    )(x2, gain, w_lru, conv_state, h0, cw, cb, wax, ba, bx, alog)


def _gdn_core(zq, zk, zv, zgate, ba, cw_refs, alog_ref, dtb_ref, nw_ref, out_ref, xp_scr, s_scr,
              h_grp, *, bt, tt, chunk, hg, seq_group):
    rows = bt * tt
    width = LANES * hg
    n_chunks = tt // chunk

    def conv_silu(idx, cur2, cw_ref):
        cur = cur2.reshape(bt, tt, width)
        xp_scr[idx, :, 8:8 + tt, :] = cur
        cw = cw_ref[...]
        y = cur * cw[CONV_W - 1].reshape(1, 1, width)
        for i in range(CONV_W - 2, -1, -1):
            y = y + xp_scr[idx, :, 5 + i:5 + i + tt, :] * cw[i].reshape(1, 1, width)
        xp_scr[idx, :, 5:8, :] = xp_scr[idx, :, 5 + tt:8 + tt, :]
        return _silu(y.reshape(rows, width))

    q_all = conv_silu(0, zq, cw_refs[0])
    k_all = conv_silu(1, zk, cw_refs[1])
    v_all = conv_silu(2, zv, cw_refs[2])

    beta_all = jax.nn.sigmoid(ba)
    g_all = -jnp.exp(alog_ref[...]) * _softplus(ba + dtb_ref[...])
    row = lax.broadcasted_iota(jnp.int32, (rows, LANES), 0)
    row_in_chunk = row % chunk
    gc_all = g_all
    s = 1
    while s < chunk:
        gc_all = gc_all + jnp.where(row_in_chunk >= s, pltpu.roll(gc_all, s, axis=0), 0.0)
        s *= 2
    gct_all = gc_all.T

    lane = lax.broadcasted_iota(jnp.int32, (rows, LANES), 1)
    sub = lax.broadcasted_iota(jnp.int32, (LANES, rows), 0)
    ri = lax.broadcasted_iota(jnp.int32, (rows, rows), 0)
    ci = lax.broadcasted_iota(jnp.int32, (rows, rows), 1)
    same_chunk = (ri // chunk) == (ci // chunk)
    causal = jnp.logical_and(same_chunk, ri >= ci)
    strict = jnp.logical_and(same_chunk, ri > ci)
    n_steps = (chunk - 1).bit_length()

    hd = []
    for hh in range(hg):
        head = h_grp * hg + hh
        ls = slice(hh * LANES, (hh + 1) * LANES)
        qh, kh, vh = q_all[:, ls], k_all[:, ls], v_all[:, ls]
        qh = qh * lax.rsqrt(jnp.sum(qh * qh, axis=-1, keepdims=True) + NORM_EPS) * (GDN_DK ** -0.5)
        kh = kh * lax.rsqrt(jnp.sum(kh * kh, axis=-1, keepdims=True) + NORM_EPS)

        beta = jnp.sum(jnp.where(lane == BETA_LANE + head, beta_all, 0.0), axis=1, keepdims=True)
        gc = jnp.sum(jnp.where(lane == ALPHA_LANE + head, gc_all, 0.0), axis=1, keepdims=True)
        gc_row = jnp.sum(jnp.where(sub == ALPHA_LANE + head, gct_all, 0.0), axis=0, keepdims=True)

        decay = jnp.where(causal, jnp.exp(jnp.where(causal, gc - gc_row, 0.0)), 0.0)
        kb = kh * beta
        gram = lax.dot_general(jnp.concatenate([kb, qh], axis=0), kh,
                               (((1,), (1,)), ((), ())), preferred_element_type=F32)
        e_gc = jnp.exp(gc)
        hd.append(dict(
            ls=ls, kh=kh, gc=gc,
            p=jnp.where(strict, -(gram[:rows] * decay), 0.0),
            a_qk=gram[rows:] * decay,
            x=jnp.concatenate([vh * beta, kb * e_gc], axis=1),
            q_dec=qh * e_gc))

    for st in range(n_steps):
        for d in hd:
            if st < n_steps - 1:
                res = _fdot(d["p"], jnp.concatenate([d["x"], d["p"]], axis=1))
                d["x"] = d["x"] + res[:, :2 * LANES]
                d["p"] = res[:, 2 * LANES:]
            else:
                d["x"] = d["x"] + _fdot(d["p"], d["x"])

    for d in hd:
        d["v_new"] = {}
        d["qs"] = {}
    for sq0 in range(0, bt, seq_group):
        pairs = [(sq, hh) for sq in range(sq0, min(sq0 + seq_group, bt)) for hh in range(hg)]
        state = {pr: s_scr[pr[0], pr[1]] for pr in pairs}
        for c in range(n_chunks):
            rsl = {sq: slice(sq * tt + c * chunk, sq * tt + (c + 1) * chunk) for sq, _ in pairs}
            ws, v_new, k_dec, e_last, upd = {}, {}, {}, {}, {}
            for pr in pairs:
                d, rs = hd[pr[1]], rsl[pr[0]]
                ws[pr] = _fdot(jnp.concatenate([d["x"][rs, LANES:], d["q_dec"][rs]], axis=0), state[pr])
            for pr in pairs:
                d, rs = hd[pr[1]], rsl[pr[0]]
                v_new[pr] = d["x"][rs, :LANES] - ws[pr][:chunk]
                g_last = d["gc"][rs.stop - 1:rs.stop]
                k_dec[pr] = d["kh"][rs] * jnp.exp(g_last - d["gc"][rs])
                e_last[pr] = jnp.exp(g_last)
                d["v_new"][(pr[0], c)] = v_new[pr]
                d["qs"][(pr[0], c)] = ws[pr][chunk:]
            for pr in pairs:
                upd[pr] = lax.dot_general(k_dec[pr], v_new[pr], (((0,), (0,)), ((), ())),
                                          preferred_element_type=F32)
            for pr in pairs:
                state[pr] = state[pr] * e_last[pr] + upd[pr]
        for pr in pairs:
            s_scr[pr[0], pr[1]] = state[pr]

    order = [(sq, c) for sq in range(bt) for c in range(n_chunks)]
    for d in hd:
        v_parts = [d["v_new"][k] for k in order]
        q_parts = [d["qs"][k] for k in order]
        v_new_all = jnp.concatenate(v_parts, axis=0) if len(v_parts) > 1 else v_parts[0]
        qs_all = jnp.concatenate(q_parts, axis=0) if len(q_parts) > 1 else q_parts[0]
        o = qs_all + _fdot(d["a_qk"], v_new_all)
        o = o * lax.rsqrt(jnp.mean(o * o, axis=-1, keepdims=True) + NORM_EPS) * nw_ref[...]
        o = o * _silu(zgate[:, d["ls"]])
        out_ref[:, d["ls"]] = o.astype(out_ref.dtype)


def _gdn_init_state(j, cs_refs, s0_ref, xp_scr, s_scr):
    @pl.when(j == 0)
    def _():
        for idx, cs_ref in enumerate(cs_refs):
            xp_scr[idx, :, 5:8, :] = cs_ref[...]
        s_scr[...] = s0_ref[...]


def _gdn_fused_body(x_ref, gain_ref, w_ref, wba_ref, csq_ref, csk_ref, csv_ref, s0_ref,
                    cwq_ref, cwk_ref, cwv_ref, alog_ref, dtb_ref, nw_ref,
                    out_ref, sout_ref, csout_ref, xp_scr, s_scr, *, bt, tt, chunk):
    j = pl.program_id(1)
    _gdn_init_state(j, (csq_ref, csk_ref, csv_ref), s0_ref, xp_scr, s_scr)
    u = _rms_norm_bf16(x_ref[...], gain_ref[...])
    z = jnp.dot(u, w_ref[...], preferred_element_type=F32)
    ba = jnp.dot(u, wba_ref[...], preferred_element_type=F32)
    seg = lambda c: z[:, c * SEG:(c + 1) * SEG]
    _gdn_core(seg(COL_Q), seg(COL_K), seg(COL_V), seg(COL_GDN_GATE), ba,
              (cwq_ref, cwk_ref, cwv_ref), alog_ref, dtb_ref, nw_ref, out_ref, xp_scr, s_scr,
              0, bt=bt, tt=tt, chunk=chunk, hg=GDN_HEADS, seq_group=1)

    @pl.when(j == pl.num_programs(1) - 1)
    def _():
        sout_ref[...] = s_scr[...]
        for idx in range(3):
            csout_ref[:, :, idx * SEG:(idx + 1) * SEG] = xp_scr[idx, :, 5:8, :]


def _gdn_z_body(zq_ref, zk_ref, zv_ref, zg_ref, ba_ref, csq_ref, csk_ref, csv_ref, s0_ref,
                cwq_ref, cwk_ref, cwv_ref, alog_ref, dtb_ref, nw_ref,
                out_ref, sout_ref, xp_scr, s_scr, *, bt, tt, chunk, hg):
    j = pl.program_id(2)
    _gdn_init_state(j, (csq_ref, csk_ref, csv_ref), s0_ref, xp_scr, s_scr)
    _gdn_core(zq_ref[...], zk_ref[...], zv_ref[...], zg_ref[...], ba_ref[...],
              (cwq_ref, cwk_ref, cwv_ref), alog_ref, dtb_ref, nw_ref, out_ref, xp_scr, s_scr,
              pl.program_id(1), bt=bt, tt=tt, chunk=chunk, hg=hg, seq_group=GDN_SEQ_GROUP)

    @pl.when(j == pl.num_programs(2) - 1)
    def _():
        sout_ref[...] = s_scr[...]


def _gdn_fused(x2, gain, w_gdn, w_ba, conv_state, s0, cw, alog_pad, dtb_pad, nw, *, nb, t, bt, tt, chunk):
    nt = t // tt
    rows = bt * tt
    hg = GDN_HEADS
    body = functools.partial(_gdn_fused_body, bt=bt, tt=tt, chunk=chunk)
    cs_spec = lambda seg: pl.BlockSpec((bt, CONV_W - 1, SEG), lambda i, j: (i, 0, seg))
    cw_spec = lambda seg: pl.BlockSpec((CONV_W, SEG), lambda i, j: (0, seg))
    vec = lambda n: pl.BlockSpec((1, n), lambda i, j: (0, 0))
    return pl.pallas_call(
        body,
        grid=(nb // bt, nt),
        in_specs=[
            pl.BlockSpec((rows, D_MODEL), lambda i, j: (i * nt + j, 0)),
            vec(D_MODEL),
            pl.BlockSpec((D_MODEL, GDN_Z_WIDTH), lambda i, j: (0, 0)),
            pl.BlockSpec((D_MODEL, LANES), lambda i, j: (0, 0)),
            cs_spec(0), cs_spec(1), cs_spec(2),
            pl.BlockSpec((bt, hg, GDN_DK, GDN_DV), lambda i, j: (i, 0, 0, 0)),
            cw_spec(0), cw_spec(1), cw_spec(2),
            vec(LANES), vec(LANES), vec(LANES),
        ],
        out_specs=[
            pl.BlockSpec((rows, GDN_VAL_W), lambda i, j: (i * nt + j, 0)),
            pl.BlockSpec((bt, hg, GDN_DK, GDN_DV), lambda i, j: (i, 0, 0, 0)),
            pl.BlockSpec((bt, CONV_W - 1, GDN_QKV_W), lambda i, j: (i, 0, 0)),
        ],
        out_shape=[
            jax.ShapeDtypeStruct((nb * t, GDN_VAL_W), BF16),
            jax.ShapeDtypeStruct((nb, GDN_HEADS, GDN_DK, GDN_DV), F32),
            jax.ShapeDtypeStruct((nb, CONV_W - 1, GDN_QKV_W), F32),
        ],
        scratch_shapes=[
            pltpu.VMEM((3, bt, SUBLANES + tt, SEG), F32),
            pltpu.VMEM((bt, hg, GDN_DK, GDN_DV), F32),
        ],
        compiler_params=pltpu.CompilerParams(
            dimension_semantics=("arbitrary", "arbitrary"), vmem_limit_bytes=VMEM_LIMIT),
        name="gdn_fused",
    )(x2, gain, w_gdn, w_ba, conv_state, conv_state, conv_state, s0, cw, cw, cw, alog_pad, dtb_pad, nw)


def _gdn_z(z, ba, conv_state, s0, cw, alog_pad, dtb_pad, nw, *, nb, t, bt, tt, chunk, hg):
    nt = t // tt
    rows = bt * tt
    width = LANES * hg
    per = SEG // width
    body = functools.partial(_gdn_z_body, bt=bt, tt=tt, chunk=chunk, hg=hg)

    def zspec(col):
        return pl.BlockSpec((rows, width), lambda i, h, j: (i * nt + j, col * per + h))

    def cs_spec(seg):
        return pl.BlockSpec((bt, CONV_W - 1, width), lambda i, h, j: (i, 0, seg * per + h))

    def cw_spec(seg):
        return pl.BlockSpec((CONV_W, width), lambda i, h, j: (0, seg * per + h))

    vec = lambda: pl.BlockSpec((1, LANES), lambda i, h, j: (0, 0))
    return pl.pallas_call(
        body,
        grid=(nb // bt, GDN_HEADS // hg, nt),
        in_specs=[
            zspec(COL_Q), zspec(COL_K), zspec(COL_V), zspec(COL_GDN_GATE),
            pl.BlockSpec((rows, LANES), lambda i, h, j: (i * nt + j, 0)),
            cs_spec(0), cs_spec(1), cs_spec(2),
            pl.BlockSpec((bt, hg, GDN_DK, GDN_DV), lambda i, h, j: (i, h, 0, 0)),
            cw_spec(0), cw_spec(1), cw_spec(2),
            vec(), vec(), vec(),
        ],
        out_specs=[
            pl.BlockSpec((rows, width), lambda i, h, j: (i * nt + j, h)),
            pl.BlockSpec((bt, hg, GDN_DK, GDN_DV), lambda i, h, j: (i, h, 0, 0)),
        ],
        out_shape=[
            jax.ShapeDtypeStruct((nb * t, GDN_VAL_W), BF16),
            jax.ShapeDtypeStruct((nb, GDN_HEADS, GDN_DK, GDN_DV), F32),
        ],
        scratch_shapes=[
            pltpu.VMEM((3, bt, SUBLANES + tt, width), F32),
            pltpu.VMEM((bt, hg, GDN_DK, GDN_DV), F32),
        ],
        compiler_params=pltpu.CompilerParams(
            dimension_semantics=("arbitrary", "arbitrary", "arbitrary"), vmem_limit_bytes=VMEM_LIMIT),
        name="gdn_z",
    )(z, z, z, z, ba, conv_state, conv_state, conv_state, s0, cw, cw, cw, alog_pad, dtb_pad, nw)


def _out_body(x_ref, gpre_ref, wm_ref, lru_ref, gdn_ref, wl_ref, wg_ref, wo_ref, gpost_ref, y_ref):
    x = x_ref[...]
    u = _rms_norm_bf16(x, gpre_ref[...])
    m = jnp.dot(u, wm_ref[...], preferred_element_type=F32)
    p_lru = jnp.dot(lru_ref[...], wl_ref[...], preferred_element_type=F32)
    p_gdn = jnp.dot(gdn_ref[...], wg_ref[...], preferred_element_type=F32)
    merged = jax.nn.sigmoid(m[:, :D_MODEL]) * p_lru + jax.nn.sigmoid(m[:, D_MODEL:]) * p_gdn
    y = jnp.dot(merged.astype(BF16), wo_ref[...], preferred_element_type=F32)
    ms = jnp.mean(y * y, axis=-1, keepdims=True)
    y_ref[...] = x + (y * lax.rsqrt(ms + NORM_EPS)) * gpost_ref[...]


def _out_proj(x2, gpre, w_m, lru_out, gdn_out, wl, wg, wo, gpost, *, tm):
    n = x2.shape[0]
    wspec = lambda: pl.BlockSpec((D_MODEL, D_MODEL), lambda i: (0, 0))
    vec = lambda: pl.BlockSpec((1, D_MODEL), lambda i: (0, 0))
    rowspec = lambda: pl.BlockSpec((tm, D_MODEL), lambda i: (i, 0))
    return pl.pallas_call(
        _out_body,
        grid=(n // tm,),
        in_specs=[
            rowspec(), vec(),
            pl.BlockSpec((D_MODEL, 2 * D_MODEL), lambda i: (0, 0)),
            rowspec(), rowspec(),
            wspec(), wspec(), wspec(),
            vec(),
        ],
        out_specs=rowspec(),
        out_shape=jax.ShapeDtypeStruct((n, D_MODEL), F32),
        compiler_params=pltpu.CompilerParams(
            dimension_semantics=("arbitrary",), vmem_limit_bytes=VMEM_LIMIT),
        name="out_proj",
    )(x2, gpre, w_m, lru_out, gdn_out, wl, wg, wo, gpost)


def _prep_weights(norm_pre, norm_post, w_in, lru_conv_w, lru_conv_b, lru_wa, lru_ba, lru_wx, lru_bx,
                  lru_a_logit, gdn_conv_w, gdn_A_log, gdn_dt_bias, gdn_norm_w, w_br_lru, w_br_gdn, w_out):
    c_gdn = 2 * LRU_WIDTH
    c_ba = c_gdn + GDN_Z_WIDTH
    c_m = c_ba + 2 * GDN_HEADS
    w_ba = jnp.pad(w_in[:, c_ba:c_m], ((0, 0), (0, LANES - 2 * GDN_HEADS))).astype(BF16)
    wax = jnp.concatenate([lru_wa, lru_wx], axis=-1).astype(BF16)
    pad_alpha = lambda v: jnp.pad(v.reshape(1, GDN_HEADS), ((0, 0), (ALPHA_LANE, LANES - ALPHA_LANE - GDN_HEADS)))
    row = lambda v: v.reshape(1, -1)
    return dict(
        norm_pre=row(norm_pre), norm_post=row(norm_post),
        w_lru=w_in[:, :c_gdn].astype(BF16), w_gdn=w_in[:, c_gdn:c_ba].astype(BF16), w_ba=w_ba,
        w_m=w_in[:, c_m:].astype(BF16),
        lru_cw=lru_conv_w, lru_cb=row(lru_conv_b), wax=wax, lru_ba=row(lru_ba), lru_bx=row(lru_bx),
        lru_alog=row(lru_a_logit), gdn_cw=gdn_conv_w, alog_pad=pad_alpha(gdn_A_log),
        dtb_pad=pad_alpha(gdn_dt_bias), gdn_nw=row(gdn_norm_w),
        wl=w_br_lru.astype(BF16), wg=w_br_gdn.astype(BF16), wo=w_out.astype(BF16))


def _layer(x, lru_conv, lru_h, gdn_conv, gdn_s, p, *, reset_first, tm, lru_bt, lru_tt, gdn_bt, gdn_tt,
           gdn_hg, gdn_fused):
    nb, t, _ = x.shape
    x2 = x.reshape(nb * t, D_MODEL)
    lru_out, h_last, lru_conv_new = _lru(
        x2, p["norm_pre"], p["w_lru"], lru_conv, lru_h.reshape(nb, 1, LRU_WIDTH), p["lru_cw"], p["lru_cb"],
        p["wax"], p["lru_ba"], p["lru_bx"], p["lru_alog"], nb=nb, t=t, bt=lru_bt, tt=lru_tt,
        reset_first=reset_first)
    chunk = min(GDN_CHUNK, t)
    if gdn_fused:
        gdn_out, s_new, gdn_conv_new = _gdn_fused(
            x2, p["norm_pre"], p["w_gdn"], p["w_ba"], gdn_conv, gdn_s, p["gdn_cw"], p["alog_pad"],
            p["dtb_pad"], p["gdn_nw"], nb=nb, t=t, bt=gdn_bt, tt=gdn_tt, chunk=chunk)
    else:
        z, ba = _in_proj(x2, p["norm_pre"], p["w_gdn"], p["w_ba"], tm=tm, tn=SEG)
        gdn_out, s_new = _gdn_z(z, ba, gdn_conv, gdn_s, p["gdn_cw"], p["alog_pad"], p["dtb_pad"], p["gdn_nw"],
                                nb=nb, t=t, bt=gdn_bt, tt=gdn_tt, chunk=chunk, hg=gdn_hg)
        gdn_conv_new = z.reshape(nb, t, GDN_Z_WIDTH)[:, t - (CONV_W - 1):, :GDN_QKV_W]
    y = _out_proj(x2, p["norm_pre"], p["w_m"], lru_out, gdn_out, p["wl"], p["wg"], p["wo"], p["norm_post"],
                  tm=min(tm, 512))
    return y.reshape(nb, t, D_MODEL), lru_conv_new, h_last.reshape(nb, LRU_WIDTH), gdn_conv_new, s_new


def kernel(x_prompt, x_sample, state_lru_conv, state_lru_h, state_gdn_conv, state_gdn_S, norm_pre, norm_post, w_in, lru_conv_w, lru_conv_b, lru_wa, lru_ba, lru_wx, lru_bx, lru_a_logit, gdn_conv_w, gdn_A_log, gdn_dt_bias, gdn_norm_w, w_br_lru, w_br_gdn, w_out):
    depth = w_in.shape[0]
    assert depth == 1
    nb = x_prompt.shape[0]
    p = _prep_weights(norm_pre[0], norm_post[0], w_in[0], lru_conv_w[0], lru_conv_b[0], lru_wa[0], lru_ba[0],
                      lru_wx[0], lru_bx[0], lru_a_logit[0], gdn_conv_w[0], gdn_A_log[0], gdn_dt_bias[0],
                      gdn_norm_w[0], w_br_lru[0], w_br_gdn[0], w_out[0])
    yp, p_lc, p_lh, p_gc, p_gs = _layer(
        x_prompt,
        jnp.zeros((nb, CONV_W - 1, LRU_WIDTH), F32), jnp.zeros((nb, LRU_WIDTH), F32),
        jnp.zeros((nb, CONV_W - 1, GDN_QKV_W), F32), jnp.zeros((nb, GDN_HEADS, GDN_DK, GDN_DV), F32),
        p, reset_first=True, tm=1024, lru_bt=1, lru_tt=256, gdn_bt=1, gdn_tt=256, gdn_hg=GDN_HEADS,
        gdn_fused=True)
    ys, s_lc, s_lh, s_gc, s_gs = _layer(
        x_sample, state_lru_conv[0], state_lru_h[0], state_gdn_conv[0], state_gdn_S[0],
        p, reset_first=False, tm=1024, lru_bt=32, lru_tt=8, gdn_bt=8, gdn_tt=8, gdn_hg=8, gdn_fused=False)
    return (yp, ys, p_lc[None], p_lh[None], p_gc[None], p_gs[None],
            s_lc[None], s_lh[None], s_gc[None], s_gs[None])
```

```python
import functools

import jax
import jax.numpy as jnp
from jax import lax
from jax.experimental import pallas as pl
from jax.experimental.pallas import tpu as pltpu

F32 = jnp.float32
BF16 = jnp.bfloat16

D_MODEL = 1024
CONV_W = 4
LRU_WIDTH = 1024
LRU_BLOCKS = 8
LRU_BLOCK = LRU_WIDTH // LRU_BLOCKS
LRU_C = 8.0
GDN_HEADS = 8
GDN_DK = 128
GDN_DV = 128
GDN_KEY_W = GDN_HEADS * GDN_DK
GDN_VAL_W = GDN_HEADS * GDN_DV
GDN_QKV_W = 2 * GDN_KEY_W + GDN_VAL_W
GDN_CHUNK = 64
NORM_EPS = 1e-6

LANES = 128
SUBLANES = 8
SEG = 1024
LRU_Z_WIDTH = 2 * SEG
GDN_Z_WIDTH = 4 * SEG
COL_Q, COL_K, COL_V, COL_GDN_GATE = range(4)
BETA_LANE = 0
ALPHA_LANE = GDN_HEADS
VMEM_LIMIT = 48 * 1024 * 1024
SEQ_VMEM_LIMIT = 56 * 1024 * 1024
GDN_SEQ_GROUP = 2
LRU_CARRY_PIECE = 8
FILL_EVERY = 4


def _softplus(x):
    return jnp.maximum(x, 0.0) + jnp.log1p(jnp.exp(-jnp.abs(x)))


def _silu(x):
    return x * jax.nn.sigmoid(x)


def _fdot(a, b):
    return jnp.dot(a, b, preferred_element_type=F32)


def _rms_norm_bf16(x, gain):
    ms = jnp.mean(x * x, axis=-1, keepdims=True)
    return ((x * lax.rsqrt(ms + NORM_EPS)) * gain).astype(BF16)


def _resident(shape, index_map):
    return pl.BlockSpec(shape, index_map, pipeline_mode=pl.Buffered(1))


def _in_proj_body(x_ref, gain_ref, w_ref, wba_ref, z_ref, ba_ref, u_scr):
    @pl.when(pl.program_id(1) == 0)
    def _():
        u = _rms_norm_bf16(x_ref[...], gain_ref[...])
        u_scr[...] = u
        ba_ref[...] = jnp.dot(u, wba_ref[...], preferred_element_type=F32)

    z_ref[...] = jnp.dot(u_scr[...], w_ref[...], preferred_element_type=F32)


def _in_proj(x2, gain, w, w_ba, *, tm, tn):
    n = x2.shape[0]
    width = w.shape[1]
    return pl.pallas_call(
        _in_proj_body,
        grid=(n // tm, width // tn),
        in_specs=[
            pl.BlockSpec((tm, D_MODEL), lambda i, j: (i, 0)),
            pl.BlockSpec((1, D_MODEL), lambda i, j: (0, 0)),
            pl.BlockSpec((D_MODEL, tn), lambda i, j: (0, j)),
            pl.BlockSpec((D_MODEL, LANES), lambda i, j: (0, 0)),
        ],
        out_specs=[
            pl.BlockSpec((tm, tn), lambda i, j: (i, j)),
            pl.BlockSpec((tm, LANES), lambda i, j: (i, 0)),
        ],
        out_shape=[
            jax.ShapeDtypeStruct((n, width), F32),
            jax.ShapeDtypeStruct((n, LANES), F32),
        ],
        scratch_shapes=[pltpu.VMEM((tm, D_MODEL), BF16)],
        compiler_params=pltpu.CompilerParams(
            dimension_semantics=("arbitrary", "arbitrary"), vmem_limit_bytes=VMEM_LIMIT),
        name="in_proj",
    )(x2, gain, w, w_ba)


def _lru_stages(z_lru, gate, j, cw_ref, cb_ref, wax_ref, ba_ref, bx_ref, alog_ref, out_ref,
                xp_scr, a_scr, b_scr, h_scr, *, bt, tt, reset_first):
    rows = bt * tt
    width = LRU_WIDTH
    n_vreg_rows = rows // SUBLANES
    n_groups = tt // SUBLANES
    env = {}

    def conv():
        cur = z_lru.reshape(bt, tt, width)
        xp_scr[:, 8:8 + tt, :] = cur
        cw = cw_ref[...]
        xc = cur * cw[CONV_W - 1].reshape(1, 1, width)
        for i in range(CONV_W - 2, -1, -1):
            xc = xc + xp_scr[:, 5 + i:5 + i + tt, :] * cw[i].reshape(1, 1, width)
        xp_scr[:, 5:8, :] = xp_scr[:, 5 + tt:8 + tt, :]
        env["xc"] = (xc + cb_ref[...].reshape(1, 1, width)).reshape(rows, width)
        a_l = alog_ref[...]
        env["log_sig_a"] = jnp.minimum(a_l, 0.0) - jnp.log1p(jnp.exp(-jnp.abs(a_l)))
        env["sub3"] = lax.broadcasted_iota(jnp.int32, (n_vreg_rows, SUBLANES, LRU_BLOCK), 1)
        if reset_first:
            row = lax.broadcasted_iota(jnp.int32, (rows, LRU_BLOCK), 0)
            env["is_reset"] = jnp.logical_and(row % tt == 0, j == 0)

    def block(g):
        gs = slice(g * LRU_BLOCK, (g + 1) * LRU_BLOCK)
        xg = env["xc"][:, gs]
        pre = jnp.dot(xg.astype(BF16), wax_ref[g], preferred_element_type=F32)
        r = jax.nn.sigmoid(pre[:, :LRU_BLOCK] + ba_ref[:, gs])
        ig = jax.nn.sigmoid(pre[:, LRU_BLOCK:] + bx_ref[:, gs])
        log_a = (LRU_C * r) * env["log_sig_a"][:, gs]
        a = jnp.exp(log_a)
        t = jnp.tanh(log_a)
        m2 = (-2.0 * t) / (1.0 - t)
        mult = jnp.where(m2 > 0.0, m2 * lax.rsqrt(m2), 0.0)
        if reset_first:
            a = jnp.where(env["is_reset"], 0.0, a)
            mult = jnp.where(env["is_reset"], 1.0, mult)
        b = mult * ig * xg
        a = a.reshape(n_vreg_rows, SUBLANES, LRU_BLOCK)
        b = b.reshape(n_vreg_rows, SUBLANES, LRU_BLOCK)
        s = 1
        while s < SUBLANES:
            keep = env["sub3"] >= s
            a_sh = jnp.where(keep, pltpu.roll(a, s, axis=1), 1.0)
            b_sh = jnp.where(keep, pltpu.roll(b, s, axis=1), 0.0)
            b = a * b_sh + b
            a = a * a_sh
            s *= 2
        a_scr[:, :, gs] = a.reshape(bt, tt, LRU_BLOCK)
        b_scr[:, :, gs] = b.reshape(bt, tt, LRU_BLOCK)

    def carry_groups(g0, g1):
        carry = env.get("carry")
        if carry is None:
            carry = h_scr[...]
        for i in range(g0, g1):
            rs = slice(i * SUBLANES, (i + 1) * SUBLANES)
            h = b_scr[:, rs, :] + a_scr[:, rs, :] * carry
            b_scr[:, rs, :] = h
            carry = h[:, SUBLANES - 1:SUBLANES, :]
        env["carry"] = carry
        if g1 == n_groups:
            h_scr[...] = carry

    def finish():
        h = b_scr[...].reshape(rows, width)
        out_ref[...] = (h * _silu(gate)).astype(out_ref.dtype)

    per_piece = min(n_groups, LRU_CARRY_PIECE)
    pieces = [functools.partial(carry_groups, g0, min(g0 + per_piece, n_groups))
              for g0 in range(0, n_groups, per_piece)]
    return [conv] + [functools.partial(block, g) for g in range(LRU_BLOCKS)] + pieces + [finish]


def _lru_body(x_ref, gain_ref, w_ref, cs_ref, h0_ref, cw_ref, cb_ref, wax_ref, ba_ref, bx_ref, alog_ref,
              out_ref, hout_ref, csout_ref, xp_scr, a_scr, b_scr, h_scr, *, bt, tt, reset_first):
    j = pl.program_id(1)

    @pl.when(j == 0)
    def _():
        xp_scr[:, 5:8, :] = cs_ref[...]
        h_scr[...] = h0_ref[...]

    u = _rms_norm_bf16(x_ref[...], gain_ref[...])
    z = jnp.dot(u, w_ref[...], preferred_element_type=F32)
    for stage in _lru_stages(z[:, :LRU_WIDTH], z[:, LRU_WIDTH:], j, cw_ref, cb_ref, wax_ref, ba_ref, bx_ref,
                             alog_ref, out_ref, xp_scr, a_scr, b_scr, h_scr, bt=bt, tt=tt,
                             reset_first=reset_first):
        stage()

    @pl.when(j == pl.num_programs(1) - 1)
    def _():
        hout_ref[...] = h_scr[...]
        csout_ref[...] = xp_scr[:, 5:8, :]


def _lru_scratch(bt, tt):
    return [
        pltpu.VMEM((bt, SUBLANES + tt, LRU_WIDTH), F32),
        pltpu.VMEM((bt, tt, LRU_WIDTH), F32),
        pltpu.VMEM((bt, tt, LRU_WIDTH), F32),
        pltpu.VMEM((bt, 1, LRU_WIDTH), F32),
    ]


def _lru(x2, gain, w_lru, conv_state, h0, cw, cb, wax, ba, bx, alog, *, nb, t, bt, tt, reset_first):
    nt = t // tt
    rows = bt * tt
    body = functools.partial(_lru_body, bt=bt, tt=tt, reset_first=reset_first)
    vec = lambda: pl.BlockSpec((1, LRU_WIDTH), lambda i, j: (0, 0))
    return pl.pallas_call(
        body,
        grid=(nb // bt, nt),
        in_specs=[
            pl.BlockSpec((rows, D_MODEL), lambda i, j: (i * nt + j, 0)),
            vec(),
            pl.BlockSpec((D_MODEL, LRU_Z_WIDTH), lambda i, j: (0, 0)),
            pl.BlockSpec((bt, CONV_W - 1, LRU_WIDTH), lambda i, j: (i, 0, 0)),
            pl.BlockSpec((bt, 1, LRU_WIDTH), lambda i, j: (i, 0, 0)),
            pl.BlockSpec((CONV_W, LRU_WIDTH), lambda i, j: (0, 0)),
            vec(),
            pl.BlockSpec((LRU_BLOCKS, LRU_BLOCK, 2 * LRU_BLOCK), lambda i, j: (0, 0, 0)),
            vec(), vec(), vec(),
        ],
        out_specs=[
            pl.BlockSpec((rows, LRU_WIDTH), lambda i, j: (i * nt + j, 0)),
            pl.BlockSpec((bt, 1, LRU_WIDTH), lambda i, j: (i, 0, 0)),
            pl.BlockSpec((bt, CONV_W - 1, LRU_WIDTH), lambda i, j: (i, 0, 0)),
        ],
        out_shape=[
            jax.ShapeDtypeStruct((nb * t, LRU_WIDTH), BF16),
            jax.ShapeDtypeStruct((nb, 1, LRU_WIDTH), F32),
            jax.ShapeDtypeStruct((nb, CONV_W - 1, LRU_WIDTH), F32),
        ],
        scratch_shapes=_lru_scratch(bt, tt),
        compiler_params=pltpu.CompilerParams(
            dimension_semantics=("arbitrary", "arbitrary"), vmem_limit_bytes=VMEM_LIMIT),
        name="lru",
    )(x2, gain, w_lru, conv_state, h0, cw, cb, wax, ba, bx, alog)


def _gdn_core(zq, zk, zv, zgate, ba, cw_refs, alog_ref, dtb_ref, nw_ref, out_ref, xp_scr, s_scr,
              h_grp, *, bt, tt, chunk, hg, seq_group, fill=lambda: None):
    rows = bt * tt
    width = LANES * hg
    n_chunks = tt // chunk

    def conv_silu(idx, cur2, cw_ref):
        cur = cur2.reshape(bt, tt, width)
        xp_scr[idx, :, 8:8 + tt, :] = cur
        cw = cw_ref[...]
        y = cur * cw[CONV_W - 1].reshape(1, 1, width)
        for i in range(CONV_W - 2, -1, -1):
            y = y + xp_scr[idx, :, 5 + i:5 + i + tt, :] * cw[i].reshape(1, 1, width)
        xp_scr[idx, :, 5:8, :] = xp_scr[idx, :, 5 + tt:8 + tt, :]
        return _silu(y.reshape(rows, width))

    q_all = conv_silu(0, zq, cw_refs[0])
    k_all = conv_silu(1, zk, cw_refs[1])
    v_all = conv_silu(2, zv, cw_refs[2])

    beta_all = jax.nn.sigmoid(ba)
    g_all = -jnp.exp(alog_ref[...]) * _softplus(ba + dtb_ref[...])
    row = lax.broadcasted_iota(jnp.int32, (rows, LANES), 0)
    row_in_chunk = row % chunk
    gc_all = g_all
    s = 1
    while s < chunk:
        gc_all = gc_all + jnp.where(row_in_chunk >= s, pltpu.roll(gc_all, s, axis=0), 0.0)
        s *= 2
    gct_all = gc_all.T

    lane = lax.broadcasted_iota(jnp.int32, (rows, LANES), 1)
    sub = lax.broadcasted_iota(jnp.int32, (LANES, rows), 0)
    ri = lax.broadcasted_iota(jnp.int32, (rows, rows), 0)
    ci = lax.broadcasted_iota(jnp.int32, (rows, rows), 1)
    same_chunk = (ri // chunk) == (ci // chunk)
    causal = jnp.logical_and(same_chunk, ri >= ci)
    strict = jnp.logical_and(same_chunk, ri > ci)
    n_steps = (chunk - 1).bit_length()

    hd = []
    for hh in range(hg):
        head = h_grp * hg + hh
        ls = slice(hh * LANES, (hh + 1) * LANES)
        qh, kh, vh = q_all[:, ls], k_all[:, ls], v_all[:, ls]
        qh = qh * lax.rsqrt(jnp.sum(qh * qh, axis=-1, keepdims=True) + NORM_EPS) * (GDN_DK ** -0.5)
        kh = kh * lax.rsqrt(jnp.sum(kh * kh, axis=-1, keepdims=True) + NORM_EPS)

        beta = jnp.sum(jnp.where(lane == BETA_LANE + head, beta_all, 0.0), axis=1, keepdims=True)
        gc = jnp.sum(jnp.where(lane == ALPHA_LANE + head, gc_all, 0.0), axis=1, keepdims=True)
        gc_row = jnp.sum(jnp.where(sub == ALPHA_LANE + head, gct_all, 0.0), axis=0, keepdims=True)

        decay = jnp.where(causal, jnp.exp(jnp.where(causal, gc - gc_row, 0.0)), 0.0)
        kb = kh * beta
        gram = lax.dot_general(jnp.concatenate([kb, qh], axis=0), kh,
                               (((1,), (1,)), ((), ())), preferred_element_type=F32)
        e_gc = jnp.exp(gc)
        hd.append(dict(
            ls=ls, kh=kh, gc=gc,
            p=jnp.where(strict, -(gram[:rows] * decay), 0.0),
            a_qk=gram[rows:] * decay,
            x=jnp.concatenate([vh * beta, kb * e_gc], axis=1),
            q_dec=qh * e_gc))

    n_dots = 0
    for st in range(n_steps):
        for d in hd:
            if st < n_steps - 1:
                res = _fdot(d["p"], jnp.concatenate([d["x"], d["p"]], axis=1))
                d["x"] = d["x"] + res[:, :2 * LANES]
                d["p"] = res[:, 2 * LANES:]
            else:
                d["x"] = d["x"] + _fdot(d["p"], d["x"])
            n_dots += 1
            if n_dots % FILL_EVERY == 0:
                fill()

    for d in hd:
        d["v_new"] = {}
        d["qs"] = {}
    for sq0 in range(0, bt, seq_group):
        pairs = [(sq, hh) for sq in range(sq0, min(sq0 + seq_group, bt)) for hh in range(hg)]
        state = {pr: s_scr[pr[0], pr[1]] for pr in pairs}
        for c in range(n_chunks):
            rsl = {sq: slice(sq * tt + c * chunk, sq * tt + (c + 1) * chunk) for sq, _ in pairs}
            ws, v_new, k_dec, e_last, upd = {}, {}, {}, {}, {}
            for pr in pairs:
                d, rs = hd[pr[1]], rsl[pr[0]]
                ws[pr] = _fdot(jnp.concatenate([d["x"][rs, LANES:], d["q_dec"][rs]], axis=0), state[pr])
            for pr in pairs:
                d, rs = hd[pr[1]], rsl[pr[0]]
                v_new[pr] = d["x"][rs, :LANES] - ws[pr][:chunk]
                g_last = d["gc"][rs.stop - 1:rs.stop]
                k_dec[pr] = d["kh"][rs] * jnp.exp(g_last - d["gc"][rs])
                e_last[pr] = jnp.exp(g_last)
                d["v_new"][(pr[0], c)] = v_new[pr]
                d["qs"][(pr[0], c)] = ws[pr][chunk:]
            for pr in pairs:
                upd[pr] = lax.dot_general(k_dec[pr], v_new[pr], (((0,), (0,)), ((), ())),
                                          preferred_element_type=F32)
            for pr in pairs:
                state[pr] = state[pr] * e_last[pr] + upd[pr]
            fill()
        for pr in pairs:
            s_scr[pr[0], pr[1]] = state[pr]

    order = [(sq, c) for sq in range(bt) for c in range(n_chunks)]
    for d in hd:
        v_parts = [d["v_new"][k] for k in order]
        q_parts = [d["qs"][k] for k in order]
        v_new_all = jnp.concatenate(v_parts, axis=0) if len(v_parts) > 1 else v_parts[0]
        qs_all = jnp.concatenate(q_parts, axis=0) if len(q_parts) > 1 else q_parts[0]
        o = qs_all + _fdot(d["a_qk"], v_new_all)
        o = o * lax.rsqrt(jnp.mean(o * o, axis=-1, keepdims=True) + NORM_EPS) * nw_ref[...]
        o = o * _silu(zgate[:, d["ls"]])
        out_ref[:, d["ls"]] = o.astype(out_ref.dtype)


def _gdn_init_state(j, cs_refs, s0_ref, xp_scr, s_scr):
    @pl.when(j == 0)
    def _():
        for idx, cs_ref in enumerate(cs_refs):
            xp_scr[idx, :, 5:8, :] = cs_ref[...]
        s_scr[...] = s0_ref[...]


def _gdn_z_body(zq_ref, zk_ref, zv_ref, zg_ref, ba_ref, csq_ref, csk_ref, csv_ref, s0_ref,
                cwq_ref, cwk_ref, cwv_ref, alog_ref, dtb_ref, nw_ref,
                out_ref, sout_ref, xp_scr, s_scr, *, bt, tt, chunk, hg):
    j = pl.program_id(2)
    _gdn_init_state(j, (csq_ref, csk_ref, csv_ref), s0_ref, xp_scr, s_scr)
    _gdn_core(zq_ref[...], zk_ref[...], zv_ref[...], zg_ref[...], ba_ref[...],
              (cwq_ref, cwk_ref, cwv_ref), alog_ref, dtb_ref, nw_ref, out_ref, xp_scr, s_scr,
              pl.program_id(1), bt=bt, tt=tt, chunk=chunk, hg=hg, seq_group=GDN_SEQ_GROUP)

    @pl.when(j == pl.num_programs(2) - 1)
    def _():
        sout_ref[...] = s_scr[...]


def _gdn_z(z, ba, conv_state, s0, cw, alog_pad, dtb_pad, nw, *, nb, t, bt, tt, chunk, hg):
    nt = t // tt
    rows = bt * tt
    width = LANES * hg
    per = SEG // width
    body = functools.partial(_gdn_z_body, bt=bt, tt=tt, chunk=chunk, hg=hg)

    def zspec(col):
        return pl.BlockSpec((rows, width), lambda i, h, j: (i * nt + j, col * per + h))

    def cs_spec(seg):
        return pl.BlockSpec((bt, CONV_W - 1, width), lambda i, h, j: (i, 0, seg * per + h))

    def cw_spec(seg):
        return pl.BlockSpec((CONV_W, width), lambda i, h, j: (0, seg * per + h))

    vec = lambda: pl.BlockSpec((1, LANES), lambda i, h, j: (0, 0))
    return pl.pallas_call(
        body,
        grid=(nb // bt, GDN_HEADS // hg, nt),
        in_specs=[
            zspec(COL_Q), zspec(COL_K), zspec(COL_V), zspec(COL_GDN_GATE),
            pl.BlockSpec((rows, LANES), lambda i, h, j: (i * nt + j, 0)),
            cs_spec(0), cs_spec(1), cs_spec(2),
            pl.BlockSpec((bt, hg, GDN_DK, GDN_DV), lambda i, h, j: (i, h, 0, 0)),
            cw_spec(0), cw_spec(1), cw_spec(2),
            vec(), vec(), vec(),
        ],
        out_specs=[
            pl.BlockSpec((rows, width), lambda i, h, j: (i * nt + j, h)),
            pl.BlockSpec((bt, hg, GDN_DK, GDN_DV), lambda i, h, j: (i, h, 0, 0)),
        ],
        out_shape=[
            jax.ShapeDtypeStruct((nb * t, GDN_VAL_W), BF16),
            jax.ShapeDtypeStruct((nb, GDN_HEADS, GDN_DK, GDN_DV), F32),
        ],
        scratch_shapes=[
            pltpu.VMEM((3, bt, SUBLANES + tt, width), F32),
            pltpu.VMEM((bt, hg, GDN_DK, GDN_DV), F32),
        ],
        compiler_params=pltpu.CompilerParams(
            dimension_semantics=("arbitrary", "arbitrary", "arbitrary"), vmem_limit_bytes=VMEM_LIMIT),
        name="gdn_z",
    )(z, z, z, z, ba, conv_state, conv_state, conv_state, s0, cw, cw, cw, alog_pad, dtb_pad, nw)


def _seq_body(x_ref, gain_ref, w_ref, wba_ref,
              lcs_ref, h0_ref, lcw_ref, lcb_ref, wax_ref, lba_ref, lbx_ref, lalog_ref,
              csq_ref, csk_ref, csv_ref, s0_ref, cwq_ref, cwk_ref, cwv_ref, galog_ref, dtb_ref, nw_ref,
              lru_out_ref, hout_ref, lcsout_ref, gdn_out_ref, sout_ref, gcsout_ref,
              lxp_scr, a_scr, b_scr, h_scr, gxp_scr, s_scr, *, bt, tt, chunk, reset_first):
    j = pl.program_id(1)

    @pl.when(j == 0)
    def _():
        lxp_scr[:, 5:8, :] = lcs_ref[...]
        h_scr[...] = h0_ref[...]

    _gdn_init_state(j, (csq_ref, csk_ref, csv_ref), s0_ref, gxp_scr, s_scr)

    u = _rms_norm_bf16(x_ref[...], gain_ref[...])
    z = jnp.dot(u, w_ref[...], preferred_element_type=F32)
    ba = jnp.dot(u, wba_ref[...], preferred_element_type=F32)
    seg = lambda c: z[:, c * SEG:(c + 1) * SEG]

    stages = iter(_lru_stages(seg(0), seg(1), j, lcw_ref, lcb_ref, wax_ref, lba_ref, lbx_ref, lalog_ref,
                              lru_out_ref, lxp_scr, a_scr, b_scr, h_scr, bt=bt, tt=tt,
                              reset_first=reset_first))

    def fill():
        stage = next(stages, None)
        if stage is not None:
            stage()

    _gdn_core(seg(2 + COL_Q), seg(2 + COL_K), seg(2 + COL_V), seg(2 + COL_GDN_GATE), ba,
              (cwq_ref, cwk_ref, cwv_ref), galog_ref, dtb_ref, nw_ref, gdn_out_ref, gxp_scr, s_scr,
              0, bt=bt, tt=tt, chunk=chunk, hg=GDN_HEADS, seq_group=1, fill=fill)
    for stage in stages:
        stage()

    @pl.when(j == pl.num_programs(1) - 1)
    def _():
        hout_ref[...] = h_scr[...]
        lcsout_ref[...] = lxp_scr[:, 5:8, :]
        sout_ref[...] = s_scr[...]
        for idx in range(3):
            gcsout_ref[:, :, idx * SEG:(idx + 1) * SEG] = gxp_scr[idx, :, 5:8, :]


def _seq(x2, gain, w_seq, w_ba, lru_conv, h0, lcw, lcb, wax, lba, lbx, lalog,
         gdn_conv, s0, gcw, galog_pad, dtb_pad, nw, *, nb, t, bt, tt, chunk, reset_first):
    nt = t // tt
    rows = bt * tt
    hg = GDN_HEADS
    body = functools.partial(_seq_body, bt=bt, tt=tt, chunk=chunk, reset_first=reset_first)
    const2 = lambda i, j: (0, 0)
    vec = lambda n: pl.BlockSpec((1, n), const2)
    cs_spec = lambda seg: pl.BlockSpec((bt, CONV_W - 1, SEG), lambda i, j: (i, 0, seg))
    cw_spec = lambda seg: pl.BlockSpec((CONV_W, SEG), lambda i, j: (0, seg))
    state_spec = pl.BlockSpec((bt, hg, GDN_DK, GDN_DV), lambda i, j: (i, 0, 0, 0))
    return pl.pallas_call(
        body,
        grid=(nb // bt, nt),
        in_specs=[
            pl.BlockSpec((rows, D_MODEL), lambda i, j: (i * nt + j, 0)),
            vec(D_MODEL),
            _resident((D_MODEL, LRU_Z_WIDTH + GDN_Z_WIDTH), const2),
            _resident((D_MODEL, LANES), const2),
            pl.BlockSpec((bt, CONV_W - 1, LRU_WIDTH), lambda i, j: (i, 0, 0)),
            pl.BlockSpec((bt, 1, LRU_WIDTH), lambda i, j: (i, 0, 0)),
            pl.BlockSpec((CONV_W, LRU_WIDTH), const2),
            vec(LRU_WIDTH),
            pl.BlockSpec((LRU_BLOCKS, LRU_BLOCK, 2 * LRU_BLOCK), lambda i, j: (0, 0, 0)),
            vec(LRU_WIDTH), vec(LRU_WIDTH), vec(LRU_WIDTH),
            cs_spec(0), cs_spec(1), cs_spec(2),
            state_spec,
            cw_spec(0), cw_spec(1), cw_spec(2),
            vec(LANES), vec(LANES), vec(LANES),
        ],
        out_specs=[
            pl.BlockSpec((rows, LRU_WIDTH), lambda i, j: (i * nt + j, 0)),
            pl.BlockSpec((bt, 1, LRU_WIDTH), lambda i, j: (i, 0, 0)),
            pl.BlockSpec((bt, CONV_W - 1, LRU_WIDTH), lambda i, j: (i, 0, 0)),
            pl.BlockSpec((rows, GDN_VAL_W), lambda i, j: (i * nt + j, 0)),
            state_spec,
            pl.BlockSpec((bt, CONV_W - 1, GDN_QKV_W), lambda i, j: (i, 0, 0)),
        ],
        out_shape=[
            jax.ShapeDtypeStruct((nb * t, LRU_WIDTH), BF16),
            jax.ShapeDtypeStruct((nb, 1, LRU_WIDTH), F32),
            jax.ShapeDtypeStruct((nb, CONV_W - 1, LRU_WIDTH), F32),
            jax.ShapeDtypeStruct((nb * t, GDN_VAL_W), BF16),
            jax.ShapeDtypeStruct((nb, GDN_HEADS, GDN_DK, GDN_DV), F32),
            jax.ShapeDtypeStruct((nb, CONV_W - 1, GDN_QKV_W), F32),
        ],
        scratch_shapes=_lru_scratch(bt, tt) + [
            pltpu.VMEM((3, bt, SUBLANES + tt, SEG), F32),
            pltpu.VMEM((bt, hg, GDN_DK, GDN_DV), F32),
        ],
        compiler_params=pltpu.CompilerParams(
            dimension_semantics=("arbitrary", "arbitrary"), vmem_limit_bytes=SEQ_VMEM_LIMIT),
        name="seq",
    )(x2, gain, w_seq, w_ba, lru_conv, h0, lcw, lcb, wax, lba, lbx, lalog,
      gdn_conv, gdn_conv, gdn_conv, s0, gcw, gcw, gcw, galog_pad, dtb_pad, nw)


def _out_body(x_ref, gpre_ref, wm_ref, lru_ref, gdn_ref, wl_ref, wg_ref, wo_ref, gpost_ref, y_ref):
    x = x_ref[...]
    u = _rms_norm_bf16(x, gpre_ref[...])
    m = jnp.dot(u, wm_ref[...], preferred_element_type=F32)
    p_lru = jnp.dot(lru_ref[...], wl_ref[...], preferred_element_type=F32)
    p_gdn = jnp.dot(gdn_ref[...], wg_ref[...], preferred_element_type=F32)
    merged = jax.nn.sigmoid(m[:, :D_MODEL]) * p_lru + jax.nn.sigmoid(m[:, D_MODEL:]) * p_gdn
    y = jnp.dot(merged.astype(BF16), wo_ref[...], preferred_element_type=F32)
    ms = jnp.mean(y * y, axis=-1, keepdims=True)
    y_ref[...] = x + (y * lax.rsqrt(ms + NORM_EPS)) * gpost_ref[...]


def _out_proj(x2, gpre, w_m, lru_out, gdn_out, wl, wg, wo, gpost, *, tm):
    n = x2.shape[0]
    wspec = lambda: pl.BlockSpec((D_MODEL, D_MODEL), lambda i: (0, 0))
    vec = lambda: pl.BlockSpec((1, D_MODEL), lambda i: (0, 0))
    rowspec = lambda: pl.BlockSpec((tm, D_MODEL), lambda i: (i, 0))
    return pl.pallas_call(
        _out_body,
        grid=(n // tm,),
        in_specs=[
            rowspec(), vec(),
            pl.BlockSpec((D_MODEL, 2 * D_MODEL), lambda i: (0, 0)),
            rowspec(), rowspec(),
            wspec(), wspec(), wspec(),
            vec(),
        ],
        out_specs=rowspec(),
        out_shape=jax.ShapeDtypeStruct((n, D_MODEL), F32),
        compiler_params=pltpu.CompilerParams(
            dimension_semantics=("arbitrary",), vmem_limit_bytes=VMEM_LIMIT),
        name="out_proj",
    )(x2, gpre, w_m, lru_out, gdn_out, wl, wg, wo, gpost)


def _prep_weights(norm_pre, norm_post, w_in, lru_conv_w, lru_conv_b, lru_wa, lru_ba, lru_wx, lru_bx,
                  lru_a_logit, gdn_conv_w, gdn_A_log, gdn_dt_bias, gdn_norm_w, w_br_lru, w_br_gdn, w_out):
    c_gdn = LRU_Z_WIDTH
    c_ba = c_gdn + GDN_Z_WIDTH
    c_m = c_ba + 2 * GDN_HEADS
    w_ba = jnp.pad(w_in[:, c_ba:c_m], ((0, 0), (0, LANES - 2 * GDN_HEADS))).astype(BF16)
    wax = jnp.concatenate([lru_wa, lru_wx], axis=-1).astype(BF16)
    pad_alpha = lambda v: jnp.pad(v.reshape(1, GDN_HEADS), ((0, 0), (ALPHA_LANE, LANES - ALPHA_LANE - GDN_HEADS)))
    row = lambda v: v.reshape(1, -1)
    w_seq = w_in[:, :c_ba].astype(BF16)
    return dict(
        norm_pre=row(norm_pre), norm_post=row(norm_post),
        w_seq=w_seq, w_lru=w_seq[:, :c_gdn], w_gdn=w_seq[:, c_gdn:], w_ba=w_ba,
        w_m=w_in[:, c_m:].astype(BF16),
        lru_cw=lru_conv_w, lru_cb=row(lru_conv_b), wax=wax, lru_ba=row(lru_ba), lru_bx=row(lru_bx),
        lru_alog=row(lru_a_logit), gdn_cw=gdn_conv_w, alog_pad=pad_alpha(gdn_A_log),
        dtb_pad=pad_alpha(gdn_dt_bias), gdn_nw=row(gdn_norm_w),
        wl=w_br_lru.astype(BF16), wg=w_br_gdn.astype(BF16), wo=w_out.astype(BF16))


def _layer(x, lru_conv, lru_h, gdn_conv, gdn_s, p, *, reset_first, fused, tm, lru_bt=None, lru_tt=None,
           gdn_bt=None, gdn_tt=None):
    nb, t, _ = x.shape
    x2 = x.reshape(nb * t, D_MODEL)
    h0 = lru_h.reshape(nb, 1, LRU_WIDTH)
    chunk = min(GDN_CHUNK, t)
    lru_w = (p["lru_cw"], p["lru_cb"], p["wax"], p["lru_ba"], p["lru_bx"], p["lru_alog"])
    gdn_w = (p["gdn_cw"], p["alog_pad"], p["dtb_pad"], p["gdn_nw"])
    if fused:
        lru_out, h_last, lru_conv_new, gdn_out, s_new, gdn_conv_new = _seq(
            x2, p["norm_pre"], p["w_seq"], p["w_ba"], lru_conv, h0, *lru_w, gdn_conv, gdn_s, *gdn_w,
            nb=nb, t=t, bt=1, tt=tm, chunk=chunk, reset_first=reset_first)
    else:
        lru_out, h_last, lru_conv_new = _lru(
            x2, p["norm_pre"], p["w_lru"], lru_conv, h0, *lru_w, nb=nb, t=t, bt=lru_bt, tt=lru_tt,
            reset_first=reset_first)
        z, ba = _in_proj(x2, p["norm_pre"], p["w_gdn"], p["w_ba"], tm=nb * t, tn=SEG)
        gdn_out, s_new = _gdn_z(z, ba, gdn_conv, gdn_s, *gdn_w, nb=nb, t=t, bt=gdn_bt, tt=gdn_tt, chunk=chunk,
                                hg=GDN_HEADS)
        gdn_conv_new = z.reshape(nb, t, GDN_Z_WIDTH)[:, t - (CONV_W - 1):, :GDN_QKV_W]
    y = _out_proj(x2, p["norm_pre"], p["w_m"], lru_out, gdn_out, p["wl"], p["wg"], p["wo"], p["norm_post"],
                  tm=min(512, nb * t))
    return y.reshape(nb, t, D_MODEL), lru_conv_new, h_last.reshape(nb, LRU_WIDTH), gdn_conv_new, s_new


def kernel(x_prompt, x_sample, state_lru_conv, state_lru_h, state_gdn_conv, state_gdn_S, norm_pre, norm_post, w_in, lru_conv_w, lru_conv_b, lru_wa, lru_ba, lru_wx, lru_bx, lru_a_logit, gdn_conv_w, gdn_A_log, gdn_dt_bias, gdn_norm_w, w_br_lru, w_br_gdn, w_out):
    depth = w_in.shape[0]
    assert depth == 1
    nb = x_prompt.shape[0]
    p = _prep_weights(norm_pre[0], norm_post[0], w_in[0], lru_conv_w[0], lru_conv_b[0], lru_wa[0], lru_ba[0],
                      lru_wx[0], lru_bx[0], lru_a_logit[0], gdn_conv_w[0], gdn_A_log[0], gdn_dt_bias[0],
                      gdn_norm_w[0], w_br_lru[0], w_br_gdn[0], w_out[0])
    yp, p_lc, p_lh, p_gc, p_gs = _layer(
        x_prompt,
        jnp.zeros((nb, CONV_W - 1, LRU_WIDTH), F32), jnp.zeros((nb, LRU_WIDTH), F32),
        jnp.zeros((nb, CONV_W - 1, GDN_QKV_W), F32), jnp.zeros((nb, GDN_HEADS, GDN_DK, GDN_DV), F32),
        p, reset_first=True, fused=True, tm=256)
    ys, s_lc, s_lh, s_gc, s_gs = _layer(
        x_sample, state_lru_conv[0], state_lru_h[0], state_gdn_conv[0], state_gdn_S[0],
        p, reset_first=False, fused=False, tm=1024, lru_bt=32, lru_tt=8, gdn_bt=8, gdn_tt=8)
    return (yp, ys, p_lc[None], p_lh[None], p_gc[None], p_gs[None],
            s_lc[None], s_lh[None], s_gc[None], s_gs[None])
```

```python
import functools

import jax
import jax.numpy as jnp
from jax import lax
from jax.experimental import pallas as pl
from jax.experimental.pallas import tpu as pltpu

F32 = jnp.float32
BF16 = jnp.bfloat16

D_MODEL = 1024
CONV_W = 4
LRU_WIDTH = 1024
LRU_BLOCKS = 8
LRU_BLOCK = LRU_WIDTH // LRU_BLOCKS
LRU_C = 8.0
GDN_HEADS = 8
GDN_DK = 128
GDN_DV = 128
GDN_KEY_W = GDN_HEADS * GDN_DK
GDN_VAL_W = GDN_HEADS * GDN_DV
GDN_QKV_W = 2 * GDN_KEY_W + GDN_VAL_W
GDN_CHUNK = 64
NORM_EPS = 1e-6

LANES = 128
SUBLANES = 8
SEG = 1024
LRU_Z_WIDTH = 2 * SEG
GDN_Z_WIDTH = 4 * SEG
COL_Q, COL_K, COL_V, COL_GDN_GATE = range(4)
BETA_LANE = 0
ALPHA_LANE = GDN_HEADS
VMEM_LIMIT = 48 * 1024 * 1024
SEQ_VMEM_LIMIT = 56 * 1024 * 1024
GDN_SEQ_GROUP = 2
LRU_CARRY_PIECE = 8
FILL_EVERY = 4


def _softplus(x):
    return jnp.maximum(x, 0.0) + jnp.log1p(jnp.exp(-jnp.abs(x)))


def _silu(x):
    return x * jax.nn.sigmoid(x)


def _fdot(a, b, dims=((1,), (0,))):
    return lax.dot_general(a, b, (dims, ((), ())), preferred_element_type=F32)


def _rms_norm_bf16(x, gain):
    ms = jnp.mean(x * x, axis=-1, keepdims=True)
    return ((x * lax.rsqrt(ms + NORM_EPS)) * gain).astype(BF16)


def _resident(shape, index_map):
    return pl.BlockSpec(shape, index_map, pipeline_mode=pl.Buffered(1))


def _in_proj_body(x_ref, gain_ref, w_ref, wba_ref, z_ref, ba_ref, u_scr):
    @pl.when(pl.program_id(1) == 0)
    def _():
        u = _rms_norm_bf16(x_ref[...], gain_ref[...])
        u_scr[...] = u
        ba_ref[...] = jnp.dot(u, wba_ref[...], preferred_element_type=F32)

    z_ref[...] = jnp.dot(u_scr[...], w_ref[...], preferred_element_type=F32)


def _in_proj(x2, gain, w, w_ba, *, tm, tn, col0, width):
    n = x2.shape[0]
    return pl.pallas_call(
        _in_proj_body,
        grid=(n // tm, width // tn),
        in_specs=[
            pl.BlockSpec((tm, D_MODEL), lambda i, j: (i, 0)),
            pl.BlockSpec((1, D_MODEL), lambda i, j: (0, 0)),
            pl.BlockSpec((D_MODEL, tn), lambda i, j: (0, col0 + j)),
            pl.BlockSpec((D_MODEL, LANES), lambda i, j: (0, 0)),
        ],
        out_specs=[
            pl.BlockSpec((tm, tn), lambda i, j: (i, j)),
            pl.BlockSpec((tm, LANES), lambda i, j: (i, 0)),
        ],
        out_shape=[
            jax.ShapeDtypeStruct((n, width), F32),
            jax.ShapeDtypeStruct((n, LANES), F32),
        ],
        scratch_shapes=[pltpu.VMEM((tm, D_MODEL), BF16)],
        compiler_params=pltpu.CompilerParams(
            dimension_semantics=("arbitrary", "arbitrary"), vmem_limit_bytes=VMEM_LIMIT),
        name="in_proj",
    )(x2, gain, w, w_ba)


def _lru_stages(z_lru, gate, j, cw_ref, cb_ref, wax_ref, ba_ref, bx_ref, alog_ref, out_ref,
                xp_scr, a_scr, b_scr, h_scr, *, bt, tt, reset_first):
    rows = bt * tt
    width = LRU_WIDTH
    n_vreg_rows = rows // SUBLANES
    n_groups = tt // SUBLANES
    env = {}

    def conv():
        cur = z_lru().reshape(bt, tt, width)
        xp_scr[:, 8:8 + tt, :] = cur
        cw = cw_ref[...]
        xc = cur * cw[CONV_W - 1].reshape(1, 1, width)
        for i in range(CONV_W - 2, -1, -1):
            xc = xc + xp_scr[:, 5 + i:5 + i + tt, :] * cw[i].reshape(1, 1, width)
        xp_scr[:, 5:8, :] = xp_scr[:, 5 + tt:8 + tt, :]
        env["xc"] = (xc + cb_ref[...].reshape(1, 1, width)).reshape(rows, width)
        a_l = alog_ref[...]
        env["log_sig_a"] = jnp.minimum(a_l, 0.0) - jnp.log1p(jnp.exp(-jnp.abs(a_l)))
        env["sub3"] = lax.broadcasted_iota(jnp.int32, (n_vreg_rows, SUBLANES, LRU_BLOCK), 1)
        if reset_first:
            row = lax.broadcasted_iota(jnp.int32, (rows, LRU_BLOCK), 0)
            env["is_reset"] = jnp.logical_and(row % tt == 0, j == 0)

    def block(g):
        gs = slice(g * LRU_BLOCK, (g + 1) * LRU_BLOCK)
        xg = env["xc"][:, gs]
        pre = jnp.dot(xg.astype(BF16), wax_ref[g], preferred_element_type=F32)
        r = jax.nn.sigmoid(pre[:, :LRU_BLOCK] + ba_ref[:, gs])
        ig = jax.nn.sigmoid(pre[:, LRU_BLOCK:] + bx_ref[:, gs])
        log_a = (LRU_C * r) * env["log_sig_a"][:, gs]
        a = jnp.exp(log_a)
        t = jnp.tanh(log_a)
        m2 = (-2.0 * t) / (1.0 - t)
        mult = jnp.where(m2 > 0.0, m2 * lax.rsqrt(m2), 0.0)
        if reset_first:
            a = jnp.where(env["is_reset"], 0.0, a)
            mult = jnp.where(env["is_reset"], 1.0, mult)
        b = mult * ig * xg
        a = a.reshape(n_vreg_rows, SUBLANES, LRU_BLOCK)
        b = b.reshape(n_vreg_rows, SUBLANES, LRU_BLOCK)
        s = 1
        while s < SUBLANES:
            keep = env["sub3"] >= s
            a_sh = jnp.where(keep, pltpu.roll(a, s, axis=1), 1.0)
            b_sh = jnp.where(keep, pltpu.roll(b, s, axis=1), 0.0)
            b = a * b_sh + b
            a = a * a_sh
            s *= 2
        a_scr[:, :, gs] = a.reshape(bt, tt, LRU_BLOCK)
        b_scr[:, :, gs] = b.reshape(bt, tt, LRU_BLOCK)

    def carry_groups(g0, g1):
        carry = env.get("carry")
        if carry is None:
            carry = h_scr[...]
        for i in range(g0, g1):
            rs = slice(i * SUBLANES, (i + 1) * SUBLANES)
            h = b_scr[:, rs, :] + a_scr[:, rs, :] * carry
            b_scr[:, rs, :] = h
            carry = h[:, SUBLANES - 1:SUBLANES, :]
        env["carry"] = carry
        if g1 == n_groups:
            h_scr[...] = carry

    def finish():
        h = b_scr[...].reshape(rows, width)
        out_ref[...] = (h * _silu(gate())).astype(out_ref.dtype)

    per_piece = min(n_groups, LRU_CARRY_PIECE)
    pieces = [functools.partial(carry_groups, g0, min(g0 + per_piece, n_groups))
              for g0 in range(0, n_groups, per_piece)]
    return [conv] + [functools.partial(block, g) for g in range(LRU_BLOCKS)] + pieces + [finish]


def _lru_body(x_ref, gain_ref, w_ref, cs_ref, h0_ref, cw_ref, cb_ref, wax_ref, ba_ref, bx_ref, alog_ref,
              out_ref, hout_ref, csout_ref, xp_scr, a_scr, b_scr, h_scr, *, bt, tt, reset_first):
    j = pl.program_id(1)

    @pl.when(j == 0)
    def _():
        xp_scr[:, 5:8, :] = cs_ref[...]
        h_scr[...] = h0_ref[...]

    u = _rms_norm_bf16(x_ref[...], gain_ref[...])
    z = jnp.dot(u, w_ref[...], preferred_element_type=F32)
    for stage in _lru_stages(lambda: z[:, :LRU_WIDTH], lambda: z[:, LRU_WIDTH:], j,
                             cw_ref, cb_ref, wax_ref, ba_ref, bx_ref, alog_ref, out_ref, xp_scr, a_scr, b_scr, h_scr, bt=bt, tt=tt,
                             reset_first=reset_first):
        stage()

    @pl.when(j == pl.num_programs(1) - 1)
    def _():
        hout_ref[...] = h_scr[...]
        csout_ref[...] = xp_scr[:, 5:8, :]


def _lru_scratch(bt, tt):
    return [
        pltpu.VMEM((bt, SUBLANES + tt, LRU_WIDTH), F32),
        pltpu.VMEM((bt, tt, LRU_WIDTH), F32),
        pltpu.VMEM((bt, tt, LRU_WIDTH), F32),
        pltpu.VMEM((bt, 1, LRU_WIDTH), F32),
    ]


def _lru(x2, gain, w_lru, conv_state, h0, cw, cb, wax, ba, bx, alog, *, nb, t, bt, tt, reset_first):
    nt = t // tt
    rows = bt * tt
    body = functools.partial(_lru_body, bt=bt, tt=tt, reset_first=reset_first)
    vec = lambda: pl.BlockSpec((1, LRU_WIDTH), lambda i, j: (0, 0))
    return pl.pallas_call(
        body,
        grid=(nb // bt, nt),
        in_specs=[
            pl.BlockSpec((rows, D_MODEL), lambda i, j: (i * nt + j, 0)),
            vec(),
            pl.BlockSpec((D_MODEL, LRU_Z_WIDTH), lambda i, j: (0, 0)),
            pl.BlockSpec((bt, CONV_W - 1, LRU_WIDTH), lambda i, j: (i, 0, 0)),
            pl.BlockSpec((bt, 1, LRU_WIDTH), lambda i, j: (i, 0, 0)),
            pl.BlockSpec((CONV_W, LRU_WIDTH), lambda i, j: (0, 0)),
            vec(),
            pl.BlockSpec((LRU_BLOCKS, LRU_BLOCK, 2 * LRU_BLOCK), lambda i, j: (0, 0, 0)),
            vec(), vec(), vec(),
        ],
        out_specs=[
            pl.BlockSpec((rows, LRU_WIDTH), lambda i, j: (i * nt + j, 0)),
            pl.BlockSpec((bt, 1, LRU_WIDTH), lambda i, j: (i, 0, 0)),
            pl.BlockSpec((bt, CONV_W - 1, LRU_WIDTH), lambda i, j: (i, 0, 0)),
        ],
        out_shape=[
            jax.ShapeDtypeStruct((nb * t, LRU_WIDTH), BF16),
            jax.ShapeDtypeStruct((nb, 1, LRU_WIDTH), F32),
            jax.ShapeDtypeStruct((nb, CONV_W - 1, LRU_WIDTH), F32),
        ],
        scratch_shapes=_lru_scratch(bt, tt),
        compiler_params=pltpu.CompilerParams(
            dimension_semantics=("arbitrary", "arbitrary"), vmem_limit_bytes=VMEM_LIMIT),
        name="lru",
    )(x2, gain, w_lru, conv_state, h0, cw, cb, wax, ba, bx, alog)


def _gdn_core(zq, zk, zv, zgate, ba, cw_refs, alog_ref, dtb_ref, nw_ref, out_ref, xp_scr, s_scr,
              h_grp, *, bt, tt, chunk, hg, seq_group, fill=lambda: None):
    rows = bt * tt
    width = LANES * hg
    n_chunks = tt // chunk

    def conv_silu(idx, cur2, cw_ref):
        cur = cur2.reshape(bt, tt, width)
        xp_scr[idx, :, 8:8 + tt, :] = cur
        cw = cw_ref[...]
        y = cur * cw[CONV_W - 1].reshape(1, 1, width)
        for i in range(CONV_W - 2, -1, -1):
            y = y + xp_scr[idx, :, 5 + i:5 + i + tt, :] * cw[i].reshape(1, 1, width)
        xp_scr[idx, :, 5:8, :] = xp_scr[idx, :, 5 + tt:8 + tt, :]
        return _silu(y.reshape(rows, width))

    q_all = conv_silu(0, zq(), cw_refs[0])
    k_all = conv_silu(1, zk(), cw_refs[1])
    v_all = conv_silu(2, zv(), cw_refs[2])
    ba = ba()

    beta_all = jax.nn.sigmoid(ba)
    g_all = -jnp.exp(alog_ref[...]) * _softplus(ba + dtb_ref[...])
    row = lax.broadcasted_iota(jnp.int32, (rows, LANES), 0)
    row_in_chunk = row % chunk
    gc_all = g_all
    s = 1
    while s < chunk:
        gc_all = gc_all + jnp.where(row_in_chunk >= s, pltpu.roll(gc_all, s, axis=0), 0.0)
        s *= 2
    gct_all = gc_all.T

    lane = lax.broadcasted_iota(jnp.int32, (rows, LANES), 1)
    sub = lax.broadcasted_iota(jnp.int32, (LANES, rows), 0)
    ri = lax.broadcasted_iota(jnp.int32, (rows, rows), 0)
    ci = lax.broadcasted_iota(jnp.int32, (rows, rows), 1)
    same_chunk = (ri // chunk) == (ci // chunk)
    causal = jnp.logical_and(same_chunk, ri >= ci)
    strict = jnp.logical_and(same_chunk, ri > ci)
    n_steps = (chunk - 1).bit_length()

    hd = []
    for hh in range(hg):
        head = h_grp * hg + hh
        ls = slice(hh * LANES, (hh + 1) * LANES)
        qh, kh, vh = q_all[:, ls], k_all[:, ls], v_all[:, ls]
        qh = qh * lax.rsqrt(jnp.sum(qh * qh, axis=-1, keepdims=True) + NORM_EPS) * (GDN_DK ** -0.5)
        kh = kh * lax.rsqrt(jnp.sum(kh * kh, axis=-1, keepdims=True) + NORM_EPS)

        beta = jnp.sum(jnp.where(lane == BETA_LANE + head, beta_all, 0.0), axis=1, keepdims=True)
        gc = jnp.sum(jnp.where(lane == ALPHA_LANE + head, gc_all, 0.0), axis=1, keepdims=True)
        gc_row = jnp.sum(jnp.where(sub == ALPHA_LANE + head, gct_all, 0.0), axis=0, keepdims=True)

        decay = jnp.where(causal, jnp.exp(jnp.where(causal, gc - gc_row, 0.0)), 0.0)
        kb = kh * beta
        gram = _fdot(jnp.concatenate([kb, qh], axis=0), kh, ((1,), (1,)))
        e_gc = jnp.exp(gc)
        hd.append(dict(
            ls=ls, kh=kh, gc=gc,
            p=jnp.where(strict, -(gram[:rows] * decay), 0.0),
            a_qk=gram[rows:] * decay,
            x=jnp.concatenate([vh * beta, kb * e_gc], axis=1),
            q_dec=qh * e_gc))

    n_dots = 0
    for st in range(n_steps):
        for d in hd:
            if st == 0:
                d["n"] = d["p"]
                if n_steps > 1:
                    d["p"] = _fdot(d["p"], d["p"])
            elif st < n_steps - 1:
                res = _fdot(jnp.concatenate([d["n"], d["p"]], axis=0), d["p"])
                d["n"] = d["n"] + d["p"] + res[:rows]
                d["p"] = res[rows:]
            else:
                d["n"] = d["n"] + d["p"] + _fdot(d["n"], d["p"])
            n_dots += 1
            if n_dots % FILL_EVERY == 0:
                fill()
    for d in hd:
        d["x"] = d["x"] + _fdot(d["n"], d["x"])

    for d in hd:
        d["v_new"] = {}
        d["qs"] = {}
    for sq0 in range(0, bt, seq_group):
        pairs = [(sq, hh) for sq in range(sq0, min(sq0 + seq_group, bt)) for hh in range(hg)]
        state = {pr: s_scr[pr[0], pr[1]] for pr in pairs}
        for c in range(n_chunks):
            rsl = {sq: slice(sq * tt + c * chunk, sq * tt + (c + 1) * chunk) for sq, _ in pairs}
            ws, v_new, k_dec, e_last, upd = {}, {}, {}, {}, {}
            for pr in pairs:
                d, rs = hd[pr[1]], rsl[pr[0]]
                ws[pr] = _fdot(jnp.concatenate([d["x"][rs, LANES:], d["q_dec"][rs]], axis=0), state[pr])
            for pr in pairs:
                d, rs = hd[pr[1]], rsl[pr[0]]
                v_new[pr] = d["x"][rs, :LANES] - ws[pr][:chunk]
                g_last = d["gc"][rs.stop - 1:rs.stop]
                k_dec[pr] = d["kh"][rs] * jnp.exp(g_last - d["gc"][rs])
                e_last[pr] = jnp.exp(g_last)
                d["v_new"][(pr[0], c)] = v_new[pr]
                d["qs"][(pr[0], c)] = ws[pr][chunk:]
            for pr in pairs:
                upd[pr] = _fdot(k_dec[pr], v_new[pr], ((0,), (0,)))
            for pr in pairs:
                state[pr] = state[pr] * e_last[pr] + upd[pr]
            fill()
        for pr in pairs:
            s_scr[pr[0], pr[1]] = state[pr]

    order = [(sq, c) for sq in range(bt) for c in range(n_chunks)]
    gate = zgate()
    for d in hd:
        v_parts = [d["v_new"][k] for k in order]
        q_parts = [d["qs"][k] for k in order]
        v_new_all = jnp.concatenate(v_parts, axis=0) if len(v_parts) > 1 else v_parts[0]
        qs_all = jnp.concatenate(q_parts, axis=0) if len(q_parts) > 1 else q_parts[0]
        o = qs_all + _fdot(d["a_qk"], v_new_all)
        o = o * lax.rsqrt(jnp.mean(o * o, axis=-1, keepdims=True) + NORM_EPS) * nw_ref[...]
        o = o * _silu(gate[:, d["ls"]])
        out_ref[:, d["ls"]] = o.astype(out_ref.dtype)


def _gdn_init_state(j, cs_refs, s0_ref, xp_scr, s_scr):
    @pl.when(j == 0)
    def _():
        for idx, cs_ref in enumerate(cs_refs):
            xp_scr[idx, :, 5:8, :] = cs_ref[...]
        s_scr[...] = s0_ref[...]


def _gdn_z_body(zq_ref, zk_ref, zv_ref, zg_ref, ba_ref, csq_ref, csk_ref, csv_ref, s0_ref,
                cwq_ref, cwk_ref, cwv_ref, alog_ref, dtb_ref, nw_ref,
                out_ref, sout_ref, xp_scr, s_scr, *, bt, tt, chunk, hg):
    j = pl.program_id(2)
    _gdn_init_state(j, (csq_ref, csk_ref, csv_ref), s0_ref, xp_scr, s_scr)
    _gdn_core(lambda: zq_ref[...], lambda: zk_ref[...], lambda: zv_ref[...], lambda: zg_ref[...],
              lambda: ba_ref[...],
              (cwq_ref, cwk_ref, cwv_ref), alog_ref, dtb_ref, nw_ref, out_ref, xp_scr, s_scr,
              pl.program_id(1), bt=bt, tt=tt, chunk=chunk, hg=hg, seq_group=GDN_SEQ_GROUP)

    @pl.when(j == pl.num_programs(2) - 1)
    def _():
        sout_ref[...] = s_scr[...]


def _gdn_z(z, ba, conv_state, s0, cw, alog_pad, dtb_pad, nw, *, nb, t, bt, tt, chunk, hg):
    nt = t // tt
    rows = bt * tt
    width = LANES * hg
    per = SEG // width
    body = functools.partial(_gdn_z_body, bt=bt, tt=tt, chunk=chunk, hg=hg)

    def zspec(col):
        return pl.BlockSpec((rows, width), lambda i, h, j: (i * nt + j, col * per + h))

    def cs_spec(seg):
        return pl.BlockSpec((bt, CONV_W - 1, width), lambda i, h, j: (i, 0, seg * per + h))

    def cw_spec(seg):
        return pl.BlockSpec((CONV_W, width), lambda i, h, j: (0, seg * per + h))

    vec = lambda: pl.BlockSpec((1, LANES), lambda i, h, j: (0, 0))
    return pl.pallas_call(
        body,
        grid=(nb // bt, GDN_HEADS // hg, nt),
        in_specs=[
            zspec(COL_Q), zspec(COL_K), zspec(COL_V), zspec(COL_GDN_GATE),
            pl.BlockSpec((rows, LANES), lambda i, h, j: (i * nt + j, 0)),
            cs_spec(0), cs_spec(1), cs_spec(2),
            pl.BlockSpec((bt, hg, GDN_DK, GDN_DV), lambda i, h, j: (i, h, 0, 0)),
            cw_spec(0), cw_spec(1), cw_spec(2),
            vec(), vec(), vec(),
        ],
        out_specs=[
            pl.BlockSpec((rows, width), lambda i, h, j: (i * nt + j, h)),
            pl.BlockSpec((bt, hg, GDN_DK, GDN_DV), lambda i, h, j: (i, h, 0, 0)),
        ],
        out_shape=[
            jax.ShapeDtypeStruct((nb * t, GDN_VAL_W), BF16),
            jax.ShapeDtypeStruct((nb, GDN_HEADS, GDN_DK, GDN_DV), F32),
        ],
        scratch_shapes=[
            pltpu.VMEM((3, bt, SUBLANES + tt, width), F32),
            pltpu.VMEM((bt, hg, GDN_DK, GDN_DV), F32),
        ],
        compiler_params=pltpu.CompilerParams(
            dimension_semantics=("arbitrary", "arbitrary", "arbitrary"), vmem_limit_bytes=VMEM_LIMIT),
        name="gdn_z",
    )(z, z, z, z, ba, conv_state, conv_state, conv_state, s0, cw, cw, cw, alog_pad, dtb_pad, nw)


def _seq_body(x_ref, gain_ref, w_ref, wba_ref,
              lcs_ref, h0_ref, lcw_ref, lcb_ref, wax_ref, lba_ref, lbx_ref, lalog_ref,
              csq_ref, csk_ref, csv_ref, s0_ref, cwq_ref, cwk_ref, cwv_ref, galog_ref, dtb_ref, nw_ref,
              lru_out_ref, hout_ref, lcsout_ref, gdn_out_ref, sout_ref, gcsout_ref,
              lxp_scr, a_scr, b_scr, h_scr, gxp_scr, s_scr, *, bt, tt, chunk, reset_first):
    j = pl.program_id(1)

    @pl.when(j == 0)
    def _():
        lxp_scr[:, 5:8, :] = lcs_ref[...]
        h_scr[...] = h0_ref[...]

    _gdn_init_state(j, (csq_ref, csk_ref, csv_ref), s0_ref, gxp_scr, s_scr)

    u = _rms_norm_bf16(x_ref[...], gain_ref[...])

    def seg(c):
        return lambda: jnp.dot(u, w_ref[:, c * SEG:(c + 1) * SEG], preferred_element_type=F32)

    ba = lambda: jnp.dot(u, wba_ref[...], preferred_element_type=F32)

    stages = iter(_lru_stages(seg(0), seg(1), j, lcw_ref, lcb_ref, wax_ref, lba_ref, lbx_ref, lalog_ref,
                              lru_out_ref, lxp_scr, a_scr, b_scr, h_scr, bt=bt, tt=tt,
                              reset_first=reset_first))

    def fill():
        stage = next(stages, None)
        if stage is not None:
            stage()

    _gdn_core(seg(2 + COL_Q), seg(2 + COL_K), seg(2 + COL_V), seg(2 + COL_GDN_GATE), ba,
              (cwq_ref, cwk_ref, cwv_ref), galog_ref, dtb_ref, nw_ref, gdn_out_ref, gxp_scr, s_scr,
              0, bt=bt, tt=tt, chunk=chunk, hg=GDN_HEADS, seq_group=1, fill=fill)
    for stage in stages:
        stage()

    @pl.when(j == pl.num_programs(1) - 1)
    def _():
        hout_ref[...] = h_scr[...]
        lcsout_ref[...] = lxp_scr[:, 5:8, :]
        sout_ref[...] = s_scr[...]
        for idx in range(3):
            gcsout_ref[:, :, idx * SEG:(idx + 1) * SEG] = gxp_scr[idx, :, 5:8, :]


def _seq(x2, gain, w_seq, w_ba, lru_conv, h0, lcw, lcb, wax, lba, lbx, lalog,
         gdn_conv, s0, gcw, galog_pad, dtb_pad, nw, *, nb, t, bt, tt, chunk, reset_first):
    nt = t // tt
    rows = bt * tt
    hg = GDN_HEADS
    body = functools.partial(_seq_body, bt=bt, tt=tt, chunk=chunk, reset_first=reset_first)
    const2 = lambda i, j: (0, 0)
    vec = lambda n: pl.BlockSpec((1, n), const2)
    cs_spec = lambda seg: pl.BlockSpec((bt, CONV_W - 1, SEG), lambda i, j: (i, 0, seg))
    cw_spec = lambda seg: pl.BlockSpec((CONV_W, SEG), lambda i, j: (0, seg))
    state_spec = pl.BlockSpec((bt, hg, GDN_DK, GDN_DV), lambda i, j: (i, 0, 0, 0))
    return pl.pallas_call(
        body,
        grid=(nb // bt, nt),
        in_specs=[
            pl.BlockSpec((rows, D_MODEL), lambda i, j: (i * nt + j, 0)),
            vec(D_MODEL),
            _resident((D_MODEL, LRU_Z_WIDTH + GDN_Z_WIDTH), const2),
            _resident((D_MODEL, LANES), const2),
            pl.BlockSpec((bt, CONV_W - 1, LRU_WIDTH), lambda i, j: (i, 0, 0)),
            pl.BlockSpec((bt, 1, LRU_WIDTH), lambda i, j: (i, 0, 0)),
            pl.BlockSpec((CONV_W, LRU_WIDTH), const2),
            vec(LRU_WIDTH),
            pl.BlockSpec((LRU_BLOCKS, LRU_BLOCK, 2 * LRU_BLOCK), lambda i, j: (0, 0, 0)),
            vec(LRU_WIDTH), vec(LRU_WIDTH), vec(LRU_WIDTH),
            cs_spec(0), cs_spec(1), cs_spec(2),
            state_spec,
            cw_spec(0), cw_spec(1), cw_spec(2),
            vec(LANES), vec(LANES), vec(LANES),
        ],
        out_specs=[
            pl.BlockSpec((rows, LRU_WIDTH), lambda i, j: (i * nt + j, 0)),
            pl.BlockSpec((bt, 1, LRU_WIDTH), lambda i, j: (i, 0, 0)),
            pl.BlockSpec((bt, CONV_W - 1, LRU_WIDTH), lambda i, j: (i, 0, 0)),
            pl.BlockSpec((rows, GDN_VAL_W), lambda i, j: (i * nt + j, 0)),
            state_spec,
            pl.BlockSpec((bt, CONV_W - 1, GDN_QKV_W), lambda i, j: (i, 0, 0)),
        ],
        out_shape=[
            jax.ShapeDtypeStruct((nb * t, LRU_WIDTH), BF16),
            jax.ShapeDtypeStruct((nb, 1, LRU_WIDTH), F32),
            jax.ShapeDtypeStruct((nb, CONV_W - 1, LRU_WIDTH), F32),
            jax.ShapeDtypeStruct((nb * t, GDN_VAL_W), BF16),
            jax.ShapeDtypeStruct((nb, GDN_HEADS, GDN_DK, GDN_DV), F32),
            jax.ShapeDtypeStruct((nb, CONV_W - 1, GDN_QKV_W), F32),
        ],
        scratch_shapes=_lru_scratch(bt, tt) + [
            pltpu.VMEM((3, bt, SUBLANES + tt, SEG), F32),
            pltpu.VMEM((bt, hg, GDN_DK, GDN_DV), F32),
        ],
        compiler_params=pltpu.CompilerParams(
            dimension_semantics=("arbitrary", "arbitrary"), vmem_limit_bytes=SEQ_VMEM_LIMIT),
        name="seq",
    )(x2, gain, w_seq, w_ba, lru_conv, h0, lcw, lcb, wax, lba, lbx, lalog,
      gdn_conv, gdn_conv, gdn_conv, s0, gcw, gcw, gcw, galog_pad, dtb_pad, nw)


def _out_body(x_ref, gpre_ref, wm_ref, lru_ref, gdn_ref, wl_ref, wg_ref, wo_ref, gpost_ref, y_ref):
    x = x_ref[...]
    u = _rms_norm_bf16(x, gpre_ref[...])
    m = jnp.dot(u, wm_ref[...], preferred_element_type=F32)
    p_lru = jnp.dot(lru_ref[...], wl_ref[...], preferred_element_type=F32)
    p_gdn = jnp.dot(gdn_ref[...], wg_ref[...], preferred_element_type=F32)
    merged = jax.nn.sigmoid(m[:, :D_MODEL]) * p_lru + jax.nn.sigmoid(m[:, D_MODEL:]) * p_gdn
    y = jnp.dot(merged.astype(BF16), wo_ref[...], preferred_element_type=F32)
    ms = jnp.mean(y * y, axis=-1, keepdims=True)
    y_ref[...] = x + (y * lax.rsqrt(ms + NORM_EPS)) * gpost_ref[...]


def _out_proj(x2, gpre, w_m, lru_out, gdn_out, wl, wg, wo, gpost, *, tm):
    n = x2.shape[0]
    wspec = lambda: pl.BlockSpec((D_MODEL, D_MODEL), lambda i: (0, 0))
    vec = lambda: pl.BlockSpec((1, D_MODEL), lambda i: (0, 0))
    rowspec = lambda: pl.BlockSpec((tm, D_MODEL), lambda i: (i, 0))
    return pl.pallas_call(
        _out_body,
        grid=(n // tm,),
        in_specs=[
            rowspec(), vec(),
            pl.BlockSpec((D_MODEL, 2 * D_MODEL), lambda i: (0, 0)),
            rowspec(), rowspec(),
            wspec(), wspec(), wspec(),
            vec(),
        ],
        out_specs=rowspec(),
        out_shape=jax.ShapeDtypeStruct((n, D_MODEL), F32),
        compiler_params=pltpu.CompilerParams(
            dimension_semantics=("arbitrary",), vmem_limit_bytes=VMEM_LIMIT),
        name="out_proj",
    )(x2, gpre, w_m, lru_out, gdn_out, wl, wg, wo, gpost)


def _prep_weights(norm_pre, norm_post, w_in, lru_conv_w, lru_conv_b, lru_wa, lru_ba, lru_wx, lru_bx,
                  lru_a_logit, gdn_conv_w, gdn_A_log, gdn_dt_bias, gdn_norm_w, w_br_lru, w_br_gdn, w_out):
    c_gdn = LRU_Z_WIDTH
    c_ba = c_gdn + GDN_Z_WIDTH
    c_m = c_ba + 2 * GDN_HEADS
    w_ba = jnp.pad(w_in[:, c_ba:c_m], ((0, 0), (0, LANES - 2 * GDN_HEADS))).astype(BF16)
    wax = jnp.concatenate([lru_wa, lru_wx], axis=-1).astype(BF16)
    pad_alpha = lambda v: jnp.pad(v.reshape(1, GDN_HEADS), ((0, 0), (ALPHA_LANE, LANES - ALPHA_LANE - GDN_HEADS)))
    row = lambda v: v.reshape(1, -1)
    w_seq = w_in[:, :c_ba].astype(BF16)
    return dict(
        norm_pre=row(norm_pre), norm_post=row(norm_post),
        w_seq=w_seq, w_ba=w_ba,
        w_m=w_in[:, c_m:].astype(BF16),
        lru_cw=lru_conv_w, lru_cb=row(lru_conv_b), wax=wax, lru_ba=row(lru_ba), lru_bx=row(lru_bx),
        lru_alog=row(lru_a_logit), gdn_cw=gdn_conv_w, alog_pad=pad_alpha(gdn_A_log),
        dtb_pad=pad_alpha(gdn_dt_bias), gdn_nw=row(gdn_norm_w),
        wl=w_br_lru.astype(BF16), wg=w_br_gdn.astype(BF16), wo=w_out.astype(BF16))


def _layer(x, lru_conv, lru_h, gdn_conv, gdn_s, p, *, reset_first, fused, tm, lru_bt=None, lru_tt=None,
           gdn_bt=None, gdn_tt=None):
    nb, t, _ = x.shape
    x2 = x.reshape(nb * t, D_MODEL)
    h0 = lru_h.reshape(nb, 1, LRU_WIDTH)
    chunk = min(GDN_CHUNK, t)
    lru_w = (p["lru_cw"], p["lru_cb"], p["wax"], p["lru_ba"], p["lru_bx"], p["lru_alog"])
    gdn_w = (p["gdn_cw"], p["alog_pad"], p["dtb_pad"], p["gdn_nw"])
    if fused:
        lru_out, h_last, lru_conv_new, gdn_out, s_new, gdn_conv_new = _seq(
            x2, p["norm_pre"], p["w_seq"], p["w_ba"], lru_conv, h0, *lru_w, gdn_conv, gdn_s, *gdn_w,
            nb=nb, t=t, bt=1, tt=tm, chunk=chunk, reset_first=reset_first)
    else:
        lru_out, h_last, lru_conv_new = _lru(
            x2, p["norm_pre"], p["w_seq"], lru_conv, h0, *lru_w, nb=nb, t=t, bt=lru_bt, tt=lru_tt,
            reset_first=reset_first)
        z, ba = _in_proj(x2, p["norm_pre"], p["w_seq"], p["w_ba"], tm=nb * t, tn=SEG,
                         col0=LRU_Z_WIDTH // SEG, width=GDN_Z_WIDTH)
        gdn_out, s_new = _gdn_z(z, ba, gdn_conv, gdn_s, *gdn_w, nb=nb, t=t, bt=gdn_bt, tt=gdn_tt, chunk=chunk,
                                hg=GDN_HEADS)
        gdn_conv_new = z.reshape(nb, t, GDN_Z_WIDTH)[:, t - (CONV_W - 1):, :GDN_QKV_W]
    y = _out_proj(x2, p["norm_pre"], p["w_m"], lru_out, gdn_out, p["wl"], p["wg"], p["wo"], p["norm_post"],
                  tm=min(512, nb * t))
    return y.reshape(nb, t, D_MODEL), lru_conv_new, h_last.reshape(nb, LRU_WIDTH), gdn_conv_new, s_new


def kernel(x_prompt, x_sample, state_lru_conv, state_lru_h, state_gdn_conv, state_gdn_S, norm_pre, norm_post, w_in, lru_conv_w, lru_conv_b, lru_wa, lru_ba, lru_wx, lru_bx, lru_a_logit, gdn_conv_w, gdn_A_log, gdn_dt_bias, gdn_norm_w, w_br_lru, w_br_gdn, w_out):
    depth = w_in.shape[0]
    assert depth == 1
    nb = x_prompt.shape[0]
    p = _prep_weights(norm_pre[0], norm_post[0], w_in[0], lru_conv_w[0], lru_conv_b[0], lru_wa[0], lru_ba[0],
                      lru_wx[0], lru_bx[0], lru_a_logit[0], gdn_conv_w[0], gdn_A_log[0], gdn_dt_bias[0],
                      gdn_norm_w[0], w_br_lru[0], w_br_gdn[0], w_out[0])
    yp, p_lc, p_lh, p_gc, p_gs = _layer(
        x_prompt,
        jnp.zeros((nb, CONV_W - 1, LRU_WIDTH), F32), jnp.zeros((nb, LRU_WIDTH), F32),
        jnp.zeros((nb, CONV_W - 1, GDN_QKV_W), F32), jnp.zeros((nb, GDN_HEADS, GDN_DK, GDN_DV), F32),
        p, reset_first=True, fused=True, tm=256)
    ys, s_lc, s_lh, s_gc, s_gs = _layer(
        x_sample, state_lru_conv[0], state_lru_h[0], state_gdn_conv[0], state_gdn_S[0],
        p, reset_first=False, fused=False, tm=1024, lru_bt=32, lru_tt=8, gdn_bt=8, gdn_tt=8)
    return (yp, ys, p_lc[None], p_lh[None], p_gc[None], p_gs[None],
            s_lc[None], s_lh[None], s_gc[None], s_gs[None])
```

```python
import functools

import jax
import jax.numpy as jnp
from jax import lax
from jax.experimental import pallas as pl
from jax.experimental.pallas import tpu as pltpu

F32 = jnp.float32
BF16 = jnp.bfloat16

D_MODEL = 1024
CONV_W = 4
LRU_WIDTH = 1024
LRU_BLOCKS = 8
LRU_BLOCK = LRU_WIDTH // LRU_BLOCKS
LRU_C = 8.0
GDN_HEADS = 8
GDN_DK = 128
GDN_DV = 128
GDN_KEY_W = GDN_HEADS * GDN_DK
GDN_VAL_W = GDN_HEADS * GDN_DV
GDN_QKV_W = 2 * GDN_KEY_W + GDN_VAL_W
GDN_CHUNK = 64
NORM_EPS = 1e-6

LANES = 128
SUBLANES = 8
SEG = 1024
LRU_Z_WIDTH = 2 * SEG
GDN_Z_WIDTH = 4 * SEG
COL_Q, COL_K, COL_V, COL_GDN_GATE = range(4)
BETA_LANE = 0
ALPHA_LANE = GDN_HEADS
BA_COL_BLOCK = (LRU_Z_WIDTH + GDN_Z_WIDTH) // LANES
VMEM_LIMIT = 48 * 1024 * 1024
SEQ_VMEM_LIMIT = 56 * 1024 * 1024
GDN_SEQ_GROUP = 2
LRU_CARRY_PIECE = 8
FILL_EVERY = 4


def _softplus(x):
    return jnp.maximum(x, 0.0) + jnp.log1p(jnp.exp(-jnp.abs(x)))


def _sigmoid(x):
    return jax.nn.sigmoid(x)


def _silu(x):
    h = 0.5 * x
    return h + h * jnp.tanh(h)


def _fdot(a, b, dims=((1,), (0,))):
    return lax.dot_general(a, b, (dims, ((), ())), preferred_element_type=F32)


def _rms_norm_bf16(x, gain):
    ms = jnp.mean(x * x, axis=-1, keepdims=True)
    return ((x * lax.rsqrt(ms + NORM_EPS)) * gain).astype(BF16)


def _resident(shape, index_map):
    return pl.BlockSpec(shape, index_map, pipeline_mode=pl.Buffered(1))


def _in_proj_body(x_ref, gain_ref, w_ref, wba_ref, z_ref, ba_ref, u_scr):
    @pl.when(pl.program_id(1) == 0)
    def _():
        u = _rms_norm_bf16(x_ref[...], gain_ref[...])
        u_scr[...] = u
        ba_ref[...] = jnp.dot(u, wba_ref[...], preferred_element_type=F32)

    z_ref[...] = jnp.dot(u_scr[...], w_ref[...], preferred_element_type=F32)


def _in_proj(x2, gain, w, w_ba, *, tm, tn, col0, width):
    n = x2.shape[0]
    return pl.pallas_call(
        _in_proj_body,
        grid=(n // tm, width // tn),
        in_specs=[
            pl.BlockSpec((tm, D_MODEL), lambda i, j: (i, 0)),
            pl.BlockSpec((1, D_MODEL), lambda i, j: (0, 0)),
            pl.BlockSpec((D_MODEL, tn), lambda i, j: (0, col0 + j)),
            pl.BlockSpec((D_MODEL, LANES), lambda i, j: (0, BA_COL_BLOCK)),
        ],
        out_specs=[
            pl.BlockSpec((tm, tn), lambda i, j: (i, j)),
            pl.BlockSpec((tm, LANES), lambda i, j: (i, 0)),
        ],
        out_shape=[
            jax.ShapeDtypeStruct((n, width), F32),
            jax.ShapeDtypeStruct((n, LANES), F32),
        ],
        scratch_shapes=[pltpu.VMEM((tm, D_MODEL), BF16)],
        compiler_params=pltpu.CompilerParams(
            dimension_semantics=("arbitrary", "arbitrary"), vmem_limit_bytes=VMEM_LIMIT),
        name="in_proj",
    )(x2, gain, w, w_ba)


def _lru_stages(z_lru, gate, j, cw_ref, cb_ref, wax_ref, ba_ref, bx_ref, alog_ref, out_ref,
                xp_scr, a_scr, b_scr, h_scr, *, bt, tt, reset_first):
    rows = bt * tt
    width = LRU_WIDTH
    n_vreg_rows = rows // SUBLANES
    n_groups = tt // SUBLANES
    env = {}

    def conv():
        cur = z_lru().reshape(bt, tt, width)
        xp_scr[:, 8:8 + tt, :] = cur
        cw = cw_ref[...]
        xc = cur * cw[CONV_W - 1].reshape(1, 1, width)
        for i in range(CONV_W - 2, -1, -1):
            xc = xc + xp_scr[:, 5 + i:5 + i + tt, :] * cw[i].reshape(1, 1, width)
        xp_scr[:, 5:8, :] = xp_scr[:, 5 + tt:8 + tt, :]
        env["xc"] = (xc + cb_ref[...].reshape(1, 1, width)).reshape(rows, width)
        a_l = alog_ref[...]
        env["log_sig_a"] = jnp.minimum(a_l, 0.0) - jnp.log1p(jnp.exp(-jnp.abs(a_l)))
        env["sub3"] = lax.broadcasted_iota(jnp.int32, (n_vreg_rows, SUBLANES, LRU_BLOCK), 1)
        if reset_first:
            row = lax.broadcasted_iota(jnp.int32, (rows, LRU_BLOCK), 0)
            env["is_reset"] = jnp.logical_and(row % tt == 0, j == 0)

    def block(g):
        gs = slice(g * LRU_BLOCK, (g + 1) * LRU_BLOCK)
        xg = env["xc"][:, gs]
        pre = jnp.dot(xg.astype(BF16), wax_ref[g], preferred_element_type=F32)
        r = _sigmoid(pre[:, :LRU_BLOCK] + ba_ref[:, gs])
        ig = _sigmoid(pre[:, LRU_BLOCK:] + bx_ref[:, gs])
        log_a = (LRU_C * r) * env["log_sig_a"][:, gs]
        a = jnp.exp(log_a)
        t = jnp.tanh(log_a)
        m2 = (-2.0 * t) / (1.0 - t)
        mult = jnp.where(m2 > 0.0, m2 * lax.rsqrt(m2), 0.0)
        if reset_first:
            a = jnp.where(env["is_reset"], 0.0, a)
            mult = jnp.where(env["is_reset"], 1.0, mult)
        b = mult * ig * xg
        a = a.reshape(n_vreg_rows, SUBLANES, LRU_BLOCK)
        b = b.reshape(n_vreg_rows, SUBLANES, LRU_BLOCK)
        s = 1
        while s < SUBLANES:
            keep = env["sub3"] >= s
            a_sh = jnp.where(keep, pltpu.roll(a, s, axis=1), 1.0)
            b_sh = jnp.where(keep, pltpu.roll(b, s, axis=1), 0.0)
            b = a * b_sh + b
            a = a * a_sh
            s *= 2
        a_scr[:, :, gs] = a.reshape(bt, tt, LRU_BLOCK)
        b_scr[:, :, gs] = b.reshape(bt, tt, LRU_BLOCK)

    def carry_groups(g0, g1):
        carry = env.get("carry")
        if carry is None:
            carry = h_scr[...]
        for i in range(g0, g1):
            rs = slice(i * SUBLANES, (i + 1) * SUBLANES)
            h = b_scr[:, rs, :] + a_scr[:, rs, :] * carry
            b_scr[:, rs, :] = h
            carry = h[:, SUBLANES - 1:SUBLANES, :]
        env["carry"] = carry
        if g1 == n_groups:
            h_scr[...] = carry

    def finish():
        h = b_scr[...].reshape(rows, width)
        out_ref[...] = (h * _silu(gate())).astype(out_ref.dtype)

    per_piece = min(n_groups, LRU_CARRY_PIECE)
    pieces = [functools.partial(carry_groups, g0, min(g0 + per_piece, n_groups))
              for g0 in range(0, n_groups, per_piece)]
    return [conv] + [functools.partial(block, g) for g in range(LRU_BLOCKS)] + pieces + [finish]


def _lru_body(x_ref, gain_ref, w_ref, cs_ref, h0_ref, cw_ref, cb_ref, wax_ref, ba_ref, bx_ref, alog_ref,
              out_ref, hout_ref, csout_ref, xp_scr, a_scr, b_scr, h_scr, *, bt, tt, reset_first):
    j = pl.program_id(1)

    @pl.when(j == 0)
    def _():
        xp_scr[:, 5:8, :] = cs_ref[...]
        h_scr[...] = h0_ref[...]

    u = _rms_norm_bf16(x_ref[...], gain_ref[...])
    z = jnp.dot(u, w_ref[...], preferred_element_type=F32)
    for stage in _lru_stages(lambda: z[:, :LRU_WIDTH], lambda: z[:, LRU_WIDTH:], j,
                             cw_ref, cb_ref, wax_ref, ba_ref, bx_ref, alog_ref, out_ref, xp_scr, a_scr, b_scr, h_scr, bt=bt, tt=tt,
                             reset_first=reset_first):
        stage()

    @pl.when(j == pl.num_programs(1) - 1)
    def _():
        hout_ref[...] = h_scr[...]
        csout_ref[...] = xp_scr[:, 5:8, :]


def _lru_scratch(bt, tt):
    return [
        pltpu.VMEM((bt, SUBLANES + tt, LRU_WIDTH), F32),
        pltpu.VMEM((bt, tt, LRU_WIDTH), F32),
        pltpu.VMEM((bt, tt, LRU_WIDTH), F32),
        pltpu.VMEM((bt, 1, LRU_WIDTH), F32),
    ]


def _lru(x2, gain, w_lru, conv_state, h0, cw, cb, wax, ba, bx, alog, *, nb, t, bt, tt, reset_first):
    nt = t // tt
    rows = bt * tt
    body = functools.partial(_lru_body, bt=bt, tt=tt, reset_first=reset_first)
    vec = lambda: pl.BlockSpec((1, LRU_WIDTH), lambda i, j: (0, 0))
    return pl.pallas_call(
        body,
        grid=(nb // bt, nt),
        in_specs=[
            pl.BlockSpec((rows, D_MODEL), lambda i, j: (i * nt + j, 0)),
            vec(),
            pl.BlockSpec((D_MODEL, LRU_Z_WIDTH), lambda i, j: (0, 0)),
            pl.BlockSpec((bt, CONV_W - 1, LRU_WIDTH), lambda i, j: (i, 0, 0)),
            pl.BlockSpec((bt, 1, LRU_WIDTH), lambda i, j: (i, 0, 0)),
            pl.BlockSpec((CONV_W, LRU_WIDTH), lambda i, j: (0, 0)),
            vec(),
            pl.BlockSpec((LRU_BLOCKS, LRU_BLOCK, 2 * LRU_BLOCK), lambda i, j: (0, 0, 0)),
            vec(), vec(), vec(),
        ],
        out_specs=[
            pl.BlockSpec((rows, LRU_WIDTH), lambda i, j: (i * nt + j, 0)),
            pl.BlockSpec((bt, 1, LRU_WIDTH), lambda i, j: (i, 0, 0)),
            pl.BlockSpec((bt, CONV_W - 1, LRU_WIDTH), lambda i, j: (i, 0, 0)),
        ],
        out_shape=[
            jax.ShapeDtypeStruct((nb * t, LRU_WIDTH), BF16),
            jax.ShapeDtypeStruct((nb, 1, LRU_WIDTH), F32),
            jax.ShapeDtypeStruct((nb, CONV_W - 1, LRU_WIDTH), F32),
        ],
        scratch_shapes=_lru_scratch(bt, tt),
        compiler_params=pltpu.CompilerParams(
            dimension_semantics=("arbitrary", "arbitrary"), vmem_limit_bytes=VMEM_LIMIT),
        name="lru",
    )(x2, gain, w_lru, conv_state, h0, cw, cb, wax, ba, bx, alog)


def _gdn_core(zq, zk, zv, zgate, ba, cw_refs, alog_ref, dtb_ref, nw_ref, out_ref, xp_scr, s_scr,
              h_grp, *, bt, tt, chunk, hg, seq_group, fill=lambda: None):
    rows = bt * tt
    width = LANES * hg
    n_chunks = tt // chunk

    def conv_silu(idx, cur2, cw_ref):
        cur = cur2.reshape(bt, tt, width)
        xp_scr[idx, :, 8:8 + tt, :] = cur
        cw = cw_ref[...]
        y = cur * cw[CONV_W - 1].reshape(1, 1, width)
        for i in range(CONV_W - 2, -1, -1):
            y = y + xp_scr[idx, :, 5 + i:5 + i + tt, :] * cw[i].reshape(1, 1, width)
        xp_scr[idx, :, 5:8, :] = xp_scr[idx, :, 5 + tt:8 + tt, :]
        return _silu(y.reshape(rows, width))

    q_all = conv_silu(0, zq(), cw_refs[0])
    k_all = conv_silu(1, zk(), cw_refs[1])
    v_all = conv_silu(2, zv(), cw_refs[2])
    ba = ba()

    beta_all = _sigmoid(ba)
    g_all = -jnp.exp(alog_ref[...]) * _softplus(ba + dtb_ref[...])
    row = lax.broadcasted_iota(jnp.int32, (rows, LANES), 0)
    row_in_chunk = row % chunk
    gc_all = g_all
    s = 1
    while s < chunk:
        gc_all = gc_all + jnp.where(row_in_chunk >= s, pltpu.roll(gc_all, s, axis=0), 0.0)
        s *= 2
    gct_all = gc_all.T

    lane = lax.broadcasted_iota(jnp.int32, (rows, LANES), 1)
    sub = lax.broadcasted_iota(jnp.int32, (LANES, rows), 0)
    ri = lax.broadcasted_iota(jnp.int32, (rows, rows), 0)
    ci = lax.broadcasted_iota(jnp.int32, (rows, rows), 1)
    same_chunk = (ri // chunk) == (ci // chunk)
    causal = jnp.logical_and(same_chunk, ri >= ci)
    strict = jnp.logical_and(same_chunk, ri > ci)
    n_steps = (chunk - 1).bit_length()

    hd = []
    for hh in range(hg):
        head = h_grp * hg + hh
        ls = slice(hh * LANES, (hh + 1) * LANES)
        qh, kh, vh = q_all[:, ls], k_all[:, ls], v_all[:, ls]
        qh = qh * lax.rsqrt(jnp.sum(qh * qh, axis=-1, keepdims=True) + NORM_EPS) * (GDN_DK ** -0.5)
        kh = kh * lax.rsqrt(jnp.sum(kh * kh, axis=-1, keepdims=True) + NORM_EPS)

        beta = jnp.sum(jnp.where(lane == BETA_LANE + head, beta_all, 0.0), axis=1, keepdims=True)
        gc = jnp.sum(jnp.where(lane == ALPHA_LANE + head, gc_all, 0.0), axis=1, keepdims=True)
        gc_row = jnp.sum(jnp.where(sub == ALPHA_LANE + head, gct_all, 0.0), axis=0, keepdims=True)

        decay = jnp.where(causal, jnp.exp(jnp.where(causal, gc - gc_row, 0.0)), 0.0)
        kb = kh * beta
        gram = _fdot(jnp.concatenate([kb, qh], axis=0), kh, ((1,), (1,)))
        e_gc = jnp.exp(gc)
        hd.append(dict(
            ls=ls, kh=kh, gc=gc,
            p=jnp.where(strict, -(gram[:rows] * decay), 0.0),
            a_qk=gram[rows:] * decay,
            x=jnp.concatenate([vh * beta, kb * e_gc], axis=1),
            q_dec=qh * e_gc))

    n_dots = 0
    for st in range(n_steps):
        for d in hd:
            if st == 0:
                d["n"] = d["p"]
                if n_steps > 1:
                    d["p"] = _fdot(d["p"], d["p"])
            elif st < n_steps - 1:
                res = _fdot(jnp.concatenate([d["n"], d["p"]], axis=0), d["p"])
                d["n"] = d["n"] + d["p"] + res[:rows]
                d["p"] = res[rows:]
            else:
                d["n"] = d["n"] + d["p"] + _fdot(d["n"], d["p"])
            n_dots += 1
            if n_dots % FILL_EVERY == 0:
                fill()
    for d in hd:
        d["x"] = d["x"] + _fdot(d["n"], d["x"])

    for d in hd:
        d["v_new"] = {}
        d["qs"] = {}
    for sq0 in range(0, bt, seq_group):
        pairs = [(sq, hh) for sq in range(sq0, min(sq0 + seq_group, bt)) for hh in range(hg)]
        state = {pr: s_scr[pr[0], pr[1]] for pr in pairs}
        for c in range(n_chunks):
            rsl = {sq: slice(sq * tt + c * chunk, sq * tt + (c + 1) * chunk) for sq, _ in pairs}
            ws, v_new, k_dec, e_last, upd = {}, {}, {}, {}, {}
            for pr in pairs:
                d, rs = hd[pr[1]], rsl[pr[0]]
                ws[pr] = _fdot(jnp.concatenate([d["x"][rs, LANES:], d["q_dec"][rs]], axis=0), state[pr])
            for pr in pairs:
                d, rs = hd[pr[1]], rsl[pr[0]]
                v_new[pr] = d["x"][rs, :LANES] - ws[pr][:chunk]
                g_last = d["gc"][rs.stop - 1:rs.stop]
                k_dec[pr] = d["kh"][rs] * jnp.exp(g_last - d["gc"][rs])
                e_last[pr] = jnp.exp(g_last)
                d["v_new"][(pr[0], c)] = v_new[pr]
                d["qs"][(pr[0], c)] = ws[pr][chunk:]
            for pr in pairs:
                upd[pr] = _fdot(k_dec[pr], v_new[pr], ((0,), (0,)))
            for pr in pairs:
                state[pr] = state[pr] * e_last[pr] + upd[pr]
            fill()
        for pr in pairs:
            s_scr[pr[0], pr[1]] = state[pr]

    order = [(sq, c) for sq in range(bt) for c in range(n_chunks)]
    gate = zgate()
    for d in hd:
        v_parts = [d["v_new"][k] for k in order]
        q_parts = [d["qs"][k] for k in order]
        v_new_all = jnp.concatenate(v_parts, axis=0) if len(v_parts) > 1 else v_parts[0]
        qs_all = jnp.concatenate(q_parts, axis=0) if len(q_parts) > 1 else q_parts[0]
        o = qs_all + _fdot(d["a_qk"], v_new_all)
        o = o * lax.rsqrt(jnp.mean(o * o, axis=-1, keepdims=True) + NORM_EPS) * nw_ref[...]
        o = o * _silu(gate[:, d["ls"]])
        out_ref[:, d["ls"]] = o.astype(out_ref.dtype)


def _gdn_init_state(j, cs_refs, s0_ref, xp_scr, s_scr):
    @pl.when(j == 0)
    def _():
        for idx, cs_ref in enumerate(cs_refs):
            xp_scr[idx, :, 5:8, :] = cs_ref[...]
        s_scr[...] = s0_ref[...]


def _gdn_z_body(zq_ref, zk_ref, zv_ref, zg_ref, ba_ref, csq_ref, csk_ref, csv_ref, s0_ref,
                cwq_ref, cwk_ref, cwv_ref, alog_ref, dtb_ref, nw_ref,
                out_ref, sout_ref, xp_scr, s_scr, *, bt, tt, chunk, hg):
    j = pl.program_id(2)
    _gdn_init_state(j, (csq_ref, csk_ref, csv_ref), s0_ref, xp_scr, s_scr)
    _gdn_core(lambda: zq_ref[...], lambda: zk_ref[...], lambda: zv_ref[...], lambda: zg_ref[...],
              lambda: ba_ref[...],
              (cwq_ref, cwk_ref, cwv_ref), alog_ref, dtb_ref, nw_ref, out_ref, xp_scr, s_scr,
              pl.program_id(1), bt=bt, tt=tt, chunk=chunk, hg=hg, seq_group=GDN_SEQ_GROUP)

    @pl.when(j == pl.num_programs(2) - 1)
    def _():
        sout_ref[...] = s_scr[...]


def _gdn_z(z, ba, conv_state, s0, cw, alog_pad, dtb_pad, nw, *, nb, t, bt, tt, chunk, hg):
    nt = t // tt
    rows = bt * tt
    width = LANES * hg
    per = SEG // width
    body = functools.partial(_gdn_z_body, bt=bt, tt=tt, chunk=chunk, hg=hg)

    def zspec(col):
        return pl.BlockSpec((rows, width), lambda i, h, j: (i * nt + j, col * per + h))

    def cs_spec(seg):
        return pl.BlockSpec((bt, CONV_W - 1, width), lambda i, h, j: (i, 0, seg * per + h))

    def cw_spec(seg):
        return pl.BlockSpec((CONV_W, width), lambda i, h, j: (0, seg * per + h))

    vec = lambda: pl.BlockSpec((1, LANES), lambda i, h, j: (0, 0))
    return pl.pallas_call(
        body,
        grid=(nb // bt, GDN_HEADS // hg, nt),
        in_specs=[
            zspec(COL_Q), zspec(COL_K), zspec(COL_V), zspec(COL_GDN_GATE),
            pl.BlockSpec((rows, LANES), lambda i, h, j: (i * nt + j, 0)),
            cs_spec(0), cs_spec(1), cs_spec(2),
            pl.BlockSpec((bt, hg, GDN_DK, GDN_DV), lambda i, h, j: (i, h, 0, 0)),
            cw_spec(0), cw_spec(1), cw_spec(2),
            vec(), vec(), vec(),
        ],
        out_specs=[
            pl.BlockSpec((rows, width), lambda i, h, j: (i * nt + j, h)),
            pl.BlockSpec((bt, hg, GDN_DK, GDN_DV), lambda i, h, j: (i, h, 0, 0)),
        ],
        out_shape=[
            jax.ShapeDtypeStruct((nb * t, GDN_VAL_W), BF16),
            jax.ShapeDtypeStruct((nb, GDN_HEADS, GDN_DK, GDN_DV), F32),
        ],
        scratch_shapes=[
            pltpu.VMEM((3, bt, SUBLANES + tt, width), F32),
            pltpu.VMEM((bt, hg, GDN_DK, GDN_DV), F32),
        ],
        compiler_params=pltpu.CompilerParams(
            dimension_semantics=("arbitrary", "arbitrary", "arbitrary"), vmem_limit_bytes=VMEM_LIMIT),
        name="gdn_z",
    )(z, z, z, z, ba, conv_state, conv_state, conv_state, s0, cw, cw, cw, alog_pad, dtb_pad, nw)


def _seq_body(x_ref, gain_ref, w_ref, wba_ref,
              lcs_ref, h0_ref, lcw_ref, lcb_ref, wax_ref, lba_ref, lbx_ref, lalog_ref,
              csq_ref, csk_ref, csv_ref, s0_ref, cwq_ref, cwk_ref, cwv_ref, galog_ref, dtb_ref, nw_ref,
              lru_out_ref, hout_ref, lcsout_ref, gdn_out_ref, sout_ref, gcsout_ref,
              lxp_scr, a_scr, b_scr, h_scr, gxp_scr, s_scr, *, bt, tt, chunk, reset_first):
    j = pl.program_id(1)

    @pl.when(j == 0)
    def _():
        lxp_scr[:, 5:8, :] = lcs_ref[...]
        h_scr[...] = h0_ref[...]

    _gdn_init_state(j, (csq_ref, csk_ref, csv_ref), s0_ref, gxp_scr, s_scr)

    u = _rms_norm_bf16(x_ref[...], gain_ref[...])

    def seg(c):
        return lambda: jnp.dot(u, w_ref[:, c * SEG:(c + 1) * SEG], preferred_element_type=F32)

    ba = lambda: jnp.dot(u, wba_ref[...], preferred_element_type=F32)

    stages = iter(_lru_stages(seg(0), seg(1), j, lcw_ref, lcb_ref, wax_ref, lba_ref, lbx_ref, lalog_ref,
                              lru_out_ref, lxp_scr, a_scr, b_scr, h_scr, bt=bt, tt=tt,
                              reset_first=reset_first))

    def fill():
        stage = next(stages, None)
        if stage is not None:
            stage()

    _gdn_core(seg(2 + COL_Q), seg(2 + COL_K), seg(2 + COL_V), seg(2 + COL_GDN_GATE), ba,
              (cwq_ref, cwk_ref, cwv_ref), galog_ref, dtb_ref, nw_ref, gdn_out_ref, gxp_scr, s_scr,
              0, bt=bt, tt=tt, chunk=chunk, hg=GDN_HEADS, seq_group=1, fill=fill)
    for stage in stages:
        stage()

    @pl.when(j == pl.num_programs(1) - 1)
    def _():
        hout_ref[...] = h_scr[...]
        lcsout_ref[...] = lxp_scr[:, 5:8, :]
        sout_ref[...] = s_scr[...]
        for idx in range(3):
            gcsout_ref[:, :, idx * SEG:(idx + 1) * SEG] = gxp_scr[idx, :, 5:8, :]


def _seq(x2, gain, w_seq, w_ba, lru_conv, h0, lcw, lcb, wax, lba, lbx, lalog,
         gdn_conv, s0, gcw, galog_pad, dtb_pad, nw, *, nb, t, bt, tt, chunk, reset_first):
    nt = t // tt
    rows = bt * tt
    hg = GDN_HEADS
    body = functools.partial(_seq_body, bt=bt, tt=tt, chunk=chunk, reset_first=reset_first)
    const2 = lambda i, j: (0, 0)
    vec = lambda n: pl.BlockSpec((1, n), const2)
    cs_spec = lambda seg: pl.BlockSpec((bt, CONV_W - 1, SEG), lambda i, j: (i, 0, seg))
    cw_spec = lambda seg: pl.BlockSpec((CONV_W, SEG), lambda i, j: (0, seg))
    state_spec = pl.BlockSpec((bt, hg, GDN_DK, GDN_DV), lambda i, j: (i, 0, 0, 0))
    return pl.pallas_call(
        body,
        grid=(nb // bt, nt),
        in_specs=[
            pl.BlockSpec((rows, D_MODEL), lambda i, j: (i * nt + j, 0)),
            vec(D_MODEL),
            _resident((D_MODEL, LRU_Z_WIDTH + GDN_Z_WIDTH), const2),
            _resident((D_MODEL, LANES), lambda i, j: (0, BA_COL_BLOCK)),
            pl.BlockSpec((bt, CONV_W - 1, LRU_WIDTH), lambda i, j: (i, 0, 0)),
            pl.BlockSpec((bt, 1, LRU_WIDTH), lambda i, j: (i, 0, 0)),
            pl.BlockSpec((CONV_W, LRU_WIDTH), const2),
            vec(LRU_WIDTH),
            pl.BlockSpec((LRU_BLOCKS, LRU_BLOCK, 2 * LRU_BLOCK), lambda i, j: (0, 0, 0)),
            vec(LRU_WIDTH), vec(LRU_WIDTH), vec(LRU_WIDTH),
            cs_spec(0), cs_spec(1), cs_spec(2),
            state_spec,
            cw_spec(0), cw_spec(1), cw_spec(2),
            vec(LANES), vec(LANES), vec(LANES),
        ],
        out_specs=[
            pl.BlockSpec((rows, LRU_WIDTH), lambda i, j: (i * nt + j, 0)),
            pl.BlockSpec((bt, 1, LRU_WIDTH), lambda i, j: (i, 0, 0)),
            pl.BlockSpec((bt, CONV_W - 1, LRU_WIDTH), lambda i, j: (i, 0, 0)),
            pl.BlockSpec((rows, GDN_VAL_W), lambda i, j: (i * nt + j, 0)),
            state_spec,
            pl.BlockSpec((bt, CONV_W - 1, GDN_QKV_W), lambda i, j: (i, 0, 0)),
        ],
        out_shape=[
            jax.ShapeDtypeStruct((nb * t, LRU_WIDTH), BF16),
            jax.ShapeDtypeStruct((nb, 1, LRU_WIDTH), F32),
            jax.ShapeDtypeStruct((nb, CONV_W - 1, LRU_WIDTH), F32),
            jax.ShapeDtypeStruct((nb * t, GDN_VAL_W), BF16),
            jax.ShapeDtypeStruct((nb, GDN_HEADS, GDN_DK, GDN_DV), F32),
            jax.ShapeDtypeStruct((nb, CONV_W - 1, GDN_QKV_W), F32),
        ],
        scratch_shapes=_lru_scratch(bt, tt) + [
            pltpu.VMEM((3, bt, SUBLANES + tt, SEG), F32),
            pltpu.VMEM((bt, hg, GDN_DK, GDN_DV), F32),
        ],
        compiler_params=pltpu.CompilerParams(
            dimension_semantics=("arbitrary", "arbitrary"), vmem_limit_bytes=SEQ_VMEM_LIMIT),
        name="seq",
    )(x2, gain, w_seq, w_ba, lru_conv, h0, lcw, lcb, wax, lba, lbx, lalog,
      gdn_conv, gdn_conv, gdn_conv, s0, gcw, gcw, gcw, galog_pad, dtb_pad, nw)


def _out_body(x_ref, gpre_ref, wm_ref, lru_ref, gdn_ref, wl_ref, wg_ref, wo_ref, gpost_ref, y_ref):
    x = x_ref[...]
    u = _rms_norm_bf16(x, gpre_ref[...])
    m = jnp.dot(u, wm_ref[...], preferred_element_type=F32)
    p_lru = jnp.dot(lru_ref[...], wl_ref[...], preferred_element_type=F32)
    p_gdn = jnp.dot(gdn_ref[...], wg_ref[...], preferred_element_type=F32)
    merged = _sigmoid(m[:, :D_MODEL]) * p_lru + _sigmoid(m[:, D_MODEL:]) * p_gdn
    y = jnp.dot(merged.astype(BF16), wo_ref[...], preferred_element_type=F32)
    ms = jnp.mean(y * y, axis=-1, keepdims=True)
    y_ref[...] = x + (y * lax.rsqrt(ms + NORM_EPS)) * gpost_ref[...]


def _out_proj(x2, gpre, w_m, lru_out, gdn_out, wl, wg, wo, gpost, *, tm):
    n = x2.shape[0]
    wspec = lambda: pl.BlockSpec((D_MODEL, D_MODEL), lambda i: (0, 0))
    vec = lambda: pl.BlockSpec((1, D_MODEL), lambda i: (0, 0))
    rowspec = lambda: pl.BlockSpec((tm, D_MODEL), lambda i: (i, 0))
    return pl.pallas_call(
        _out_body,
        grid=(n // tm,),
        in_specs=[
            rowspec(), vec(),
            pl.BlockSpec((D_MODEL, 2 * D_MODEL), lambda i: (0, 0)),
            rowspec(), rowspec(),
            wspec(), wspec(), wspec(),
            vec(),
        ],
        out_specs=rowspec(),
        out_shape=jax.ShapeDtypeStruct((n, D_MODEL), F32),
        compiler_params=pltpu.CompilerParams(
            dimension_semantics=("arbitrary",), vmem_limit_bytes=VMEM_LIMIT),
        name="out_proj",
    )(x2, gpre, w_m, lru_out, gdn_out, wl, wg, wo, gpost)


def _prep_weights(norm_pre, norm_post, w_in, lru_conv_w, lru_conv_b, lru_wa, lru_ba, lru_wx, lru_bx,
                  lru_a_logit, gdn_conv_w, gdn_A_log, gdn_dt_bias, gdn_norm_w, w_br_lru, w_br_gdn, w_out):
    c_m = BA_COL_BLOCK * LANES + 2 * GDN_HEADS
    wax = jnp.concatenate([lru_wa, lru_wx], axis=-1).astype(BF16)
    pad_alpha = lambda v: jnp.pad(v.reshape(1, GDN_HEADS), ((0, 0), (ALPHA_LANE, LANES - ALPHA_LANE - GDN_HEADS)))
    row = lambda v: v.reshape(1, -1)
    w_all = w_in.astype(BF16)
    return dict(
        norm_pre=row(norm_pre), norm_post=row(norm_post),
        w_all=w_all, w_m=w_all[:, c_m:],
        lru_cw=lru_conv_w, lru_cb=row(lru_conv_b), wax=wax, lru_ba=row(lru_ba), lru_bx=row(lru_bx),
        lru_alog=row(lru_a_logit), gdn_cw=gdn_conv_w, alog_pad=pad_alpha(gdn_A_log),
        dtb_pad=pad_alpha(gdn_dt_bias), gdn_nw=row(gdn_norm_w),
        wl=w_br_lru.astype(BF16), wg=w_br_gdn.astype(BF16), wo=w_out.astype(BF16))


def _layer(x, lru_conv, lru_h, gdn_conv, gdn_s, p, *, reset_first, fused, tm, lru_bt=None, lru_tt=None,
           gdn_bt=None, gdn_tt=None):
    nb, t, _ = x.shape
    x2 = x.reshape(nb * t, D_MODEL)
    h0 = lru_h.reshape(nb, 1, LRU_WIDTH)
    chunk = min(GDN_CHUNK, t)
    lru_w = (p["lru_cw"], p["lru_cb"], p["wax"], p["lru_ba"], p["lru_bx"], p["lru_alog"])
    gdn_w = (p["gdn_cw"], p["alog_pad"], p["dtb_pad"], p["gdn_nw"])
    if fused:
        lru_out, h_last, lru_conv_new, gdn_out, s_new, gdn_conv_new = _seq(
            x2, p["norm_pre"], p["w_all"], p["w_all"], lru_conv, h0, *lru_w, gdn_conv, gdn_s, *gdn_w,
            nb=nb, t=t, bt=1, tt=tm, chunk=chunk, reset_first=reset_first)
    else:
        lru_out, h_last, lru_conv_new = _lru(
            x2, p["norm_pre"], p["w_all"], lru_conv, h0, *lru_w, nb=nb, t=t, bt=lru_bt, tt=lru_tt,
            reset_first=reset_first)
        z, ba = _in_proj(x2, p["norm_pre"], p["w_all"], p["w_all"], tm=nb * t, tn=SEG,
                         col0=LRU_Z_WIDTH // SEG, width=GDN_Z_WIDTH)
        gdn_out, s_new = _gdn_z(z, ba, gdn_conv, gdn_s, *gdn_w, nb=nb, t=t, bt=gdn_bt, tt=gdn_tt, chunk=chunk,
                                hg=GDN_HEADS)
        gdn_conv_new = z.reshape(nb, t, GDN_Z_WIDTH)[:, t - (CONV_W - 1):, :GDN_QKV_W]
    y = _out_proj(x2, p["norm_pre"], p["w_m"], lru_out, gdn_out, p["wl"], p["wg"], p["wo"], p["norm_post"],
                  tm=min(512, nb * t))
    return y.reshape(nb, t, D_MODEL), lru_conv_new, h_last.reshape(nb, LRU_WIDTH), gdn_conv_new, s_new


def kernel(x_prompt, x_sample, state_lru_conv, state_lru_h, state_gdn_conv, state_gdn_S, norm_pre, norm_post, w_in, lru_conv_w, lru_conv_b, lru_wa, lru_ba, lru_wx, lru_bx, lru_a_logit, gdn_conv_w, gdn_A_log, gdn_dt_bias, gdn_norm_w, w_br_lru, w_br_gdn, w_out):
    depth = w_in.shape[0]
    assert depth == 1
    nb = x_prompt.shape[0]
    p = _prep_weights(norm_pre[0], norm_post[0], w_in[0], lru_conv_w[0], lru_conv_b[0], lru_wa[0], lru_ba[0],
                      lru_wx[0], lru_bx[0], lru_a_logit[0], gdn_conv_w[0], gdn_A_log[0], gdn_dt_bias[0],
                      gdn_norm_w[0], w_br_lru[0], w_br_gdn[0], w_out[0])
    yp, p_lc, p_lh, p_gc, p_gs = _layer(
        x_prompt,
        jnp.zeros((nb, CONV_W - 1, LRU_WIDTH), F32), jnp.zeros((nb, LRU_WIDTH), F32),
        jnp.zeros((nb, CONV_W - 1, GDN_QKV_W), F32), jnp.zeros((nb, GDN_HEADS, GDN_DK, GDN_DV), F32),
        p, reset_first=True, fused=True, tm=256)
    ys, s_lc, s_lh, s_gc, s_gs = _layer(
        x_sample, state_lru_conv[0], state_lru_h[0], state_gdn_conv[0], state_gdn_S[0],
        p, reset_first=False, fused=False, tm=1024, lru_bt=32, lru_tt=8, gdn_bt=8, gdn_tt=8)
    return (yp, ys, p_lc[None], p_lh[None], p_gc[None], p_gs[None],
            s_lc[None], s_lh[None], s_gc[None], s_gs[None])
```

```python
import functools

import jax
import jax.numpy as jnp
from jax import lax
from jax.experimental import pallas as pl
from jax.experimental.pallas import tpu as pltpu

F32 = jnp.float32
BF16 = jnp.bfloat16

D_MODEL = 1024
CONV_W = 4
LRU_WIDTH = 1024
LRU_BLOCKS = 8
LRU_BLOCK = LRU_WIDTH // LRU_BLOCKS
LRU_C = 8.0
GDN_HEADS = 8
GDN_DK = 128
GDN_DV = 128
GDN_KEY_W = GDN_HEADS * GDN_DK
GDN_VAL_W = GDN_HEADS * GDN_DV
GDN_QKV_W = 2 * GDN_KEY_W + GDN_VAL_W
GDN_CHUNK = 64
NORM_EPS = 1e-6

LANES = 128
SUBLANES = 8
SEG = 1024
LRU_Z_WIDTH = 2 * SEG
GDN_Z_WIDTH = 4 * SEG
COL_Q, COL_K, COL_V, COL_GDN_GATE = range(4)
BETA_LANE = 0
ALPHA_LANE = GDN_HEADS
BA_COL_BLOCK = (LRU_Z_WIDTH + GDN_Z_WIDTH) // LANES
VMEM_LIMIT = 48 * 1024 * 1024
SEQ_VMEM_LIMIT = 56 * 1024 * 1024
GDN_SEQ_GROUP = 2
LRU_CARRY_PIECE = 8
HEAD_GROUP = 8
FILL_EVERY = 6


def _softplus(x):
    return jnp.maximum(x, 0.0) + jnp.log1p(jnp.exp(-jnp.abs(x)))


def _sigmoid(x):
    return jax.nn.sigmoid(x)


def _silu(x):
    h = 0.5 * x
    return h + h * jnp.tanh(h)


def _fdot(a, b, dims=((1,), (0,))):
    return lax.dot_general(a, b, (dims, ((), ())), preferred_element_type=F32)


def _rms_norm_bf16(x, gain):
    ms = jnp.mean(x * x, axis=-1, keepdims=True)
    return ((x * lax.rsqrt(ms + NORM_EPS)) * gain).astype(BF16)


def _resident(shape, index_map):
    return pl.BlockSpec(shape, index_map, pipeline_mode=pl.Buffered(1))


def _in_proj_body(x_ref, gain_ref, w_ref, wba_ref, z_ref, ba_ref, u_scr):
    @pl.when(pl.program_id(1) == 0)
    def _():
        u = _rms_norm_bf16(x_ref[...], gain_ref[...])
        u_scr[...] = u
        ba_ref[...] = jnp.dot(u, wba_ref[...], preferred_element_type=F32)

    z_ref[...] = jnp.dot(u_scr[...], w_ref[...], preferred_element_type=F32)


def _in_proj(x2, gain, w, w_ba, *, tm, tn, col0, width):
    n = x2.shape[0]
    return pl.pallas_call(
        _in_proj_body,
        grid=(n // tm, width // tn),
        in_specs=[
            pl.BlockSpec((tm, D_MODEL), lambda i, j: (i, 0)),
            pl.BlockSpec((1, D_MODEL), lambda i, j: (0, 0)),
            pl.BlockSpec((D_MODEL, tn), lambda i, j: (0, col0 + j)),
            pl.BlockSpec((D_MODEL, LANES), lambda i, j: (0, BA_COL_BLOCK)),
        ],
        out_specs=[
            pl.BlockSpec((tm, tn), lambda i, j: (i, j)),
            pl.BlockSpec((tm, LANES), lambda i, j: (i, 0)),
        ],
        out_shape=[
            jax.ShapeDtypeStruct((n, width), F32),
            jax.ShapeDtypeStruct((n, LANES), F32),
        ],
        scratch_shapes=[pltpu.VMEM((tm, D_MODEL), BF16)],
        compiler_params=pltpu.CompilerParams(
            dimension_semantics=("arbitrary", "arbitrary"), vmem_limit_bytes=VMEM_LIMIT),
        name="in_proj",
    )(x2, gain, w, w_ba)


def _lru_stages(z_lru, gate, j, cw_ref, cb_ref, wax_ref, ba_ref, bx_ref, alog_ref, out_ref,
                xp_scr, a_scr, b_scr, h_scr, *, bt, tt, reset_first):
    rows = bt * tt
    width = LRU_WIDTH
    n_vreg_rows = rows // SUBLANES
    n_groups = tt // SUBLANES
    env = {}

    def conv():
        cur = z_lru().reshape(bt, tt, width)
        xp_scr[:, 8:8 + tt, :] = cur
        cw = cw_ref[...]
        xc = cur * cw[CONV_W - 1].reshape(1, 1, width)
        for i in range(CONV_W - 2, -1, -1):
            xc = xc + xp_scr[:, 5 + i:5 + i + tt, :] * cw[i].reshape(1, 1, width)
        xp_scr[:, 5:8, :] = xp_scr[:, 5 + tt:8 + tt, :]
        env["xc"] = (xc + cb_ref[...].reshape(1, 1, width)).reshape(rows, width)
        a_l = alog_ref[...]
        env["log_sig_a"] = jnp.minimum(a_l, 0.0) - jnp.log1p(jnp.exp(-jnp.abs(a_l)))
        env["sub3"] = lax.broadcasted_iota(jnp.int32, (n_vreg_rows, SUBLANES, LRU_BLOCK), 1)
        if reset_first:
            row = lax.broadcasted_iota(jnp.int32, (rows, LRU_BLOCK), 0)
            env["is_reset"] = jnp.logical_and(row % tt == 0, j == 0)

    def block(g):
        gs = slice(g * LRU_BLOCK, (g + 1) * LRU_BLOCK)
        xg = env["xc"][:, gs]
        pre = jnp.dot(xg.astype(BF16), wax_ref[g], preferred_element_type=F32)
        r = _sigmoid(pre[:, :LRU_BLOCK] + ba_ref[:, gs])
        ig = _sigmoid(pre[:, LRU_BLOCK:] + bx_ref[:, gs])
        log_a = (LRU_C * r) * env["log_sig_a"][:, gs]
        a = jnp.exp(log_a)
        t = jnp.tanh(log_a)
        m2 = (-2.0 * t) / (1.0 - t)
        mult = jnp.where(m2 > 0.0, m2 * lax.rsqrt(m2), 0.0)
        if reset_first:
            a = jnp.where(env["is_reset"], 0.0, a)
            mult = jnp.where(env["is_reset"], 1.0, mult)
        b = mult * ig * xg
        a = a.reshape(n_vreg_rows, SUBLANES, LRU_BLOCK)
        b = b.reshape(n_vreg_rows, SUBLANES, LRU_BLOCK)
        s = 1
        while s < SUBLANES:
            keep = env["sub3"] >= s
            a_sh = jnp.where(keep, pltpu.roll(a, s, axis=1), 1.0)
            b_sh = jnp.where(keep, pltpu.roll(b, s, axis=1), 0.0)
            b = a * b_sh + b
            a = a * a_sh
            s *= 2
        a_scr[:, :, gs] = a.reshape(bt, tt, LRU_BLOCK)
        b_scr[:, :, gs] = b.reshape(bt, tt, LRU_BLOCK)

    def carry_groups(g0, g1):
        carry = env.get("carry")
        if carry is None:
            carry = h_scr[...]
        for i in range(g0, g1):
            rs = slice(i * SUBLANES, (i + 1) * SUBLANES)
            h = b_scr[:, rs, :] + a_scr[:, rs, :] * carry
            b_scr[:, rs, :] = h
            carry = h[:, SUBLANES - 1:SUBLANES, :]
        env["carry"] = carry
        if g1 == n_groups:
            h_scr[...] = carry

    def finish():
        h = b_scr[...].reshape(rows, width)
        out_ref[...] = (h * _silu(gate())).astype(out_ref.dtype)

    per_piece = min(n_groups, LRU_CARRY_PIECE)
    pieces = [functools.partial(carry_groups, g0, min(g0 + per_piece, n_groups))
              for g0 in range(0, n_groups, per_piece)]
    return [conv] + [functools.partial(block, g) for g in range(LRU_BLOCKS)] + pieces + [finish]


def _lru_body(x_ref, gain_ref, w_ref, cs_ref, h0_ref, cw_ref, cb_ref, wax_ref, ba_ref, bx_ref, alog_ref,
              out_ref, hout_ref, csout_ref, xp_scr, a_scr, b_scr, h_scr, *, bt, tt, reset_first):
    j = pl.program_id(1)

    @pl.when(j == 0)
    def _():
        xp_scr[:, 5:8, :] = cs_ref[...]
        h_scr[...] = h0_ref[...]

    u = _rms_norm_bf16(x_ref[...], gain_ref[...])
    z = jnp.dot(u, w_ref[...], preferred_element_type=F32)
    for stage in _lru_stages(lambda: z[:, :LRU_WIDTH], lambda: z[:, LRU_WIDTH:], j,
                             cw_ref, cb_ref, wax_ref, ba_ref, bx_ref, alog_ref, out_ref, xp_scr, a_scr, b_scr, h_scr, bt=bt, tt=tt,
                             reset_first=reset_first):
        stage()

    @pl.when(j == pl.num_programs(1) - 1)
    def _():
        hout_ref[...] = h_scr[...]
        csout_ref[...] = xp_scr[:, 5:8, :]


def _lru_scratch(bt, tt):
    return [
        pltpu.VMEM((bt, SUBLANES + tt, LRU_WIDTH), F32),
        pltpu.VMEM((bt, tt, LRU_WIDTH), F32),
        pltpu.VMEM((bt, tt, LRU_WIDTH), F32),
        pltpu.VMEM((bt, 1, LRU_WIDTH), F32),
    ]


def _lru(x2, gain, w_lru, conv_state, h0, cw, cb, wax, ba, bx, alog, *, nb, t, bt, tt, reset_first):
    nt = t // tt
    rows = bt * tt
    body = functools.partial(_lru_body, bt=bt, tt=tt, reset_first=reset_first)
    vec = lambda: pl.BlockSpec((1, LRU_WIDTH), lambda i, j: (0, 0))
    return pl.pallas_call(
        body,
        grid=(nb // bt, nt),
        in_specs=[
            pl.BlockSpec((rows, D_MODEL), lambda i, j: (i * nt + j, 0)),
            vec(),
            pl.BlockSpec((D_MODEL, LRU_Z_WIDTH), lambda i, j: (0, 0)),
            pl.BlockSpec((bt, CONV_W - 1, LRU_WIDTH), lambda i, j: (i, 0, 0)),
            pl.BlockSpec((bt, 1, LRU_WIDTH), lambda i, j: (i, 0, 0)),
            pl.BlockSpec((CONV_W, LRU_WIDTH), lambda i, j: (0, 0)),
            vec(),
            pl.BlockSpec((LRU_BLOCKS, LRU_BLOCK, 2 * LRU_BLOCK), lambda i, j: (0, 0, 0)),
            vec(), vec(), vec(),
        ],
        out_specs=[
            pl.BlockSpec((rows, LRU_WIDTH), lambda i, j: (i * nt + j, 0)),
            pl.BlockSpec((bt, 1, LRU_WIDTH), lambda i, j: (i, 0, 0)),
            pl.BlockSpec((bt, CONV_W - 1, LRU_WIDTH), lambda i, j: (i, 0, 0)),
        ],
        out_shape=[
            jax.ShapeDtypeStruct((nb * t, LRU_WIDTH), BF16),
            jax.ShapeDtypeStruct((nb, 1, LRU_WIDTH), F32),
            jax.ShapeDtypeStruct((nb, CONV_W - 1, LRU_WIDTH), F32),
        ],
        scratch_shapes=_lru_scratch(bt, tt),
        compiler_params=pltpu.CompilerParams(
            dimension_semantics=("arbitrary", "arbitrary"), vmem_limit_bytes=VMEM_LIMIT),
        name="lru",
    )(x2, gain, w_lru, conv_state, h0, cw, cb, wax, ba, bx, alog)


def _gdn_core(zq, zk, zv, zgate, ba, cw_refs, alog_ref, dtb_ref, nw_ref, out_ref, xp_scr, s_scr,
              h_grp, *, bt, tt, chunk, hg, seq_group, fill=lambda: None):
    rows = bt * tt
    width = LANES * hg
    n_chunks = tt // chunk

    def conv_silu(idx, cur2, cw_ref):
        cur = cur2.reshape(bt, tt, width)
        xp_scr[idx, :, 8:8 + tt, :] = cur
        cw = cw_ref[...]
        y = cur * cw[CONV_W - 1].reshape(1, 1, width)
        for i in range(CONV_W - 2, -1, -1):
            y = y + xp_scr[idx, :, 5 + i:5 + i + tt, :] * cw[i].reshape(1, 1, width)
        xp_scr[idx, :, 5:8, :] = xp_scr[idx, :, 5 + tt:8 + tt, :]
        return _silu(y.reshape(rows, width))

    q_all = conv_silu(0, zq(), cw_refs[0])
    k_all = conv_silu(1, zk(), cw_refs[1])
    v_all = conv_silu(2, zv(), cw_refs[2])
    ba = ba()

    beta_all = _sigmoid(ba)
    g_all = -jnp.exp(alog_ref[...]) * _softplus(ba + dtb_ref[...])
    row = lax.broadcasted_iota(jnp.int32, (rows, LANES), 0)
    row_in_chunk = row % chunk
    gc_all = g_all
    s = 1
    while s < chunk:
        gc_all = gc_all + jnp.where(row_in_chunk >= s, pltpu.roll(gc_all, s, axis=0), 0.0)
        s *= 2
    gct_all = gc_all.T

    lane = lax.broadcasted_iota(jnp.int32, (rows, LANES), 1)
    sub = lax.broadcasted_iota(jnp.int32, (LANES, rows), 0)
    ri = lax.broadcasted_iota(jnp.int32, (rows, rows), 0)
    ci = lax.broadcasted_iota(jnp.int32, (rows, rows), 1)
    same_chunk = (ri // chunk) == (ci // chunk)
    causal = jnp.logical_and(same_chunk, ri >= ci)
    strict = jnp.logical_and(same_chunk, ri > ci)
    n_steps = (chunk - 1).bit_length()

    hd = []
    for hh in range(hg):
        head = h_grp * hg + hh
        ls = slice(hh * LANES, (hh + 1) * LANES)
        qh, kh, vh = q_all[:, ls], k_all[:, ls], v_all[:, ls]
        qh = qh * lax.rsqrt(jnp.sum(qh * qh, axis=-1, keepdims=True) + NORM_EPS) * (GDN_DK ** -0.5)
        kh = kh * lax.rsqrt(jnp.sum(kh * kh, axis=-1, keepdims=True) + NORM_EPS)

        beta = jnp.sum(jnp.where(lane == BETA_LANE + head, beta_all, 0.0), axis=1, keepdims=True)
        gc = jnp.sum(jnp.where(lane == ALPHA_LANE + head, gc_all, 0.0), axis=1, keepdims=True)
        gc_row = jnp.sum(jnp.where(sub == ALPHA_LANE + head, gct_all, 0.0), axis=0, keepdims=True)

        decay = jnp.where(causal, jnp.exp(jnp.where(causal, gc - gc_row, 0.0)), 0.0)
        kb = kh * beta
        gram = _fdot(jnp.concatenate([kb, qh], axis=0), kh, ((1,), (1,)))
        e_gc = jnp.exp(gc)
        hd.append(dict(
            hh=hh, ls=ls, kh=kh, gc=gc,
            p=jnp.where(strict, -(gram[:rows] * decay), 0.0),
            a_qk=gram[rows:] * decay,
            x=jnp.concatenate([vh * beta, kb * e_gc], axis=1),
            q_dec=qh * e_gc))

    gate_env = {}

    def run_group(hd_g):
        n_dots = 0
        for st in range(n_steps):
            for d in hd_g:
                if st == 0:
                    d["n"] = d["p"]
                    if n_steps > 1:
                        d["p"] = _fdot(d["p"], d["p"])
                elif st < n_steps - 1:
                    res = _fdot(jnp.concatenate([d["n"], d["p"]], axis=0), d["p"])
                    d["n"] = d["n"] + d["p"] + res[:rows]
                    d["p"] = res[rows:]
                else:
                    d["n"] = d["n"] + d["p"] + _fdot(d["n"], d["p"])
                n_dots += 1
                if n_dots % FILL_EVERY == 0:
                    fill()
        for d in hd_g:
            d["x"] = d["x"] + _fdot(d["n"], d["x"])

        for d in hd_g:
            d["v_new"] = {}
            d["qs"] = {}
        for sq0 in range(0, bt, seq_group):
            pairs = [(sq, d["hh"]) for sq in range(sq0, min(sq0 + seq_group, bt)) for d in hd_g]
            state = {pr: s_scr[pr[0], pr[1]] for pr in pairs}
            for c in range(n_chunks):
                rsl = {sq: slice(sq * tt + c * chunk, sq * tt + (c + 1) * chunk) for sq, _ in pairs}
                ws, v_new, k_dec, e_last, upd = {}, {}, {}, {}, {}
                for pr in pairs:
                    d, rs = hd[pr[1]], rsl[pr[0]]
                    ws[pr] = _fdot(jnp.concatenate([d["x"][rs, LANES:], d["q_dec"][rs]], axis=0), state[pr])
                for pr in pairs:
                    d, rs = hd[pr[1]], rsl[pr[0]]
                    v_new[pr] = d["x"][rs, :LANES] - ws[pr][:chunk]
                    g_last = d["gc"][rs.stop - 1:rs.stop]
                    k_dec[pr] = d["kh"][rs] * jnp.exp(g_last - d["gc"][rs])
                    e_last[pr] = jnp.exp(g_last)
                    d["v_new"][(pr[0], c)] = v_new[pr]
                    d["qs"][(pr[0], c)] = ws[pr][chunk:]
                for pr in pairs:
                    upd[pr] = _fdot(k_dec[pr], v_new[pr], ((0,), (0,)))
                for pr in pairs:
                    state[pr] = state[pr] * e_last[pr] + upd[pr]
                fill()
            for pr in pairs:
                s_scr[pr[0], pr[1]] = state[pr]

        order = [(sq, c) for sq in range(bt) for c in range(n_chunks)]
        for d in hd_g:
            v_parts = [d["v_new"][k] for k in order]
            q_parts = [d["qs"][k] for k in order]
            v_new_all = jnp.concatenate(v_parts, axis=0) if len(v_parts) > 1 else v_parts[0]
            qs_all = jnp.concatenate(q_parts, axis=0) if len(q_parts) > 1 else q_parts[0]
            o = qs_all + _fdot(d["a_qk"], v_new_all)
            o = o * lax.rsqrt(jnp.mean(o * o, axis=-1, keepdims=True) + NORM_EPS) * nw_ref[...]
            if "gate" not in gate_env:
                gate_env["gate"] = zgate()
            o = o * _silu(gate_env["gate"][:, d["ls"]])
            out_ref[:, d["ls"]] = o.astype(out_ref.dtype)

    for g0 in range(0, hg, HEAD_GROUP):
        run_group(hd[g0:g0 + HEAD_GROUP])


def _gdn_init_state(j, cs_refs, s0_ref, xp_scr, s_scr):
    @pl.when(j == 0)
    def _():
        for idx, cs_ref in enumerate(cs_refs):
            xp_scr[idx, :, 5:8, :] = cs_ref[...]
        s_scr[...] = s0_ref[...]


def _gdn_z_body(zq_ref, zk_ref, zv_ref, zg_ref, ba_ref, csq_ref, csk_ref, csv_ref, s0_ref,
                cwq_ref, cwk_ref, cwv_ref, alog_ref, dtb_ref, nw_ref,
                out_ref, sout_ref, xp_scr, s_scr, *, bt, tt, chunk, hg):
    j = pl.program_id(2)
    _gdn_init_state(j, (csq_ref, csk_ref, csv_ref), s0_ref, xp_scr, s_scr)
    _gdn_core(lambda: zq_ref[...], lambda: zk_ref[...], lambda: zv_ref[...], lambda: zg_ref[...],
              lambda: ba_ref[...],
              (cwq_ref, cwk_ref, cwv_ref), alog_ref, dtb_ref, nw_ref, out_ref, xp_scr, s_scr,
              pl.program_id(1), bt=bt, tt=tt, chunk=chunk, hg=hg, seq_group=GDN_SEQ_GROUP)

    @pl.when(j == pl.num_programs(2) - 1)
    def _():
        sout_ref[...] = s_scr[...]


def _gdn_z(z, ba, conv_state, s0, cw, alog_pad, dtb_pad, nw, *, nb, t, bt, tt, chunk, hg):
    nt = t // tt
    rows = bt * tt
    width = LANES * hg
    per = SEG // width
    body = functools.partial(_gdn_z_body, bt=bt, tt=tt, chunk=chunk, hg=hg)

    def zspec(col):
        return pl.BlockSpec((rows, width), lambda i, h, j: (i * nt + j, col * per + h))

    def cs_spec(seg):
        return pl.BlockSpec((bt, CONV_W - 1, width), lambda i, h, j: (i, 0, seg * per + h))

    def cw_spec(seg):
        return pl.BlockSpec((CONV_W, width), lambda i, h, j: (0, seg * per + h))

    vec = lambda: pl.BlockSpec((1, LANES), lambda i, h, j: (0, 0))
    return pl.pallas_call(
        body,
        grid=(nb // bt, GDN_HEADS // hg, nt),
        in_specs=[
            zspec(COL_Q), zspec(COL_K), zspec(COL_V), zspec(COL_GDN_GATE),
            pl.BlockSpec((rows, LANES), lambda i, h, j: (i * nt + j, 0)),
            cs_spec(0), cs_spec(1), cs_spec(2),
            pl.BlockSpec((bt, hg, GDN_DK, GDN_DV), lambda i, h, j: (i, h, 0, 0)),
            cw_spec(0), cw_spec(1), cw_spec(2),
            vec(), vec(), vec(),
        ],
        out_specs=[
            pl.BlockSpec((rows, width), lambda i, h, j: (i * nt + j, h)),
            pl.BlockSpec((bt, hg, GDN_DK, GDN_DV), lambda i, h, j: (i, h, 0, 0)),
        ],
        out_shape=[
            jax.ShapeDtypeStruct((nb * t, GDN_VAL_W), BF16),
            jax.ShapeDtypeStruct((nb, GDN_HEADS, GDN_DK, GDN_DV), F32),
        ],
        scratch_shapes=[
            pltpu.VMEM((3, bt, SUBLANES + tt, width), F32),
            pltpu.VMEM((bt, hg, GDN_DK, GDN_DV), F32),
        ],
        compiler_params=pltpu.CompilerParams(
            dimension_semantics=("arbitrary", "arbitrary", "arbitrary"), vmem_limit_bytes=VMEM_LIMIT),
        name="gdn_z",
    )(z, z, z, z, ba, conv_state, conv_state, conv_state, s0, cw, cw, cw, alog_pad, dtb_pad, nw)


def _seq_body(x_ref, gain_ref, w_ref, wba_ref,
              lcs_ref, h0_ref, lcw_ref, lcb_ref, wax_ref, lba_ref, lbx_ref, lalog_ref,
              csq_ref, csk_ref, csv_ref, s0_ref, cwq_ref, cwk_ref, cwv_ref, galog_ref, dtb_ref, nw_ref,
              lru_out_ref, hout_ref, lcsout_ref, gdn_out_ref, sout_ref, gcsout_ref,
              lxp_scr, a_scr, b_scr, h_scr, gxp_scr, s_scr, *, bt, tt, chunk, reset_first):
    j = pl.program_id(1)

    @pl.when(j == 0)
    def _():
        lxp_scr[:, 5:8, :] = lcs_ref[...]
        h_scr[...] = h0_ref[...]

    _gdn_init_state(j, (csq_ref, csk_ref, csv_ref), s0_ref, gxp_scr, s_scr)

    u = _rms_norm_bf16(x_ref[...], gain_ref[...])

    def seg(c):
        return lambda: jnp.dot(u, w_ref[:, c * SEG:(c + 1) * SEG], preferred_element_type=F32)

    ba = lambda: jnp.dot(u, wba_ref[...], preferred_element_type=F32)

    stages = iter(_lru_stages(seg(0), seg(1), j, lcw_ref, lcb_ref, wax_ref, lba_ref, lbx_ref, lalog_ref,
                              lru_out_ref, lxp_scr, a_scr, b_scr, h_scr, bt=bt, tt=tt,
                              reset_first=reset_first))

    def fill():
        stage = next(stages, None)
        if stage is not None:
            stage()

    _gdn_core(seg(2 + COL_Q), seg(2 + COL_K), seg(2 + COL_V), seg(2 + COL_GDN_GATE), ba,
              (cwq_ref, cwk_ref, cwv_ref), galog_ref, dtb_ref, nw_ref, gdn_out_ref, gxp_scr, s_scr,
              0, bt=bt, tt=tt, chunk=chunk, hg=GDN_HEADS, seq_group=1, fill=fill)
    for stage in stages:
        stage()

    @pl.when(j == pl.num_programs(1) - 1)
    def _():
        hout_ref[...] = h_scr[...]
        lcsout_ref[...] = lxp_scr[:, 5:8, :]
        sout_ref[...] = s_scr[...]
        for idx in range(3):
            gcsout_ref[:, :, idx * SEG:(idx + 1) * SEG] = gxp_scr[idx, :, 5:8, :]


def _seq(x2, gain, w_seq, w_ba, lru_conv, h0, lcw, lcb, wax, lba, lbx, lalog,
         gdn_conv, s0, gcw, galog_pad, dtb_pad, nw, *, nb, t, bt, tt, chunk, reset_first):
    nt = t // tt
    rows = bt * tt
    hg = GDN_HEADS
    body = functools.partial(_seq_body, bt=bt, tt=tt, chunk=chunk, reset_first=reset_first)
    const2 = lambda i, j: (0, 0)
    vec = lambda n: pl.BlockSpec((1, n), const2)
    cs_spec = lambda seg: pl.BlockSpec((bt, CONV_W - 1, SEG), lambda i, j: (i, 0, seg))
    cw_spec = lambda seg: pl.BlockSpec((CONV_W, SEG), lambda i, j: (0, seg))
    state_spec = pl.BlockSpec((bt, hg, GDN_DK, GDN_DV), lambda i, j: (i, 0, 0, 0))
    return pl.pallas_call(
        body,
        grid=(nb // bt, nt),
        in_specs=[
            pl.BlockSpec((rows, D_MODEL), lambda i, j: (i * nt + j, 0)),
            vec(D_MODEL),
            _resident((D_MODEL, LRU_Z_WIDTH + GDN_Z_WIDTH), const2),
            _resident((D_MODEL, LANES), lambda i, j: (0, BA_COL_BLOCK)),
            pl.BlockSpec((bt, CONV_W - 1, LRU_WIDTH), lambda i, j: (i, 0, 0)),
            pl.BlockSpec((bt, 1, LRU_WIDTH), lambda i, j: (i, 0, 0)),
            pl.BlockSpec((CONV_W, LRU_WIDTH), const2),
            vec(LRU_WIDTH),
            pl.BlockSpec((LRU_BLOCKS, LRU_BLOCK, 2 * LRU_BLOCK), lambda i, j: (0, 0, 0)),
            vec(LRU_WIDTH), vec(LRU_WIDTH), vec(LRU_WIDTH),
            cs_spec(0), cs_spec(1), cs_spec(2),
            state_spec,
            cw_spec(0), cw_spec(1), cw_spec(2),
            vec(LANES), vec(LANES), vec(LANES),
        ],
        out_specs=[
            pl.BlockSpec((rows, LRU_WIDTH), lambda i, j: (i * nt + j, 0)),
            pl.BlockSpec((bt, 1, LRU_WIDTH), lambda i, j: (i, 0, 0)),
            pl.BlockSpec((bt, CONV_W - 1, LRU_WIDTH), lambda i, j: (i, 0, 0)),
            pl.BlockSpec((rows, GDN_VAL_W), lambda i, j: (i * nt + j, 0)),
            state_spec,
            pl.BlockSpec((bt, CONV_W - 1, GDN_QKV_W), lambda i, j: (i, 0, 0)),
        ],
        out_shape=[
            jax.ShapeDtypeStruct((nb * t, LRU_WIDTH), BF16),
            jax.ShapeDtypeStruct((nb, 1, LRU_WIDTH), F32),
            jax.ShapeDtypeStruct((nb, CONV_W - 1, LRU_WIDTH), F32),
            jax.ShapeDtypeStruct((nb * t, GDN_VAL_W), BF16),
            jax.ShapeDtypeStruct((nb, GDN_HEADS, GDN_DK, GDN_DV), F32),
            jax.ShapeDtypeStruct((nb, CONV_W - 1, GDN_QKV_W), F32),
        ],
        scratch_shapes=_lru_scratch(bt, tt) + [
            pltpu.VMEM((3, bt, SUBLANES + tt, SEG), F32),
            pltpu.VMEM((bt, hg, GDN_DK, GDN_DV), F32),
        ],
        compiler_params=pltpu.CompilerParams(
            dimension_semantics=("arbitrary", "arbitrary"), vmem_limit_bytes=SEQ_VMEM_LIMIT),
        name="seq",
    )(x2, gain, w_seq, w_ba, lru_conv, h0, lcw, lcb, wax, lba, lbx, lalog,
      gdn_conv, gdn_conv, gdn_conv, s0, gcw, gcw, gcw, galog_pad, dtb_pad, nw)


def _out_body(x_ref, gpre_ref, wm_ref, lru_ref, gdn_ref, wl_ref, wg_ref, wo_ref, gpost_ref, y_ref):
    x = x_ref[...]
    u = _rms_norm_bf16(x, gpre_ref[...])
    m = jnp.dot(u, wm_ref[...], preferred_element_type=F32)
    p_lru = jnp.dot(lru_ref[...], wl_ref[...], preferred_element_type=F32)
    p_gdn = jnp.dot(gdn_ref[...], wg_ref[...], preferred_element_type=F32)
    merged = _sigmoid(m[:, :D_MODEL]) * p_lru + _sigmoid(m[:, D_MODEL:]) * p_gdn
    y = jnp.dot(merged.astype(BF16), wo_ref[...], preferred_element_type=F32)
    ms = jnp.mean(y * y, axis=-1, keepdims=True)
    y_ref[...] = x + (y * lax.rsqrt(ms + NORM_EPS)) * gpost_ref[...]


def _out_proj(x2, gpre, w_m, lru_out, gdn_out, wl, wg, wo, gpost, *, tm):
    n = x2.shape[0]
    wspec = lambda: pl.BlockSpec((D_MODEL, D_MODEL), lambda i: (0, 0))
    vec = lambda: pl.BlockSpec((1, D_MODEL), lambda i: (0, 0))
    rowspec = lambda: pl.BlockSpec((tm, D_MODEL), lambda i: (i, 0))
    return pl.pallas_call(
        _out_body,
        grid=(n // tm,),
        in_specs=[
            rowspec(), vec(),
            pl.BlockSpec((D_MODEL, 2 * D_MODEL), lambda i: (0, 0)),
            rowspec(), rowspec(),
            wspec(), wspec(), wspec(),
            vec(),
        ],
        out_specs=rowspec(),
        out_shape=jax.ShapeDtypeStruct((n, D_MODEL), F32),
        compiler_params=pltpu.CompilerParams(
            dimension_semantics=("arbitrary",), vmem_limit_bytes=VMEM_LIMIT),
        name="out_proj",
    )(x2, gpre, w_m, lru_out, gdn_out, wl, wg, wo, gpost)


def _prep_weights(norm_pre, norm_post, w_in, lru_conv_w, lru_conv_b, lru_wa, lru_ba, lru_wx, lru_bx,
                  lru_a_logit, gdn_conv_w, gdn_A_log, gdn_dt_bias, gdn_norm_w, w_br_lru, w_br_gdn, w_out):
    c_m = BA_COL_BLOCK * LANES + 2 * GDN_HEADS
    wax = jnp.concatenate([lru_wa, lru_wx], axis=-1).astype(BF16)
    pad_alpha = lambda v: jnp.pad(v.reshape(1, GDN_HEADS), ((0, 0), (ALPHA_LANE, LANES - ALPHA_LANE - GDN_HEADS)))
    row = lambda v: v.reshape(1, -1)
    w_all = w_in.astype(BF16)
    return dict(
        norm_pre=row(norm_pre), norm_post=row(norm_post),
        w_all=w_all, w_m=w_all[:, c_m:],
        lru_cw=lru_conv_w, lru_cb=row(lru_conv_b), wax=wax, lru_ba=row(lru_ba), lru_bx=row(lru_bx),
        lru_alog=row(lru_a_logit), gdn_cw=gdn_conv_w, alog_pad=pad_alpha(gdn_A_log),
        dtb_pad=pad_alpha(gdn_dt_bias), gdn_nw=row(gdn_norm_w),
        wl=w_br_lru.astype(BF16), wg=w_br_gdn.astype(BF16), wo=w_out.astype(BF16))


def _layer(x, lru_conv, lru_h, gdn_conv, gdn_s, p, *, reset_first, fused, tm, lru_bt=None, lru_tt=None,
           gdn_bt=None, gdn_tt=None):
    nb, t, _ = x.shape
    x2 = x.reshape(nb * t, D_MODEL)
    h0 = lru_h.reshape(nb, 1, LRU_WIDTH)
    chunk = min(GDN_CHUNK, t)
    lru_w = (p["lru_cw"], p["lru_cb"], p["wax"], p["lru_ba"], p["lru_bx"], p["lru_alog"])
    gdn_w = (p["gdn_cw"], p["alog_pad"], p["dtb_pad"], p["gdn_nw"])
    if fused:
        lru_out, h_last, lru_conv_new, gdn_out, s_new, gdn_conv_new = _seq(
            x2, p["norm_pre"], p["w_all"], p["w_all"], lru_conv, h0, *lru_w, gdn_conv, gdn_s, *gdn_w,
            nb=nb, t=t, bt=1, tt=tm, chunk=chunk, reset_first=reset_first)
    else:
        lru_out, h_last, lru_conv_new = _lru(
            x2, p["norm_pre"], p["w_all"], lru_conv, h0, *lru_w, nb=nb, t=t, bt=lru_bt, tt=lru_tt,
            reset_first=reset_first)
        z, ba = _in_proj(x2, p["norm_pre"], p["w_all"], p["w_all"], tm=nb * t, tn=SEG,
                         col0=LRU_Z_WIDTH // SEG, width=GDN_Z_WIDTH)
        gdn_out, s_new = _gdn_z(z, ba, gdn_conv, gdn_s, *gdn_w, nb=nb, t=t, bt=gdn_bt, tt=gdn_tt, chunk=chunk,
                                hg=GDN_HEADS)
        gdn_conv_new = z.reshape(nb, t, GDN_Z_WIDTH)[:, t - (CONV_W - 1):, :GDN_QKV_W]
    y = _out_proj(x2, p["norm_pre"], p["w_m"], lru_out, gdn_out, p["wl"], p["wg"], p["wo"], p["norm_post"],
                  tm=min(512, nb * t))
    return y.reshape(nb, t, D_MODEL), lru_conv_new, h_last.reshape(nb, LRU_WIDTH), gdn_conv_new, s_new


def kernel(x_prompt, x_sample, state_lru_conv, state_lru_h, state_gdn_conv, state_gdn_S, norm_pre, norm_post, w_in, lru_conv_w, lru_conv_b, lru_wa, lru_ba, lru_wx, lru_bx, lru_a_logit, gdn_conv_w, gdn_A_log, gdn_dt_bias, gdn_norm_w, w_br_lru, w_br_gdn, w_out):
    depth = w_in.shape[0]
    assert depth == 1
    nb = x_prompt.shape[0]
    p = _prep_weights(norm_pre[0], norm_post[0], w_in[0], lru_conv_w[0], lru_conv_b[0], lru_wa[0], lru_ba[0],
                      lru_wx[0], lru_bx[0], lru_a_logit[0], gdn_conv_w[0], gdn_A_log[0], gdn_dt_bias[0],
                      gdn_norm_w[0], w_br_lru[0], w_br_gdn[0], w_out[0])
    yp, p_lc, p_lh, p_gc, p_gs = _layer(
        x_prompt,
        jnp.zeros((nb, CONV_W - 1, LRU_WIDTH), F32), jnp.zeros((nb, LRU_WIDTH), F32),
        jnp.zeros((nb, CONV_W - 1, GDN_QKV_W), F32), jnp.zeros((nb, GDN_HEADS, GDN_DK, GDN_DV), F32),
        p, reset_first=True, fused=True, tm=128)
    ys, s_lc, s_lh, s_gc, s_gs = _layer(
        x_sample, state_lru_conv[0], state_lru_h[0], state_gdn_conv[0], state_gdn_S[0],
        p, reset_first=False, fused=False, tm=1024, lru_bt=32, lru_tt=8, gdn_bt=8, gdn_tt=8)
    return (yp, ys, p_lc[None], p_lh[None], p_gc[None], p_gs[None],
            s_lc[None], s_lh[None], s_gc[None], s_gs[None])
```

```python
import functools

import jax
import jax.numpy as jnp
from jax import lax
from jax.experimental import pallas as pl
from jax.experimental.pallas import tpu as pltpu

F32 = jnp.float32
BF16 = jnp.bfloat16

D_MODEL = 1024
CONV_W = 4
LRU_WIDTH = 1024
LRU_BLOCKS = 8
LRU_BLOCK = LRU_WIDTH // LRU_BLOCKS
LRU_C = 8.0
GDN_HEADS = 8
GDN_DK = 128
GDN_DV = 128
GDN_KEY_W = GDN_HEADS * GDN_DK
GDN_VAL_W = GDN_HEADS * GDN_DV
GDN_QKV_W = 2 * GDN_KEY_W + GDN_VAL_W
GDN_CHUNK = 64
NORM_EPS = 1e-6

LANES = 128
SUBLANES = 8
SEG = 1024
LRU_Z_WIDTH = 2 * SEG
GDN_Z_WIDTH = 4 * SEG
COL_Q, COL_K, COL_V, COL_GDN_GATE = range(4)
BETA_LANE = 0
ALPHA_LANE = GDN_HEADS
BA_COL_BLOCK = (LRU_Z_WIDTH + GDN_Z_WIDTH) // LANES
VMEM_LIMIT = 48 * 1024 * 1024
SEQ_VMEM_LIMIT = 56 * 1024 * 1024
SEQ_TILE = 128
GDN_SEQ_GROUP = 2
LRU_CARRY_PIECE = 8
FILL_EVERY = 6


def _softplus(x):
    return jnp.maximum(x, 0.0) + jnp.log1p(jnp.exp(-jnp.abs(x)))


def _sigmoid(x):
    return jax.nn.sigmoid(x)


def _silu(x):
    h = 0.5 * x
    return h + h * jnp.tanh(h)


def _fdot(a, b, dims=((1,), (0,))):
    return lax.dot_general(a, b, (dims, ((), ())), preferred_element_type=F32)


def _rms_norm_bf16(x, gain):
    ms = jnp.mean(x * x, axis=-1, keepdims=True)
    return ((x * lax.rsqrt(ms + NORM_EPS)) * gain).astype(BF16)


def _resident(shape, index_map):
    return pl.BlockSpec(shape, index_map, pipeline_mode=pl.Buffered(1))


def _in_proj_body(x_ref, gain_ref, w_ref, wba_ref, z_ref, ba_ref, u_scr):
    @pl.when(pl.program_id(1) == 0)
    def _():
        u = _rms_norm_bf16(x_ref[...], gain_ref[...])
        u_scr[...] = u
        ba_ref[...] = jnp.dot(u, wba_ref[...], preferred_element_type=F32)

    z_ref[...] = jnp.dot(u_scr[...], w_ref[...], preferred_element_type=F32)


def _in_proj(x2, gain, w, w_ba, *, tm, tn, col0, width):
    n = x2.shape[0]
    return pl.pallas_call(
        _in_proj_body,
        grid=(n // tm, width // tn),
        in_specs=[
            pl.BlockSpec((tm, D_MODEL), lambda i, j: (i, 0)),
            pl.BlockSpec((1, D_MODEL), lambda i, j: (0, 0)),
            pl.BlockSpec((D_MODEL, tn), lambda i, j: (0, col0 + j)),
            pl.BlockSpec((D_MODEL, LANES), lambda i, j: (0, BA_COL_BLOCK)),
        ],
        out_specs=[
            pl.BlockSpec((tm, tn), lambda i, j: (i, j)),
            pl.BlockSpec((tm, LANES), lambda i, j: (i, 0)),
        ],
        out_shape=[
            jax.ShapeDtypeStruct((n, width), F32),
            jax.ShapeDtypeStruct((n, LANES), F32),
        ],
        scratch_shapes=[pltpu.VMEM((tm, D_MODEL), BF16)],
        compiler_params=pltpu.CompilerParams(
            dimension_semantics=("arbitrary", "arbitrary"), vmem_limit_bytes=VMEM_LIMIT),
        name="in_proj",
    )(x2, gain, w, w_ba)


def _init_conv_tail(tail_ref, cs_ref):
    tail_ref[...] = jnp.zeros(tail_ref.shape, tail_ref.dtype)
    tail_ref[:, 5:8, :] = cs_ref[...]


def _causal_conv(cur2, cw_ref, tail_ref, *, bt, tt):
    rows, width = cur2.shape
    n_v = tt // SUBLANES
    assert bt == 1 or n_v == 1
    cur = cur2.reshape(bt * n_v, SUBLANES, width)
    tail = tail_ref[...]
    tail_ref[...] = cur[n_v - 1:] if bt == 1 else cur
    sub = lax.broadcasted_iota(jnp.int32, cur.shape, 1)
    tap = lambda i: cw_ref[i:i + 1, :].reshape(1, 1, width)
    y = cur * tap(CONV_W - 1)
    for k in range(1, CONV_W):
        rot = pltpu.roll(cur, k, axis=1)
        before = pltpu.roll(tail, k, axis=1)
        if n_v > 1:
            before = jnp.concatenate([before, rot[:-1]], axis=0)
        y = y + jnp.where(sub < k, before, rot) * tap(CONV_W - 1 - k)
    return y.reshape(rows, width)


def _lru_stages(z_lru, gate, j, cw_ref, cb_ref, wax_ref, ba_ref, bx_ref, alog_ref, out_ref,
                xp_scr, a_scr, b_scr, h_scr, *, bt, tt, reset_first):
    rows = bt * tt
    width = LRU_WIDTH
    n_vreg_rows = rows // SUBLANES
    n_groups = tt // SUBLANES
    env = {}

    def conv():
        env["xc"] = _causal_conv(z_lru(), cw_ref, xp_scr, bt=bt, tt=tt) + cb_ref[...]
        a_l = alog_ref[...]
        env["log_sig_a"] = jnp.minimum(a_l, 0.0) - jnp.log1p(jnp.exp(-jnp.abs(a_l)))
        env["sub3"] = lax.broadcasted_iota(jnp.int32, (n_vreg_rows, SUBLANES, LRU_BLOCK), 1)
        if reset_first:
            row = lax.broadcasted_iota(jnp.int32, (rows, LRU_BLOCK), 0)
            env["is_reset"] = jnp.logical_and(row % tt == 0, j == 0)

    def block(g):
        gs = slice(g * LRU_BLOCK, (g + 1) * LRU_BLOCK)
        xg = env["xc"][:, gs]
        pre = jnp.dot(xg.astype(BF16), wax_ref[g], preferred_element_type=F32)
        r = _sigmoid(pre[:, :LRU_BLOCK] + ba_ref[:, gs])
        ig = _sigmoid(pre[:, LRU_BLOCK:] + bx_ref[:, gs])
        log_a = (LRU_C * r) * env["log_sig_a"][:, gs]
        a = jnp.exp(log_a)
        t = jnp.tanh(log_a)
        m2 = (-2.0 * t) / (1.0 - t)
        mult = jnp.where(m2 > 0.0, m2 * lax.rsqrt(m2), 0.0)
        if reset_first:
            a = jnp.where(env["is_reset"], 0.0, a)
            mult = jnp.where(env["is_reset"], 1.0, mult)
        b = mult * ig * xg
        a = a.reshape(n_vreg_rows, SUBLANES, LRU_BLOCK)
        b = b.reshape(n_vreg_rows, SUBLANES, LRU_BLOCK)
        s = 1
        while s < SUBLANES:
            keep = env["sub3"] >= s
            a_sh = jnp.where(keep, pltpu.roll(a, s, axis=1), 1.0)
            b_sh = jnp.where(keep, pltpu.roll(b, s, axis=1), 0.0)
            b = a * b_sh + b
            a = a * a_sh
            s *= 2
        a_scr[:, :, gs] = a.reshape(bt, tt, LRU_BLOCK)
        b_scr[:, :, gs] = b.reshape(bt, tt, LRU_BLOCK)

    def carry_groups(g0, g1):
        carry = env.get("carry")
        if carry is None:
            carry = h_scr[...]
        for i in range(g0, g1):
            rs = slice(i * SUBLANES, (i + 1) * SUBLANES)
            h = b_scr[:, rs, :] + a_scr[:, rs, :] * carry
            b_scr[:, rs, :] = h
            carry = h[:, SUBLANES - 1:SUBLANES, :]
        env["carry"] = carry
        if g1 == n_groups:
            h_scr[...] = carry

    def finish():
        h = b_scr[...].reshape(rows, width)
        out_ref[...] = (h * _silu(gate())).astype(out_ref.dtype)

    per_piece = min(n_groups, LRU_CARRY_PIECE)
    pieces = [functools.partial(carry_groups, g0, min(g0 + per_piece, n_groups))
              for g0 in range(0, n_groups, per_piece)]
    return [conv] + [functools.partial(block, g) for g in range(LRU_BLOCKS)] + pieces + [finish]


def _lru_body(x_ref, gain_ref, w_ref, cs_ref, h0_ref, cw_ref, cb_ref, wax_ref, ba_ref, bx_ref, alog_ref,
              out_ref, hout_ref, csout_ref, xp_scr, a_scr, b_scr, h_scr, *, bt, tt, reset_first):
    j = pl.program_id(1)

    @pl.when(j == 0)
    def _():
        _init_conv_tail(xp_scr, cs_ref)
        h_scr[...] = h0_ref[...]

    u = _rms_norm_bf16(x_ref[...], gain_ref[...])
    z = jnp.dot(u, w_ref[...], preferred_element_type=F32)
    for stage in _lru_stages(lambda: z[:, :LRU_WIDTH], lambda: z[:, LRU_WIDTH:], j,
                             cw_ref, cb_ref, wax_ref, ba_ref, bx_ref, alog_ref, out_ref, xp_scr, a_scr, b_scr, h_scr, bt=bt, tt=tt,
                             reset_first=reset_first):
        stage()

    @pl.when(j == pl.num_programs(1) - 1)
    def _():
        hout_ref[...] = h_scr[...]
        csout_ref[...] = xp_scr[:, 5:8, :]


def _lru_scratch(bt, tt):
    return [
        pltpu.VMEM((bt, SUBLANES, LRU_WIDTH), F32),
        pltpu.VMEM((bt, tt, LRU_WIDTH), F32),
        pltpu.VMEM((bt, tt, LRU_WIDTH), F32),
        pltpu.VMEM((bt, 1, LRU_WIDTH), F32),
    ]


def _lru(x2, gain, w_lru, conv_state, h0, cw, cb, wax, ba, bx, alog, *, nb, t, bt, tt, reset_first):
    nt = t // tt
    rows = bt * tt
    body = functools.partial(_lru_body, bt=bt, tt=tt, reset_first=reset_first)
    vec = lambda: pl.BlockSpec((1, LRU_WIDTH), lambda i, j: (0, 0))
    return pl.pallas_call(
        body,
        grid=(nb // bt, nt),
        in_specs=[
            pl.BlockSpec((rows, D_MODEL), lambda i, j: (i * nt + j, 0)),
            vec(),
            pl.BlockSpec((D_MODEL, LRU_Z_WIDTH), lambda i, j: (0, 0)),
            pl.BlockSpec((bt, CONV_W - 1, LRU_WIDTH), lambda i, j: (i, 0, 0)),
            pl.BlockSpec((bt, 1, LRU_WIDTH), lambda i, j: (i, 0, 0)),
            pl.BlockSpec((CONV_W, LRU_WIDTH), lambda i, j: (0, 0)),
            vec(),
            pl.BlockSpec((LRU_BLOCKS, LRU_BLOCK, 2 * LRU_BLOCK), lambda i, j: (0, 0, 0)),
            vec(), vec(), vec(),
        ],
        out_specs=[
            pl.BlockSpec((rows, LRU_WIDTH), lambda i, j: (i * nt + j, 0)),
            pl.BlockSpec((bt, 1, LRU_WIDTH), lambda i, j: (i, 0, 0)),
            pl.BlockSpec((bt, CONV_W - 1, LRU_WIDTH), lambda i, j: (i, 0, 0)),
        ],
        out_shape=[
            jax.ShapeDtypeStruct((nb * t, LRU_WIDTH), BF16),
            jax.ShapeDtypeStruct((nb, 1, LRU_WIDTH), F32),
            jax.ShapeDtypeStruct((nb, CONV_W - 1, LRU_WIDTH), F32),
        ],
        scratch_shapes=_lru_scratch(bt, tt),
        compiler_params=pltpu.CompilerParams(
            dimension_semantics=("arbitrary", "arbitrary"), vmem_limit_bytes=VMEM_LIMIT),
        name="lru",
    )(x2, gain, w_lru, conv_state, h0, cw, cb, wax, ba, bx, alog)


def _gdn_core(zq, zk, zv, zgate, ba, cw_refs, alog_ref, dtb_ref, nw_ref, out_ref, xp_scr, s_scr,
              h_grp, *, bt, tt, chunk, hg, seq_group, fill=lambda: None):
    rows = bt * tt
    width = LANES * hg
    n_chunks = tt // chunk

    def conv_silu(idx, cur2, cw_ref):
        return _silu(_causal_conv(cur2, cw_ref, xp_scr.at[idx], bt=bt, tt=tt))

    q_all = conv_silu(0, zq(), cw_refs[0])
    k_all = conv_silu(1, zk(), cw_refs[1])
    v_all = conv_silu(2, zv(), cw_refs[2])
    ba = ba()

    beta_all = _sigmoid(ba)
    g_all = -jnp.exp(alog_ref[...]) * _softplus(ba + dtb_ref[...])
    row = lax.broadcasted_iota(jnp.int32, (rows, LANES), 0)
    row_in_chunk = row % chunk
    gc_all = g_all
    s = 1
    while s < chunk:
        gc_all = gc_all + jnp.where(row_in_chunk >= s, pltpu.roll(gc_all, s, axis=0), 0.0)
        s *= 2
    gct_all = gc_all.T

    lane = lax.broadcasted_iota(jnp.int32, (rows, LANES), 1)
    sub = lax.broadcasted_iota(jnp.int32, (LANES, rows), 0)
    ri = lax.broadcasted_iota(jnp.int32, (rows, rows), 0)
    ci = lax.broadcasted_iota(jnp.int32, (rows, rows), 1)
    same_chunk = (ri // chunk) == (ci // chunk)
    causal = jnp.logical_and(same_chunk, ri >= ci)
    strict = jnp.logical_and(same_chunk, ri > ci)
    n_steps = (chunk - 1).bit_length()

    hd = []
    for hh in range(hg):
        head = h_grp * hg + hh
        ls = slice(hh * LANES, (hh + 1) * LANES)
        qh, kh, vh = q_all[:, ls], k_all[:, ls], v_all[:, ls]
        qh = qh * lax.rsqrt(jnp.sum(qh * qh, axis=-1, keepdims=True) + NORM_EPS) * (GDN_DK ** -0.5)
        kh = kh * lax.rsqrt(jnp.sum(kh * kh, axis=-1, keepdims=True) + NORM_EPS)

        beta = jnp.sum(jnp.where(lane == BETA_LANE + head, beta_all, 0.0), axis=1, keepdims=True)
        gc = jnp.sum(jnp.where(lane == ALPHA_LANE + head, gc_all, 0.0), axis=1, keepdims=True)
        gc_row = jnp.sum(jnp.where(sub == ALPHA_LANE + head, gct_all, 0.0), axis=0, keepdims=True)

        decay = jnp.where(causal, jnp.exp(jnp.where(causal, gc - gc_row, 0.0)), 0.0)
        kb = kh * beta
        gram = _fdot(jnp.concatenate([kb, qh], axis=0), kh, ((1,), (1,)))
        e_gc = jnp.exp(gc)
        hd.append(dict(
            ls=ls, kh=kh, gc=gc,
            p=jnp.where(strict, -(gram[:rows] * decay), 0.0),
            a_qk=gram[rows:] * decay,
            x=jnp.concatenate([vh * beta, kb * e_gc], axis=1),
            q_dec=qh * e_gc))

    n_dots = 0
    for st in range(n_steps):
        for d in hd:
            if st == 0:
                d["n"] = d["p"]
                if n_steps > 1:
                    d["p"] = _fdot(d["p"], d["p"])
            elif st < n_steps - 1:
                res = _fdot(jnp.concatenate([d["n"], d["p"]], axis=0), d["p"])
                d["n"] = d["n"] + d["p"] + res[:rows]
                d["p"] = res[rows:]
            else:
                d["n"] = d["n"] + d["p"] + _fdot(d["n"], d["p"])
            n_dots += 1
            if n_dots % FILL_EVERY == 0:
                fill()
    for d in hd:
        d["x"] = d["x"] + _fdot(d["n"], d["x"])

    for d in hd:
        d["v_new"] = {}
        d["qs"] = {}
    for sq0 in range(0, bt, seq_group):
        pairs = [(sq, hh) for sq in range(sq0, min(sq0 + seq_group, bt)) for hh in range(hg)]
        state = {pr: s_scr[pr[0], pr[1]] for pr in pairs}
        for c in range(n_chunks):
            rsl = {sq: slice(sq * tt + c * chunk, sq * tt + (c + 1) * chunk) for sq, _ in pairs}
            ws, v_new, k_dec, e_last, upd = {}, {}, {}, {}, {}
            for pr in pairs:
                d, rs = hd[pr[1]], rsl[pr[0]]
                ws[pr] = _fdot(jnp.concatenate([d["x"][rs, LANES:], d["q_dec"][rs]], axis=0), state[pr])
            for pr in pairs:
                d, rs = hd[pr[1]], rsl[pr[0]]
                v_new[pr] = d["x"][rs, :LANES] - ws[pr][:chunk]
                g_last = d["gc"][rs.stop - 1:rs.stop]
                k_dec[pr] = d["kh"][rs] * jnp.exp(g_last - d["gc"][rs])
                e_last[pr] = jnp.exp(g_last)
                d["v_new"][(pr[0], c)] = v_new[pr]
                d["qs"][(pr[0], c)] = ws[pr][chunk:]
            for pr in pairs:
                upd[pr] = _fdot(k_dec[pr], v_new[pr], ((0,), (0,)))
            for pr in pairs:
                state[pr] = state[pr] * e_last[pr] + upd[pr]
            fill()
        for pr in pairs:
            s_scr[pr[0], pr[1]] = state[pr]

    order = [(sq, c) for sq in range(bt) for c in range(n_chunks)]
    gate = zgate()
    for d in hd:
        v_parts = [d["v_new"][k] for k in order]
        q_parts = [d["qs"][k] for k in order]
        v_new_all = jnp.concatenate(v_parts, axis=0) if len(v_parts) > 1 else v_parts[0]
        qs_all = jnp.concatenate(q_parts, axis=0) if len(q_parts) > 1 else q_parts[0]
        o = qs_all + _fdot(d["a_qk"], v_new_all)
        o = o * lax.rsqrt(jnp.mean(o * o, axis=-1, keepdims=True) + NORM_EPS) * nw_ref[...]
        o = o * _silu(gate[:, d["ls"]])
        out_ref[:, d["ls"]] = o.astype(out_ref.dtype)


def _gdn_init_state(j, cs_refs, s0_ref, xp_scr, s_scr):
    @pl.when(j == 0)
    def _():
        for idx, cs_ref in enumerate(cs_refs):
            _init_conv_tail(xp_scr.at[idx], cs_ref)
        s_scr[...] = s0_ref[...]


def _gdn_z_body(zq_ref, zk_ref, zv_ref, zg_ref, ba_ref, csq_ref, csk_ref, csv_ref, s0_ref,
                cwq_ref, cwk_ref, cwv_ref, alog_ref, dtb_ref, nw_ref,
                out_ref, sout_ref, xp_scr, s_scr, *, bt, tt, chunk, hg):
    j = pl.program_id(2)
    _gdn_init_state(j, (csq_ref, csk_ref, csv_ref), s0_ref, xp_scr, s_scr)
    _gdn_core(lambda: zq_ref[...], lambda: zk_ref[...], lambda: zv_ref[...], lambda: zg_ref[...],
              lambda: ba_ref[...],
              (cwq_ref, cwk_ref, cwv_ref), alog_ref, dtb_ref, nw_ref, out_ref, xp_scr, s_scr,
              pl.program_id(1), bt=bt, tt=tt, chunk=chunk, hg=hg, seq_group=GDN_SEQ_GROUP)

    @pl.when(j == pl.num_programs(2) - 1)
    def _():
        sout_ref[...] = s_scr[...]


def _gdn_z(z, ba, conv_state, s0, cw, alog_pad, dtb_pad, nw, *, nb, t, bt, tt, chunk, hg):
    nt = t // tt
    rows = bt * tt
    width = LANES * hg
    per = SEG // width
    body = functools.partial(_gdn_z_body, bt=bt, tt=tt, chunk=chunk, hg=hg)

    def zspec(col):
        return pl.BlockSpec((rows, width), lambda i, h, j: (i * nt + j, col * per + h))

    def cs_spec(seg):
        return pl.BlockSpec((bt, CONV_W - 1, width), lambda i, h, j: (i, 0, seg * per + h))

    def cw_spec(seg):
        return pl.BlockSpec((CONV_W, width), lambda i, h, j: (0, seg * per + h))

    vec = lambda: pl.BlockSpec((1, LANES), lambda i, h, j: (0, 0))
    return pl.pallas_call(
        body,
        grid=(nb // bt, GDN_HEADS // hg, nt),
        in_specs=[
            zspec(COL_Q), zspec(COL_K), zspec(COL_V), zspec(COL_GDN_GATE),
            pl.BlockSpec((rows, LANES), lambda i, h, j: (i * nt + j, 0)),
            cs_spec(0), cs_spec(1), cs_spec(2),
            pl.BlockSpec((bt, hg, GDN_DK, GDN_DV), lambda i, h, j: (i, h, 0, 0)),
            cw_spec(0), cw_spec(1), cw_spec(2),
            vec(), vec(), vec(),
        ],
        out_specs=[
            pl.BlockSpec((rows, width), lambda i, h, j: (i * nt + j, h)),
            pl.BlockSpec((bt, hg, GDN_DK, GDN_DV), lambda i, h, j: (i, h, 0, 0)),
        ],
        out_shape=[
            jax.ShapeDtypeStruct((nb * t, GDN_VAL_W), BF16),
            jax.ShapeDtypeStruct((nb, GDN_HEADS, GDN_DK, GDN_DV), F32),
        ],
        scratch_shapes=[
            pltpu.VMEM((3, bt, SUBLANES, width), F32),
            pltpu.VMEM((bt, hg, GDN_DK, GDN_DV), F32),
        ],
        compiler_params=pltpu.CompilerParams(
            dimension_semantics=("arbitrary", "arbitrary", "arbitrary"), vmem_limit_bytes=VMEM_LIMIT),
        name="gdn_z",
    )(z, z, z, z, ba, conv_state, conv_state, conv_state, s0, cw, cw, cw, alog_pad, dtb_pad, nw)


def _seq_body(x_ref, gain_ref, w_ref, wba_ref,
              lcs_ref, h0_ref, lcw_ref, lcb_ref, wax_ref, lba_ref, lbx_ref, lalog_ref,
              csq_ref, csk_ref, csv_ref, s0_ref, cwq_ref, cwk_ref, cwv_ref, galog_ref, dtb_ref, nw_ref,
              lru_out_ref, hout_ref, lcsout_ref, gdn_out_ref, sout_ref, gcsout_ref,
              lxp_scr, a_scr, b_scr, h_scr, gxp_scr, s_scr, *, bt, tt, chunk, reset_first):
    j = pl.program_id(1)

    @pl.when(j == 0)
    def _():
        _init_conv_tail(lxp_scr, lcs_ref)
        h_scr[...] = h0_ref[...]

    _gdn_init_state(j, (csq_ref, csk_ref, csv_ref), s0_ref, gxp_scr, s_scr)

    u = _rms_norm_bf16(x_ref[...], gain_ref[...])

    def seg(c):
        return lambda: jnp.dot(u, w_ref[:, c * SEG:(c + 1) * SEG], preferred_element_type=F32)

    ba = lambda: jnp.dot(u, wba_ref[...], preferred_element_type=F32)

    stages = iter(_lru_stages(seg(0), seg(1), j, lcw_ref, lcb_ref, wax_ref, lba_ref, lbx_ref, lalog_ref,
                              lru_out_ref, lxp_scr, a_scr, b_scr, h_scr, bt=bt, tt=tt,
                              reset_first=reset_first))

    def fill():
        stage = next(stages, None)
        if stage is not None:
            stage()

    _gdn_core(seg(2 + COL_Q), seg(2 + COL_K), seg(2 + COL_V), seg(2 + COL_GDN_GATE), ba,
              (cwq_ref, cwk_ref, cwv_ref), galog_ref, dtb_ref, nw_ref, gdn_out_ref, gxp_scr, s_scr,
              0, bt=bt, tt=tt, chunk=chunk, hg=GDN_HEADS, seq_group=1, fill=fill)
    for stage in stages:
        stage()

    @pl.when(j == pl.num_programs(1) - 1)
    def _():
        hout_ref[...] = h_scr[...]
        lcsout_ref[...] = lxp_scr[:, 5:8, :]
        sout_ref[...] = s_scr[...]
        for idx in range(3):
            gcsout_ref[:, :, idx * SEG:(idx + 1) * SEG] = gxp_scr[idx, :, 5:8, :]


def _seq(x2, gain, w_seq, w_ba, lru_conv, h0, lcw, lcb, wax, lba, lbx, lalog,
         gdn_conv, s0, gcw, galog_pad, dtb_pad, nw, *, nb, t, bt, tt, chunk, reset_first):
    nt = t // tt
    rows = bt * tt
    hg = GDN_HEADS
    body = functools.partial(_seq_body, bt=bt, tt=tt, chunk=chunk, reset_first=reset_first)
    const2 = lambda i, j: (0, 0)
    vec = lambda n: pl.BlockSpec((1, n), const2)
    cs_spec = lambda seg: pl.BlockSpec((bt, CONV_W - 1, SEG), lambda i, j: (i, 0, seg))
    cw_spec = lambda seg: pl.BlockSpec((CONV_W, SEG), lambda i, j: (0, seg))
    state_spec = pl.BlockSpec((bt, hg, GDN_DK, GDN_DV), lambda i, j: (i, 0, 0, 0))
    return pl.pallas_call(
        body,
        grid=(nb // bt, nt),
        in_specs=[
            pl.BlockSpec((rows, D_MODEL), lambda i, j: (i * nt + j, 0)),
            vec(D_MODEL),
            _resident((D_MODEL, LRU_Z_WIDTH + GDN_Z_WIDTH), const2),
            _resident((D_MODEL, LANES), lambda i, j: (0, BA_COL_BLOCK)),
            pl.BlockSpec((bt, CONV_W - 1, LRU_WIDTH), lambda i, j: (i, 0, 0)),
            pl.BlockSpec((bt, 1, LRU_WIDTH), lambda i, j: (i, 0, 0)),
            pl.BlockSpec((CONV_W, LRU_WIDTH), const2),
            vec(LRU_WIDTH),
            pl.BlockSpec((LRU_BLOCKS, LRU_BLOCK, 2 * LRU_BLOCK), lambda i, j: (0, 0, 0)),
            vec(LRU_WIDTH), vec(LRU_WIDTH), vec(LRU_WIDTH),
            cs_spec(0), cs_spec(1), cs_spec(2),
            state_spec,
            cw_spec(0), cw_spec(1), cw_spec(2),
            vec(LANES), vec(LANES), vec(LANES),
        ],
        out_specs=[
            pl.BlockSpec((rows, LRU_WIDTH), lambda i, j: (i * nt + j, 0)),
            pl.BlockSpec((bt, 1, LRU_WIDTH), lambda i, j: (i, 0, 0)),
            pl.BlockSpec((bt, CONV_W - 1, LRU_WIDTH), lambda i, j: (i, 0, 0)),
            pl.BlockSpec((rows, GDN_VAL_W), lambda i, j: (i * nt + j, 0)),
            state_spec,
            pl.BlockSpec((bt, CONV_W - 1, GDN_QKV_W), lambda i, j: (i, 0, 0)),
        ],
        out_shape=[
            jax.ShapeDtypeStruct((nb * t, LRU_WIDTH), BF16),
            jax.ShapeDtypeStruct((nb, 1, LRU_WIDTH), F32),
            jax.ShapeDtypeStruct((nb, CONV_W - 1, LRU_WIDTH), F32),
            jax.ShapeDtypeStruct((nb * t, GDN_VAL_W), BF16),
            jax.ShapeDtypeStruct((nb, GDN_HEADS, GDN_DK, GDN_DV), F32),
            jax.ShapeDtypeStruct((nb, CONV_W - 1, GDN_QKV_W), F32),
        ],
        scratch_shapes=_lru_scratch(bt, tt) + [
            pltpu.VMEM((3, bt, SUBLANES, SEG), F32),
            pltpu.VMEM((bt, hg, GDN_DK, GDN_DV), F32),
        ],
        compiler_params=pltpu.CompilerParams(
            dimension_semantics=("arbitrary", "arbitrary"), vmem_limit_bytes=SEQ_VMEM_LIMIT),
        name="seq",
    )(x2, gain, w_seq, w_ba, lru_conv, h0, lcw, lcb, wax, lba, lbx, lalog,
      gdn_conv, gdn_conv, gdn_conv, s0, gcw, gcw, gcw, galog_pad, dtb_pad, nw)


def _out_body(x_ref, gpre_ref, wm_ref, lru_ref, gdn_ref, wl_ref, wg_ref, wo_ref, gpost_ref, y_ref):
    x = x_ref[...]
    u = _rms_norm_bf16(x, gpre_ref[...])
    m = jnp.dot(u, wm_ref[...], preferred_element_type=F32)
    p_lru = jnp.dot(lru_ref[...], wl_ref[...], preferred_element_type=F32)
    p_gdn = jnp.dot(gdn_ref[...], wg_ref[...], preferred_element_type=F32)
    merged = _sigmoid(m[:, :D_MODEL]) * p_lru + _sigmoid(m[:, D_MODEL:]) * p_gdn
    y = jnp.dot(merged.astype(BF16), wo_ref[...], preferred_element_type=F32)
    ms = jnp.mean(y * y, axis=-1, keepdims=True)
    y_ref[...] = x + (y * lax.rsqrt(ms + NORM_EPS)) * gpost_ref[...]


def _out_proj(x2, gpre, w_m, lru_out, gdn_out, wl, wg, wo, gpost, *, tm):
    n = x2.shape[0]
    wspec = lambda: pl.BlockSpec((D_MODEL, D_MODEL), lambda i: (0, 0))
    vec = lambda: pl.BlockSpec((1, D_MODEL), lambda i: (0, 0))
    rowspec = lambda: pl.BlockSpec((tm, D_MODEL), lambda i: (i, 0))
    return pl.pallas_call(
        _out_body,
        grid=(n // tm,),
        in_specs=[
            rowspec(), vec(),
            pl.BlockSpec((D_MODEL, 2 * D_MODEL), lambda i: (0, 0)),
            rowspec(), rowspec(),
            wspec(), wspec(), wspec(),
            vec(),
        ],
        out_specs=rowspec(),
        out_shape=jax.ShapeDtypeStruct((n, D_MODEL), F32),
        compiler_params=pltpu.CompilerParams(
            dimension_semantics=("arbitrary",), vmem_limit_bytes=VMEM_LIMIT),
        name="out_proj",
    )(x2, gpre, w_m, lru_out, gdn_out, wl, wg, wo, gpost)


def _prep_weights(norm_pre, norm_post, w_in, lru_conv_w, lru_conv_b, lru_wa, lru_ba, lru_wx, lru_bx,
                  lru_a_logit, gdn_conv_w, gdn_A_log, gdn_dt_bias, gdn_norm_w, w_br_lru, w_br_gdn, w_out):
    c_m = BA_COL_BLOCK * LANES + 2 * GDN_HEADS
    wax = jnp.concatenate([lru_wa, lru_wx], axis=-1).astype(BF16)
    pad_alpha = lambda v: jnp.pad(v.reshape(1, GDN_HEADS), ((0, 0), (ALPHA_LANE, LANES - ALPHA_LANE - GDN_HEADS)))
    row = lambda v: v.reshape(1, -1)
    w_all = w_in.astype(BF16)
    return dict(
        norm_pre=row(norm_pre), norm_post=row(norm_post),
        w_all=w_all, w_m=w_all[:, c_m:],
        lru_cw=lru_conv_w, lru_cb=row(lru_conv_b), wax=wax, lru_ba=row(lru_ba), lru_bx=row(lru_bx),
        lru_alog=row(lru_a_logit), gdn_cw=gdn_conv_w, alog_pad=pad_alpha(gdn_A_log),
        dtb_pad=pad_alpha(gdn_dt_bias), gdn_nw=row(gdn_norm_w),
        wl=w_br_lru.astype(BF16), wg=w_br_gdn.astype(BF16), wo=w_out.astype(BF16))


def _layer(x, lru_conv, lru_h, gdn_conv, gdn_s, p, *, reset_first, fused, tm, lru_bt=None, lru_tt=None,
           gdn_bt=None, gdn_tt=None):
    nb, t, _ = x.shape
    x2 = x.reshape(nb * t, D_MODEL)
    h0 = lru_h.reshape(nb, 1, LRU_WIDTH)
    chunk = min(GDN_CHUNK, t)
    lru_w = (p["lru_cw"], p["lru_cb"], p["wax"], p["lru_ba"], p["lru_bx"], p["lru_alog"])
    gdn_w = (p["gdn_cw"], p["alog_pad"], p["dtb_pad"], p["gdn_nw"])
    if fused:
        lru_out, h_last, lru_conv_new, gdn_out, s_new, gdn_conv_new = _seq(
            x2, p["norm_pre"], p["w_all"], p["w_all"], lru_conv, h0, *lru_w, gdn_conv, gdn_s, *gdn_w,
            nb=nb, t=t, bt=1, tt=tm, chunk=chunk, reset_first=reset_first)
    else:
        lru_out, h_last, lru_conv_new = _lru(
            x2, p["norm_pre"], p["w_all"], lru_conv, h0, *lru_w, nb=nb, t=t, bt=lru_bt, tt=lru_tt,
            reset_first=reset_first)
        z, ba = _in_proj(x2, p["norm_pre"], p["w_all"], p["w_all"], tm=nb * t, tn=SEG,
                         col0=LRU_Z_WIDTH // SEG, width=GDN_Z_WIDTH)
        gdn_out, s_new = _gdn_z(z, ba, gdn_conv, gdn_s, *gdn_w, nb=nb, t=t, bt=gdn_bt, tt=gdn_tt, chunk=chunk,
                                hg=GDN_HEADS)
        gdn_conv_new = z.reshape(nb, t, GDN_Z_WIDTH)[:, t - (CONV_W - 1):, :GDN_QKV_W]
    y = _out_proj(x2, p["norm_pre"], p["w_m"], lru_out, gdn_out, p["wl"], p["wg"], p["wo"], p["norm_post"],
                  tm=min(512, nb * t))
    return y.reshape(nb, t, D_MODEL), lru_conv_new, h_last.reshape(nb, LRU_WIDTH), gdn_conv_new, s_new


def kernel(x_prompt, x_sample, state_lru_conv, state_lru_h, state_gdn_conv, state_gdn_S, norm_pre, norm_post, w_in, lru_conv_w, lru_conv_b, lru_wa, lru_ba, lru_wx, lru_bx, lru_a_logit, gdn_conv_w, gdn_A_log, gdn_dt_bias, gdn_norm_w, w_br_lru, w_br_gdn, w_out):
    depth = w_in.shape[0]
    assert depth == 1
    nb = x_prompt.shape[0]
    p = _prep_weights(norm_pre[0], norm_post[0], w_in[0], lru_conv_w[0], lru_conv_b[0], lru_wa[0], lru_ba[0],
                      lru_wx[0], lru_bx[0], lru_a_logit[0], gdn_conv_w[0], gdn_A_log[0], gdn_dt_bias[0],
                      gdn_norm_w[0], w_br_lru[0], w_br_gdn[0], w_out[0])
    yp, p_lc, p_lh, p_gc, p_gs = _layer(
        x_prompt,
        jnp.zeros((nb, CONV_W - 1, LRU_WIDTH), F32), jnp.zeros((nb, LRU_WIDTH), F32),
        jnp.zeros((nb, CONV_W - 1, GDN_QKV_W), F32), jnp.zeros((nb, GDN_HEADS, GDN_DK, GDN_DV), F32),
        p, reset_first=True, fused=True, tm=SEQ_TILE)
    ys, s_lc, s_lh, s_gc, s_gs = _layer(
        x_sample, state_lru_conv[0], state_lru_h[0], state_gdn_conv[0], state_gdn_S[0],
        p, reset_first=False, fused=False, tm=1024, lru_bt=32, lru_tt=8, gdn_bt=8, gdn_tt=8)
    return (yp, ys, p_lc[None], p_lh[None], p_gc[None], p_gs[None],
            s_lc[None], s_lh[None], s_gc[None], s_gs[None])
```

```python
import functools

import jax
import jax.numpy as jnp
from jax import lax
from jax.experimental import pallas as pl
from jax.experimental.pallas import tpu as pltpu

F32 = jnp.float32
BF16 = jnp.bfloat16

D_MODEL = 1024
CONV_W = 4
LRU_WIDTH = 1024
LRU_BLOCKS = 8
LRU_BLOCK = LRU_WIDTH // LRU_BLOCKS
LRU_C = 8.0
GDN_HEADS = 8
GDN_DK = 128
GDN_DV = 128
GDN_KEY_W = GDN_HEADS * GDN_DK
GDN_VAL_W = GDN_HEADS * GDN_DV
GDN_QKV_W = 2 * GDN_KEY_W + GDN_VAL_W
GDN_CHUNK = 64
NORM_EPS = 1e-6

LANES = 128
SUBLANES = 8
SEG = 1024
LRU_Z_WIDTH = 2 * SEG
GDN_Z_WIDTH = 4 * SEG
COL_Q, COL_K, COL_V, COL_GDN_GATE = range(4)
BETA_LANE = 0
ALPHA_LANE = GDN_HEADS
BA_COL_BLOCK = (LRU_Z_WIDTH + GDN_Z_WIDTH) // LANES
VMEM_LIMIT = 48 * 1024 * 1024
SEQ_VMEM_LIMIT = 60 * 1024 * 1024
SEQ_TILE = 128
GDN_SEQ_GROUP = 2
LRU_CARRY_PIECE = 8
FILL_EVERY = 6


def _softplus(x):
    return jnp.maximum(x, 0.0) + jnp.log1p(jnp.exp(-jnp.abs(x)))


def _sigmoid(x):
    return jax.nn.sigmoid(x)


def _silu(x):
    h = 0.5 * x
    return h + h * jnp.tanh(h)


def _fdot(a, b, dims=((1,), (0,))):
    return lax.dot_general(a, b, (dims, ((), ())), preferred_element_type=F32)


def _rms_norm_bf16(x, gain):
    ms = jnp.mean(x * x, axis=-1, keepdims=True)
    return ((x * lax.rsqrt(ms + NORM_EPS)) * gain).astype(BF16)


def _resident(shape, index_map):
    return pl.BlockSpec(shape, index_map, pipeline_mode=pl.Buffered(1))


def _in_proj_body(x_ref, gain_ref, w_ref, wba_ref, z_ref, ba_ref, u_scr):
    @pl.when(pl.program_id(1) == 0)
    def _():
        u = _rms_norm_bf16(x_ref[...], gain_ref[...])
        u_scr[...] = u
        ba_ref[...] = jnp.dot(u, wba_ref[...], preferred_element_type=F32)

    z_ref[...] = jnp.dot(u_scr[...], w_ref[...], preferred_element_type=F32)


def _in_proj(x2, gain, w, w_ba, *, tm, tn, col0, width):
    n = x2.shape[0]
    return pl.pallas_call(
        _in_proj_body,
        grid=(n // tm, width // tn),
        in_specs=[
            pl.BlockSpec((tm, D_MODEL), lambda i, j: (i, 0)),
            pl.BlockSpec((1, D_MODEL), lambda i, j: (0, 0)),
            pl.BlockSpec((D_MODEL, tn), lambda i, j: (0, col0 + j)),
            pl.BlockSpec((D_MODEL, LANES), lambda i, j: (0, BA_COL_BLOCK)),
        ],
        out_specs=[
            pl.BlockSpec((tm, tn), lambda i, j: (i, j)),
            pl.BlockSpec((tm, LANES), lambda i, j: (i, 0)),
        ],
        out_shape=[
            jax.ShapeDtypeStruct((n, width), F32),
            jax.ShapeDtypeStruct((n, LANES), F32),
        ],
        scratch_shapes=[pltpu.VMEM((tm, D_MODEL), BF16)],
        compiler_params=pltpu.CompilerParams(
            dimension_semantics=("arbitrary", "arbitrary"), vmem_limit_bytes=VMEM_LIMIT),
        name="in_proj",
    )(x2, gain, w, w_ba)


def _init_conv_tail(tail_ref, cs_ref):
    tail_ref[...] = jnp.zeros(tail_ref.shape, tail_ref.dtype)
    tail_ref[:, 5:8, :] = cs_ref[...]


def _causal_conv(cur2, cw_ref, tail_ref, *, bt, tt):
    rows, width = cur2.shape
    n_v = tt // SUBLANES
    assert bt == 1 or n_v == 1
    cur = cur2.reshape(bt * n_v, SUBLANES, width)
    tail = tail_ref[...]
    tail_ref[...] = cur[n_v - 1:] if bt == 1 else cur
    sub = lax.broadcasted_iota(jnp.int32, cur.shape, 1)
    tap = lambda i: cw_ref[i:i + 1, :].reshape(1, 1, width)
    y = cur * tap(CONV_W - 1)
    for k in range(1, CONV_W):
        rot = pltpu.roll(cur, k, axis=1)
        before = pltpu.roll(tail, k, axis=1)
        if n_v > 1:
            before = jnp.concatenate([before, rot[:-1]], axis=0)
        y = y + jnp.where(sub < k, before, rot) * tap(CONV_W - 1 - k)
    return y.reshape(rows, width)


def _lru_stages(z_lru, gate, j, cw_ref, cb_ref, wax_ref, ba_ref, bx_ref, alog_ref, out_ref,
                xp_scr, a_scr, b_scr, h_scr, *, bt, tt, reset_first):
    rows = bt * tt
    width = LRU_WIDTH
    n_vreg_rows = rows // SUBLANES
    n_groups = tt // SUBLANES
    env = {}

    def conv():
        env["xc"] = _causal_conv(z_lru(), cw_ref, xp_scr, bt=bt, tt=tt) + cb_ref[...]
        a_l = alog_ref[...]
        env["log_sig_a"] = jnp.minimum(a_l, 0.0) - jnp.log1p(jnp.exp(-jnp.abs(a_l)))
        env["sub3"] = lax.broadcasted_iota(jnp.int32, (n_vreg_rows, SUBLANES, LRU_BLOCK), 1)
        if reset_first:
            row = lax.broadcasted_iota(jnp.int32, (rows, LRU_BLOCK), 0)
            env["is_reset"] = jnp.logical_and(row % tt == 0, j == 0)

    def block(g):
        gs = slice(g * LRU_BLOCK, (g + 1) * LRU_BLOCK)
        xg = env["xc"][:, gs]
        pre = jnp.dot(xg.astype(BF16), wax_ref[g], preferred_element_type=F32)
        r = _sigmoid(pre[:, :LRU_BLOCK] + ba_ref[:, gs])
        ig = _sigmoid(pre[:, LRU_BLOCK:] + bx_ref[:, gs])
        log_a = (LRU_C * r) * env["log_sig_a"][:, gs]
        a = jnp.exp(log_a)
        t = jnp.tanh(log_a)
        m2 = (-2.0 * t) / (1.0 - t)
        mult = jnp.where(m2 > 0.0, m2 * lax.rsqrt(m2), 0.0)
        if reset_first:
            a = jnp.where(env["is_reset"], 0.0, a)
            mult = jnp.where(env["is_reset"], 1.0, mult)
        b = mult * ig * xg
        a = a.reshape(n_vreg_rows, SUBLANES, LRU_BLOCK)
        b = b.reshape(n_vreg_rows, SUBLANES, LRU_BLOCK)
        s = 1
        while s < SUBLANES:
            keep = env["sub3"] >= s
            a_sh = jnp.where(keep, pltpu.roll(a, s, axis=1), 1.0)
            b_sh = jnp.where(keep, pltpu.roll(b, s, axis=1), 0.0)
            b = a * b_sh + b
            a = a * a_sh
            s *= 2
        a_scr[:, :, gs] = a.reshape(bt, tt, LRU_BLOCK)
        b_scr[:, :, gs] = b.reshape(bt, tt, LRU_BLOCK)

    def carry_groups(g0, g1):
        carry = env.get("carry")
        if carry is None:
            carry = h_scr[...]
        for i in range(g0, g1):
            rs = slice(i * SUBLANES, (i + 1) * SUBLANES)
            h = b_scr[:, rs, :] + a_scr[:, rs, :] * carry
            b_scr[:, rs, :] = h
            carry = h[:, SUBLANES - 1:SUBLANES, :]
        env["carry"] = carry
        if g1 == n_groups:
            h_scr[...] = carry

    def finish():
        h = b_scr[...].reshape(rows, width)
        out_ref[...] = (h * _silu(gate())).astype(out_ref.dtype)

    per_piece = min(n_groups, LRU_CARRY_PIECE)
    pieces = [functools.partial(carry_groups, g0, min(g0 + per_piece, n_groups))
              for g0 in range(0, n_groups, per_piece)]
    return [conv] + [functools.partial(block, g) for g in range(LRU_BLOCKS)] + pieces + [finish]


def _lru_body(x_ref, gain_ref, w_ref, cs_ref, h0_ref, cw_ref, cb_ref, wax_ref, ba_ref, bx_ref, alog_ref,
              out_ref, hout_ref, csout_ref, xp_scr, a_scr, b_scr, h_scr, *, bt, tt, reset_first):
    j = pl.program_id(1)

    @pl.when(j == 0)
    def _():
        _init_conv_tail(xp_scr, cs_ref)
        h_scr[...] = h0_ref[...]

    u = _rms_norm_bf16(x_ref[...], gain_ref[...])
    z = jnp.dot(u, w_ref[...], preferred_element_type=F32)
    for stage in _lru_stages(lambda: z[:, :LRU_WIDTH], lambda: z[:, LRU_WIDTH:], j,
                             cw_ref, cb_ref, wax_ref, ba_ref, bx_ref, alog_ref, out_ref, xp_scr, a_scr, b_scr, h_scr, bt=bt, tt=tt,
                             reset_first=reset_first):
        stage()

    @pl.when(j == pl.num_programs(1) - 1)
    def _():
        hout_ref[...] = h_scr[...]
        csout_ref[...] = xp_scr[:, 5:8, :]


def _lru_scratch(bt, tt):
    return [
        pltpu.VMEM((bt, SUBLANES, LRU_WIDTH), F32),
        pltpu.VMEM((bt, tt, LRU_WIDTH), F32),
        pltpu.VMEM((bt, tt, LRU_WIDTH), F32),
        pltpu.VMEM((bt, 1, LRU_WIDTH), F32),
    ]


def _lru(x2, gain, w_lru, conv_state, h0, cw, cb, wax, ba, bx, alog, *, nb, t, bt, tt, reset_first):
    nt = t // tt
    rows = bt * tt
    body = functools.partial(_lru_body, bt=bt, tt=tt, reset_first=reset_first)
    vec = lambda: pl.BlockSpec((1, LRU_WIDTH), lambda i, j: (0, 0))
    return pl.pallas_call(
        body,
        grid=(nb // bt, nt),
        in_specs=[
            pl.BlockSpec((rows, D_MODEL), lambda i, j: (i * nt + j, 0)),
            vec(),
            pl.BlockSpec((D_MODEL, LRU_Z_WIDTH), lambda i, j: (0, 0)),
            pl.BlockSpec((bt, CONV_W - 1, LRU_WIDTH), lambda i, j: (i, 0, 0)),
            pl.BlockSpec((bt, 1, LRU_WIDTH), lambda i, j: (i, 0, 0)),
            pl.BlockSpec((CONV_W, LRU_WIDTH), lambda i, j: (0, 0)),
            vec(),
            pl.BlockSpec((LRU_BLOCKS, LRU_BLOCK, 2 * LRU_BLOCK), lambda i, j: (0, 0, 0)),
            vec(), vec(), vec(),
        ],
        out_specs=[
            pl.BlockSpec((rows, LRU_WIDTH), lambda i, j: (i * nt + j, 0)),
            pl.BlockSpec((bt, 1, LRU_WIDTH), lambda i, j: (i, 0, 0)),
            pl.BlockSpec((bt, CONV_W - 1, LRU_WIDTH), lambda i, j: (i, 0, 0)),
        ],
        out_shape=[
            jax.ShapeDtypeStruct((nb * t, LRU_WIDTH), BF16),
            jax.ShapeDtypeStruct((nb, 1, LRU_WIDTH), F32),
            jax.ShapeDtypeStruct((nb, CONV_W - 1, LRU_WIDTH), F32),
        ],
        scratch_shapes=_lru_scratch(bt, tt),
        compiler_params=pltpu.CompilerParams(
            dimension_semantics=("arbitrary", "arbitrary"), vmem_limit_bytes=VMEM_LIMIT),
        name="lru",
    )(x2, gain, w_lru, conv_state, h0, cw, cb, wax, ba, bx, alog)


def _gdn_core(zq, zk, zv, zgate, ba, cw_refs, alog_ref, dtb_ref, nw_ref, out_ref, xp_scr, s_scr,
              h_grp, *, bt, tt, chunk, hg, seq_group, fill=lambda: None):
    rows = bt * tt
    width = LANES * hg
    n_chunks = tt // chunk

    def conv_silu(idx, cur2, cw_ref):
        return _silu(_causal_conv(cur2, cw_ref, xp_scr.at[idx], bt=bt, tt=tt))

    q_all = conv_silu(0, zq(), cw_refs[0])
    k_all = conv_silu(1, zk(), cw_refs[1])
    v_all = conv_silu(2, zv(), cw_refs[2])
    ba = ba()

    beta_all = _sigmoid(ba)
    g_all = -jnp.exp(alog_ref[...]) * _softplus(ba + dtb_ref[...])
    row = lax.broadcasted_iota(jnp.int32, (rows, LANES), 0)
    row_in_chunk = row % chunk
    gc_all = g_all
    s = 1
    while s < chunk:
        gc_all = gc_all + jnp.where(row_in_chunk >= s, pltpu.roll(gc_all, s, axis=0), 0.0)
        s *= 2
    gct_all = gc_all.T

    lane = lax.broadcasted_iota(jnp.int32, (rows, LANES), 1)
    sub = lax.broadcasted_iota(jnp.int32, (LANES, rows), 0)
    ri = lax.broadcasted_iota(jnp.int32, (rows, rows), 0)
    ci = lax.broadcasted_iota(jnp.int32, (rows, rows), 1)
    same_chunk = (ri // chunk) == (ci // chunk)
    causal = jnp.logical_and(same_chunk, ri >= ci)
    strict = jnp.logical_and(same_chunk, ri > ci)
    n_steps = (chunk - 1).bit_length()

    hd = []
    for hh in range(hg):
        head = h_grp * hg + hh
        ls = slice(hh * LANES, (hh + 1) * LANES)
        qh, kh, vh = q_all[:, ls], k_all[:, ls], v_all[:, ls]
        qh = qh * lax.rsqrt(jnp.sum(qh * qh, axis=-1, keepdims=True) + NORM_EPS) * (GDN_DK ** -0.5)
        kh = kh * lax.rsqrt(jnp.sum(kh * kh, axis=-1, keepdims=True) + NORM_EPS)

        beta = jnp.sum(jnp.where(lane == BETA_LANE + head, beta_all, 0.0), axis=1, keepdims=True)
        gc = jnp.sum(jnp.where(lane == ALPHA_LANE + head, gc_all, 0.0), axis=1, keepdims=True)
        gc_row = jnp.sum(jnp.where(sub == ALPHA_LANE + head, gct_all, 0.0), axis=0, keepdims=True)

        decay = jnp.where(causal, jnp.exp(jnp.where(causal, gc - gc_row, 0.0)), 0.0)
        kb = kh * beta
        gram = _fdot(jnp.concatenate([kb, qh], axis=0), kh, ((1,), (1,)))
        e_gc = jnp.exp(gc)
        hd.append(dict(
            ls=ls, kh=kh, gc=gc,
            p=jnp.where(strict, -(gram[:rows] * decay), 0.0),
            a_qk=gram[rows:] * decay,
            x=jnp.concatenate([vh * beta, kb * e_gc], axis=1),
            q_dec=qh * e_gc))

    n_dots = 0
    for st in range(n_steps):
        for d in hd:
            if st == 0:
                d["n"] = d["p"]
                if n_steps > 1:
                    d["p"] = _fdot(d["p"], d["p"])
            elif st < n_steps - 1:
                res = _fdot(jnp.concatenate([d["n"], d["p"]], axis=0), d["p"])
                d["n"] = d["n"] + d["p"] + res[:rows]
                d["p"] = res[rows:]
            else:
                d["n"] = d["n"] + d["p"] + _fdot(d["n"], d["p"])
            n_dots += 1
            if n_dots % FILL_EVERY == 0:
                fill()
    for d in hd:
        d["x"] = d["x"] + _fdot(d["n"], d["x"])

    for d in hd:
        d["v_new"] = {}
        d["qs"] = {}
    for sq0 in range(0, bt, seq_group):
        pairs = [(sq, hh) for sq in range(sq0, min(sq0 + seq_group, bt)) for hh in range(hg)]
        state = {pr: s_scr[pr[0], pr[1]] for pr in pairs}
        for c in range(n_chunks):
            rsl = {sq: slice(sq * tt + c * chunk, sq * tt + (c + 1) * chunk) for sq, _ in pairs}
            ws, v_new, k_dec, e_last, upd = {}, {}, {}, {}, {}
            for pr in pairs:
                d, rs = hd[pr[1]], rsl[pr[0]]
                ws[pr] = _fdot(jnp.concatenate([d["x"][rs, LANES:], d["q_dec"][rs]], axis=0), state[pr])
            for pr in pairs:
                d, rs = hd[pr[1]], rsl[pr[0]]
                v_new[pr] = d["x"][rs, :LANES] - ws[pr][:chunk]
                g_last = d["gc"][rs.stop - 1:rs.stop]
                k_dec[pr] = d["kh"][rs] * jnp.exp(g_last - d["gc"][rs])
                e_last[pr] = jnp.exp(g_last)
                d["v_new"][(pr[0], c)] = v_new[pr]
                d["qs"][(pr[0], c)] = ws[pr][chunk:]
            for pr in pairs:
                upd[pr] = _fdot(k_dec[pr], v_new[pr], ((0,), (0,)))
            for pr in pairs:
                state[pr] = state[pr] * e_last[pr] + upd[pr]
            fill()
        for pr in pairs:
            s_scr[pr[0], pr[1]] = state[pr]

    order = [(sq, c) for sq in range(bt) for c in range(n_chunks)]
    gate = zgate()
    for d in hd:
        v_parts = [d["v_new"][k] for k in order]
        q_parts = [d["qs"][k] for k in order]
        v_new_all = jnp.concatenate(v_parts, axis=0) if len(v_parts) > 1 else v_parts[0]
        qs_all = jnp.concatenate(q_parts, axis=0) if len(q_parts) > 1 else q_parts[0]
        o = qs_all + _fdot(d["a_qk"], v_new_all)
        o = o * lax.rsqrt(jnp.mean(o * o, axis=-1, keepdims=True) + NORM_EPS) * nw_ref[...]
        o = o * _silu(gate[:, d["ls"]])
        out_ref[:, d["ls"]] = o.astype(out_ref.dtype)


def _gdn_init_state(j, cs_refs, s0_ref, xp_scr, s_scr):
    @pl.when(j == 0)
    def _():
        for idx, cs_ref in enumerate(cs_refs):
            _init_conv_tail(xp_scr.at[idx], cs_ref)
        s_scr[...] = s0_ref[...]


def _gdn_z_body(zq_ref, zk_ref, zv_ref, zg_ref, ba_ref, csq_ref, csk_ref, csv_ref, s0_ref,
                cwq_ref, cwk_ref, cwv_ref, alog_ref, dtb_ref, nw_ref,
                out_ref, sout_ref, xp_scr, s_scr, *, bt, tt, chunk, hg):
    j = pl.program_id(2)
    _gdn_init_state(j, (csq_ref, csk_ref, csv_ref), s0_ref, xp_scr, s_scr)
    _gdn_core(lambda: zq_ref[...], lambda: zk_ref[...], lambda: zv_ref[...], lambda: zg_ref[...],
              lambda: ba_ref[...],
              (cwq_ref, cwk_ref, cwv_ref), alog_ref, dtb_ref, nw_ref, out_ref, xp_scr, s_scr,
              pl.program_id(1), bt=bt, tt=tt, chunk=chunk, hg=hg, seq_group=GDN_SEQ_GROUP)

    @pl.when(j == pl.num_programs(2) - 1)
    def _():
        sout_ref[...] = s_scr[...]


def _gdn_z(z, ba, conv_state, s0, cw, alog_pad, dtb_pad, nw, *, nb, t, bt, tt, chunk, hg):
    nt = t // tt
    rows = bt * tt
    width = LANES * hg
    per = SEG // width
    body = functools.partial(_gdn_z_body, bt=bt, tt=tt, chunk=chunk, hg=hg)

    def zspec(col):
        return pl.BlockSpec((rows, width), lambda i, h, j: (i * nt + j, col * per + h))

    def cs_spec(seg):
        return pl.BlockSpec((bt, CONV_W - 1, width), lambda i, h, j: (i, 0, seg * per + h))

    def cw_spec(seg):
        return pl.BlockSpec((CONV_W, width), lambda i, h, j: (0, seg * per + h))

    vec = lambda: pl.BlockSpec((1, LANES), lambda i, h, j: (0, 0))
    return pl.pallas_call(
        body,
        grid=(nb // bt, GDN_HEADS // hg, nt),
        in_specs=[
            zspec(COL_Q), zspec(COL_K), zspec(COL_V), zspec(COL_GDN_GATE),
            pl.BlockSpec((rows, LANES), lambda i, h, j: (i * nt + j, 0)),
            cs_spec(0), cs_spec(1), cs_spec(2),
            pl.BlockSpec((bt, hg, GDN_DK, GDN_DV), lambda i, h, j: (i, h, 0, 0)),
            cw_spec(0), cw_spec(1), cw_spec(2),
            vec(), vec(), vec(),
        ],
        out_specs=[
            pl.BlockSpec((rows, width), lambda i, h, j: (i * nt + j, h)),
            pl.BlockSpec((bt, hg, GDN_DK, GDN_DV), lambda i, h, j: (i, h, 0, 0)),
        ],
        out_shape=[
            jax.ShapeDtypeStruct((nb * t, GDN_VAL_W), BF16),
            jax.ShapeDtypeStruct((nb, GDN_HEADS, GDN_DK, GDN_DV), F32),
        ],
        scratch_shapes=[
            pltpu.VMEM((3, bt, SUBLANES, width), F32),
            pltpu.VMEM((bt, hg, GDN_DK, GDN_DV), F32),
        ],
        compiler_params=pltpu.CompilerParams(
            dimension_semantics=("arbitrary", "arbitrary", "arbitrary"), vmem_limit_bytes=VMEM_LIMIT),
        name="gdn_z",
    )(z, z, z, z, ba, conv_state, conv_state, conv_state, s0, cw, cw, cw, alog_pad, dtb_pad, nw)


def _seq_body(x_ref, gain_ref, w_ref, wba_ref,
              lcs_ref, h0_ref, lcw_ref, lcb_ref, wax_ref, lba_ref, lbx_ref, lalog_ref,
              csq_ref, csk_ref, csv_ref, s0_ref, cwq_ref, cwk_ref, cwv_ref, galog_ref, dtb_ref, nw_ref,
              wm_ref, wl_ref, wg_ref, wo_ref, gpost_ref,
              y_ref, hout_ref, lcsout_ref, sout_ref, gcsout_ref,
              lxp_scr, a_scr, b_scr, h_scr, gxp_scr, s_scr, lru_out_ref, gdn_out_ref,
              *, bt, tt, chunk, reset_first):
    j = pl.program_id(1)

    @pl.when(j == 0)
    def _():
        _init_conv_tail(lxp_scr, lcs_ref)
        h_scr[...] = h0_ref[...]

    _gdn_init_state(j, (csq_ref, csk_ref, csv_ref), s0_ref, gxp_scr, s_scr)

    u = _rms_norm_bf16(x_ref[...], gain_ref[...])
    m = jnp.dot(u, wm_ref[...], preferred_element_type=F32)

    def seg(c):
        return lambda: jnp.dot(u, w_ref[:, c * SEG:(c + 1) * SEG], preferred_element_type=F32)

    ba = lambda: jnp.dot(u, wba_ref[...], preferred_element_type=F32)

    stages = iter(_lru_stages(seg(0), seg(1), j, lcw_ref, lcb_ref, wax_ref, lba_ref, lbx_ref, lalog_ref,
                              lru_out_ref, lxp_scr, a_scr, b_scr, h_scr, bt=bt, tt=tt,
                              reset_first=reset_first))

    def fill():
        stage = next(stages, None)
        if stage is not None:
            stage()

    _gdn_core(seg(2 + COL_Q), seg(2 + COL_K), seg(2 + COL_V), seg(2 + COL_GDN_GATE), ba,
              (cwq_ref, cwk_ref, cwv_ref), galog_ref, dtb_ref, nw_ref, gdn_out_ref, gxp_scr, s_scr,
              0, bt=bt, tt=tt, chunk=chunk, hg=GDN_HEADS, seq_group=1, fill=fill)
    for stage in stages:
        stage()
    y_ref[...] = _merge_and_project(x_ref[...], m, lru_out_ref[...], gdn_out_ref[...], wl_ref, wg_ref, wo_ref,
                                    gpost_ref)

    @pl.when(j == pl.num_programs(1) - 1)
    def _():
        hout_ref[...] = h_scr[...]
        lcsout_ref[...] = lxp_scr[:, 5:8, :]
        sout_ref[...] = s_scr[...]
        for idx in range(3):
            gcsout_ref[:, :, idx * SEG:(idx + 1) * SEG] = gxp_scr[idx, :, 5:8, :]


def _seq(x2, gain, w_seq, w_ba, lru_conv, h0, lcw, lcb, wax, lba, lbx, lalog,
         gdn_conv, s0, gcw, galog_pad, dtb_pad, nw, w_m, wl, wg, wo, gpost, *, nb, t, bt, tt, chunk, reset_first):
    nt = t // tt
    rows = bt * tt
    hg = GDN_HEADS
    body = functools.partial(_seq_body, bt=bt, tt=tt, chunk=chunk, reset_first=reset_first)
    const2 = lambda i, j: (0, 0)
    vec = lambda n: pl.BlockSpec((1, n), const2)
    cs_spec = lambda seg: pl.BlockSpec((bt, CONV_W - 1, SEG), lambda i, j: (i, 0, seg))
    cw_spec = lambda seg: pl.BlockSpec((CONV_W, SEG), lambda i, j: (0, seg))
    state_spec = pl.BlockSpec((bt, hg, GDN_DK, GDN_DV), lambda i, j: (i, 0, 0, 0))
    return pl.pallas_call(
        body,
        grid=(nb // bt, nt),
        in_specs=[
            pl.BlockSpec((rows, D_MODEL), lambda i, j: (i * nt + j, 0)),
            vec(D_MODEL),
            _resident((D_MODEL, LRU_Z_WIDTH + GDN_Z_WIDTH), const2),
            _resident((D_MODEL, LANES), lambda i, j: (0, BA_COL_BLOCK)),
            pl.BlockSpec((bt, CONV_W - 1, LRU_WIDTH), lambda i, j: (i, 0, 0)),
            pl.BlockSpec((bt, 1, LRU_WIDTH), lambda i, j: (i, 0, 0)),
            pl.BlockSpec((CONV_W, LRU_WIDTH), const2),
            vec(LRU_WIDTH),
            pl.BlockSpec((LRU_BLOCKS, LRU_BLOCK, 2 * LRU_BLOCK), lambda i, j: (0, 0, 0)),
            vec(LRU_WIDTH), vec(LRU_WIDTH), vec(LRU_WIDTH),
            cs_spec(0), cs_spec(1), cs_spec(2),
            state_spec,
            cw_spec(0), cw_spec(1), cw_spec(2),
            vec(LANES), vec(LANES), vec(LANES),
            _resident((D_MODEL, 2 * D_MODEL), const2),
            _resident((D_MODEL, D_MODEL), const2), _resident((D_MODEL, D_MODEL), const2),
            _resident((D_MODEL, D_MODEL), const2),
            vec(D_MODEL),
        ],
        out_specs=[
            pl.BlockSpec((rows, D_MODEL), lambda i, j: (i * nt + j, 0)),
            pl.BlockSpec((bt, 1, LRU_WIDTH), lambda i, j: (i, 0, 0)),
            pl.BlockSpec((bt, CONV_W - 1, LRU_WIDTH), lambda i, j: (i, 0, 0)),
            state_spec,
            pl.BlockSpec((bt, CONV_W - 1, GDN_QKV_W), lambda i, j: (i, 0, 0)),
        ],
        out_shape=[
            jax.ShapeDtypeStruct((nb * t, D_MODEL), F32),
            jax.ShapeDtypeStruct((nb, 1, LRU_WIDTH), F32),
            jax.ShapeDtypeStruct((nb, CONV_W - 1, LRU_WIDTH), F32),
            jax.ShapeDtypeStruct((nb, GDN_HEADS, GDN_DK, GDN_DV), F32),
            jax.ShapeDtypeStruct((nb, CONV_W - 1, GDN_QKV_W), F32),
        ],
        scratch_shapes=_lru_scratch(bt, tt) + [
            pltpu.VMEM((3, bt, SUBLANES, SEG), F32),
            pltpu.VMEM((bt, hg, GDN_DK, GDN_DV), F32),
            pltpu.VMEM((rows, LRU_WIDTH), BF16),
            pltpu.VMEM((rows, GDN_VAL_W), BF16),
        ],
        compiler_params=pltpu.CompilerParams(
            dimension_semantics=("arbitrary", "arbitrary"), vmem_limit_bytes=SEQ_VMEM_LIMIT),
        name="seq",
    )(x2, gain, w_seq, w_ba, lru_conv, h0, lcw, lcb, wax, lba, lbx, lalog,
      gdn_conv, gdn_conv, gdn_conv, s0, gcw, gcw, gcw, galog_pad, dtb_pad, nw, w_m, wl, wg, wo, gpost)


def _merge_and_project(x, m, lru_out, gdn_out, wl_ref, wg_ref, wo_ref, gpost_ref):
    p_lru = jnp.dot(lru_out, wl_ref[...], preferred_element_type=F32)
    p_gdn = jnp.dot(gdn_out, wg_ref[...], preferred_element_type=F32)
    merged = _sigmoid(m[:, :D_MODEL]) * p_lru + _sigmoid(m[:, D_MODEL:]) * p_gdn
    y = jnp.dot(merged.astype(BF16), wo_ref[...], preferred_element_type=F32)
    ms = jnp.mean(y * y, axis=-1, keepdims=True)
    return x + (y * lax.rsqrt(ms + NORM_EPS)) * gpost_ref[...]


def _out_body(x_ref, gpre_ref, wm_ref, lru_ref, gdn_ref, wl_ref, wg_ref, wo_ref, gpost_ref, y_ref):
    x = x_ref[...]
    u = _rms_norm_bf16(x, gpre_ref[...])
    m = jnp.dot(u, wm_ref[...], preferred_element_type=F32)
    y_ref[...] = _merge_and_project(x, m, lru_ref[...], gdn_ref[...], wl_ref, wg_ref, wo_ref, gpost_ref)


def _out_proj(x2, gpre, w_m, lru_out, gdn_out, wl, wg, wo, gpost, *, tm):
    n = x2.shape[0]
    wspec = lambda: pl.BlockSpec((D_MODEL, D_MODEL), lambda i: (0, 0))
    vec = lambda: pl.BlockSpec((1, D_MODEL), lambda i: (0, 0))
    rowspec = lambda: pl.BlockSpec((tm, D_MODEL), lambda i: (i, 0))
    return pl.pallas_call(
        _out_body,
        grid=(n // tm,),
        in_specs=[
            rowspec(), vec(),
            pl.BlockSpec((D_MODEL, 2 * D_MODEL), lambda i: (0, 0)),
            rowspec(), rowspec(),
            wspec(), wspec(), wspec(),
            vec(),
        ],
        out_specs=rowspec(),
        out_shape=jax.ShapeDtypeStruct((n, D_MODEL), F32),
        compiler_params=pltpu.CompilerParams(
            dimension_semantics=("arbitrary",), vmem_limit_bytes=VMEM_LIMIT),
        name="out_proj",
    )(x2, gpre, w_m, lru_out, gdn_out, wl, wg, wo, gpost)


def _prep_weights(norm_pre, norm_post, w_in, lru_conv_w, lru_conv_b, lru_wa, lru_ba, lru_wx, lru_bx,
                  lru_a_logit, gdn_conv_w, gdn_A_log, gdn_dt_bias, gdn_norm_w, w_br_lru, w_br_gdn, w_out):
    c_m = BA_COL_BLOCK * LANES + 2 * GDN_HEADS
    wax = jnp.concatenate([lru_wa, lru_wx], axis=-1).astype(BF16)
    pad_alpha = lambda v: jnp.pad(v.reshape(1, GDN_HEADS), ((0, 0), (ALPHA_LANE, LANES - ALPHA_LANE - GDN_HEADS)))
    row = lambda v: v.reshape(1, -1)
    w_all = w_in.astype(BF16)
    return dict(
        norm_pre=row(norm_pre), norm_post=row(norm_post),
        w_all=w_all, w_m=w_all[:, c_m:],
        lru_cw=lru_conv_w, lru_cb=row(lru_conv_b), wax=wax, lru_ba=row(lru_ba), lru_bx=row(lru_bx),
        lru_alog=row(lru_a_logit), gdn_cw=gdn_conv_w, alog_pad=pad_alpha(gdn_A_log),
        dtb_pad=pad_alpha(gdn_dt_bias), gdn_nw=row(gdn_norm_w),
        wl=w_br_lru.astype(BF16), wg=w_br_gdn.astype(BF16), wo=w_out.astype(BF16))


def _layer(x, lru_conv, lru_h, gdn_conv, gdn_s, p, *, reset_first, fused, tm, lru_bt=None, lru_tt=None,
           gdn_bt=None, gdn_tt=None):
    nb, t, _ = x.shape
    x2 = x.reshape(nb * t, D_MODEL)
    h0 = lru_h.reshape(nb, 1, LRU_WIDTH)
    chunk = min(GDN_CHUNK, t)
    lru_w = (p["lru_cw"], p["lru_cb"], p["wax"], p["lru_ba"], p["lru_bx"], p["lru_alog"])
    gdn_w = (p["gdn_cw"], p["alog_pad"], p["dtb_pad"], p["gdn_nw"])
    out_w = (p["w_m"], p["wl"], p["wg"], p["wo"], p["norm_post"])
    if fused:
        y, h_last, lru_conv_new, s_new, gdn_conv_new = _seq(
            x2, p["norm_pre"], p["w_all"], p["w_all"], lru_conv, h0, *lru_w, gdn_conv, gdn_s, *gdn_w, *out_w,
            nb=nb, t=t, bt=1, tt=tm, chunk=chunk, reset_first=reset_first)
    else:
        lru_out, h_last, lru_conv_new = _lru(
            x2, p["norm_pre"], p["w_all"], lru_conv, h0, *lru_w, nb=nb, t=t, bt=lru_bt, tt=lru_tt,
            reset_first=reset_first)
        z, ba = _in_proj(x2, p["norm_pre"], p["w_all"], p["w_all"], tm=nb * t, tn=SEG,
                         col0=LRU_Z_WIDTH // SEG, width=GDN_Z_WIDTH)
        gdn_out, s_new = _gdn_z(z, ba, gdn_conv, gdn_s, *gdn_w, nb=nb, t=t, bt=gdn_bt, tt=gdn_tt, chunk=chunk,
                                hg=GDN_HEADS)
        gdn_conv_new = z.reshape(nb, t, GDN_Z_WIDTH)[:, t - (CONV_W - 1):, :GDN_QKV_W]
        y = _out_proj(x2, p["norm_pre"], p["w_m"], lru_out, gdn_out, p["wl"], p["wg"], p["wo"], p["norm_post"],
                      tm=min(512, nb * t))
    return y.reshape(nb, t, D_MODEL), lru_conv_new, h_last.reshape(nb, LRU_WIDTH), gdn_conv_new, s_new


def kernel(x_prompt, x_sample, state_lru_conv, state_lru_h, state_gdn_conv, state_gdn_S, norm_pre, norm_post, w_in, lru_conv_w, lru_conv_b, lru_wa, lru_ba, lru_wx, lru_bx, lru_a_logit, gdn_conv_w, gdn_A_log, gdn_dt_bias, gdn_norm_w, w_br_lru, w_br_gdn, w_out):
    depth = w_in.shape[0]
    assert depth == 1
    nb = x_prompt.shape[0]
    p = _prep_weights(norm_pre[0], norm_post[0], w_in[0], lru_conv_w[0], lru_conv_b[0], lru_wa[0], lru_ba[0],
                      lru_wx[0], lru_bx[0], lru_a_logit[0], gdn_conv_w[0], gdn_A_log[0], gdn_dt_bias[0],
                      gdn_norm_w[0], w_br_lru[0], w_br_gdn[0], w_out[0])
    yp, p_lc, p_lh, p_gc, p_gs = _layer(
        x_prompt,
        jnp.zeros((nb, CONV_W - 1, LRU_WIDTH), F32), jnp.zeros((nb, LRU_WIDTH), F32),
        jnp.zeros((nb, CONV_W - 1, GDN_QKV_W), F32), jnp.zeros((nb, GDN_HEADS, GDN_DK, GDN_DV), F32),
        p, reset_first=True, fused=True, tm=SEQ_TILE)
    ys, s_lc, s_lh, s_gc, s_gs = _layer(
        x_sample, state_lru_conv[0], state_lru_h[0], state_gdn_conv[0], state_gdn_S[0],
        p, reset_first=False, fused=False, tm=1024, lru_bt=32, lru_tt=8, gdn_bt=8, gdn_tt=8)
    return (yp, ys, p_lc[None], p_lh[None], p_gc[None], p_gs[None],
            s_lc[None], s_lh[None], s_gc[None], s_gs[None])
```

```python
import functools

import jax
import jax.numpy as jnp
from jax import lax
from jax.experimental import pallas as pl
from jax.experimental.pallas import tpu as pltpu

F32 = jnp.float32
BF16 = jnp.bfloat16

D_MODEL = 1024
CONV_W = 4
LRU_WIDTH = 1024
LRU_BLOCKS = 8
LRU_BLOCK = LRU_WIDTH // LRU_BLOCKS
LRU_C = 8.0
GDN_HEADS = 8
GDN_DK = 128
GDN_DV = 128
GDN_KEY_W = GDN_HEADS * GDN_DK
GDN_VAL_W = GDN_HEADS * GDN_DV
GDN_QKV_W = 2 * GDN_KEY_W + GDN_VAL_W
GDN_CHUNK = 64
NORM_EPS = 1e-6

LANES = 128
SUBLANES = 8
SEG = 1024
LRU_Z_WIDTH = 2 * SEG
GDN_Z_WIDTH = 4 * SEG
COL_Q, COL_K, COL_V, COL_GDN_GATE = range(4)
BETA_LANE = 0
ALPHA_LANE = GDN_HEADS
BA_COL_BLOCK = (LRU_Z_WIDTH + GDN_Z_WIDTH) // LANES
VMEM_LIMIT = 48 * 1024 * 1024
SEQ_VMEM_LIMIT = 56 * 1024 * 1024
SEQ_TILE = 128
GDN_SEQ_GROUP = 2
LRU_CARRY_PIECE = 8
FILL_EVERY = 6


def _softplus(x):
    return jnp.maximum(x, 0.0) + jnp.log1p(jnp.exp(-jnp.abs(x)))


def _sigmoid(x):
    return jax.nn.sigmoid(x)


def _silu(x):
    h = 0.5 * x
    return h + h * jnp.tanh(h)


def _fdot(a, b, dims=((1,), (0,))):
    return lax.dot_general(a, b, (dims, ((), ())), preferred_element_type=F32)


def _rms_norm_bf16(x, gain):
    ms = jnp.mean(x * x, axis=-1, keepdims=True)
    return ((x * lax.rsqrt(ms + NORM_EPS)) * gain).astype(BF16)


def _resident(shape, index_map):
    return pl.BlockSpec(shape, index_map, pipeline_mode=pl.Buffered(1))


def _in_proj_body(x_ref, gain_ref, w_ref, wba_ref, z_ref, ba_ref, u_scr):
    @pl.when(pl.program_id(1) == 0)
    def _():
        u = _rms_norm_bf16(x_ref[...], gain_ref[...])
        u_scr[...] = u
        ba_ref[...] = jnp.dot(u, wba_ref[...], preferred_element_type=F32)

    z_ref[...] = jnp.dot(u_scr[...], w_ref[...], preferred_element_type=F32)


def _in_proj(x2, gain, w, w_ba, *, tm, tn, col0, width):
    n = x2.shape[0]
    return pl.pallas_call(
        _in_proj_body,
        grid=(n // tm, width // tn),
        in_specs=[
            pl.BlockSpec((tm, D_MODEL), lambda i, j: (i, 0)),
            pl.BlockSpec((1, D_MODEL), lambda i, j: (0, 0)),
            pl.BlockSpec((D_MODEL, tn), lambda i, j: (0, col0 + j)),
            pl.BlockSpec((D_MODEL, LANES), lambda i, j: (0, BA_COL_BLOCK)),
        ],
        out_specs=[
            pl.BlockSpec((tm, tn), lambda i, j: (i, j)),
            pl.BlockSpec((tm, LANES), lambda i, j: (i, 0)),
        ],
        out_shape=[
            jax.ShapeDtypeStruct((n, width), F32),
            jax.ShapeDtypeStruct((n, LANES), F32),
        ],
        scratch_shapes=[pltpu.VMEM((tm, D_MODEL), BF16)],
        compiler_params=pltpu.CompilerParams(
            dimension_semantics=("arbitrary", "arbitrary"), vmem_limit_bytes=VMEM_LIMIT),
        name="in_proj",
    )(x2, gain, w, w_ba)


def _lru_stages(z_lru, gate, j, cw_ref, cb_ref, wax_ref, ba_ref, bx_ref, alog_ref, out_ref,
                xp_scr, a_scr, b_scr, h_scr, *, bt, tt, reset_first):
    rows = bt * tt
    width = LRU_WIDTH
    n_vreg_rows = rows // SUBLANES
    n_groups = tt // SUBLANES
    env = {}

    def conv():
        cur = z_lru().reshape(bt, tt, width)
        xp_scr[:, 8:8 + tt, :] = cur
        cw = cw_ref[...]
        xc = cur * cw[CONV_W - 1].reshape(1, 1, width)
        for i in range(CONV_W - 2, -1, -1):
            xc = xc + xp_scr[:, 5 + i:5 + i + tt, :] * cw[i].reshape(1, 1, width)
        xp_scr[:, 5:8, :] = xp_scr[:, 5 + tt:8 + tt, :]
        env["xc"] = (xc + cb_ref[...].reshape(1, 1, width)).reshape(rows, width)
        a_l = alog_ref[...]
        env["log_sig_a"] = jnp.minimum(a_l, 0.0) - jnp.log1p(jnp.exp(-jnp.abs(a_l)))
        env["sub3"] = lax.broadcasted_iota(jnp.int32, (n_vreg_rows, SUBLANES, LRU_BLOCK), 1)
        if reset_first:
            row = lax.broadcasted_iota(jnp.int32, (rows, LRU_BLOCK), 0)
            env["is_reset"] = jnp.logical_and(row % tt == 0, j == 0)

    def block(g):
        gs = slice(g * LRU_BLOCK, (g + 1) * LRU_BLOCK)
        xg = env["xc"][:, gs]
        pre = jnp.dot(xg.astype(BF16), wax_ref[g], preferred_element_type=F32)
        r = _sigmoid(pre[:, :LRU_BLOCK] + ba_ref[:, gs])
        ig = _sigmoid(pre[:, LRU_BLOCK:] + bx_ref[:, gs])
        log_a = (LRU_C * r) * env["log_sig_a"][:, gs]
        a = jnp.exp(log_a)
        t = jnp.tanh(log_a)
        m2 = (-2.0 * t) / (1.0 - t)
        mult = jnp.where(m2 > 0.0, m2 * lax.rsqrt(m2), 0.0)
        if reset_first:
            a = jnp.where(env["is_reset"], 0.0, a)
            mult = jnp.where(env["is_reset"], 1.0, mult)
        b = mult * ig * xg
        a = a.reshape(n_vreg_rows, SUBLANES, LRU_BLOCK)
        b = b.reshape(n_vreg_rows, SUBLANES, LRU_BLOCK)
        s = 1
        while s < SUBLANES:
            keep = env["sub3"] >= s
            a_sh = jnp.where(keep, pltpu.roll(a, s, axis=1), 1.0)
            b_sh = jnp.where(keep, pltpu.roll(b, s, axis=1), 0.0)
            b = a * b_sh + b
            a = a * a_sh
            s *= 2
        a_scr[:, :, gs] = a.reshape(bt, tt, LRU_BLOCK)
        b_scr[:, :, gs] = b.reshape(bt, tt, LRU_BLOCK)

    def carry_groups(g0, g1):
        carry = env.get("carry")
        if carry is None:
            carry = h_scr[...]
        for i in range(g0, g1):
            rs = slice(i * SUBLANES, (i + 1) * SUBLANES)
            h = b_scr[:, rs, :] + a_scr[:, rs, :] * carry
            b_scr[:, rs, :] = h
            carry = h[:, SUBLANES - 1:SUBLANES, :]
        env["carry"] = carry
        if g1 == n_groups:
            h_scr[...] = carry

    def finish():
        h = b_scr[...].reshape(rows, width)
        out_ref[...] = (h * _silu(gate())).astype(out_ref.dtype)

    per_piece = min(n_groups, LRU_CARRY_PIECE)
    pieces = [functools.partial(carry_groups, g0, min(g0 + per_piece, n_groups))
              for g0 in range(0, n_groups, per_piece)]
    return [conv] + [functools.partial(block, g) for g in range(LRU_BLOCKS)] + pieces + [finish]


def _lru_body(x_ref, gain_ref, w_ref, cs_ref, h0_ref, cw_ref, cb_ref, wax_ref, ba_ref, bx_ref, alog_ref,
              out_ref, hout_ref, csout_ref, xp_scr, a_scr, b_scr, h_scr, *, bt, tt, reset_first):
    j = pl.program_id(1)

    @pl.when(j == 0)
    def _():
        xp_scr[:, 5:8, :] = cs_ref[...]
        h_scr[...] = h0_ref[...]

    u = _rms_norm_bf16(x_ref[...], gain_ref[...])
    z = jnp.dot(u, w_ref[...], preferred_element_type=F32)
    for stage in _lru_stages(lambda: z[:, :LRU_WIDTH], lambda: z[:, LRU_WIDTH:], j,
                             cw_ref, cb_ref, wax_ref, ba_ref, bx_ref, alog_ref, out_ref, xp_scr, a_scr, b_scr, h_scr, bt=bt, tt=tt,
                             reset_first=reset_first):
        stage()

    @pl.when(j == pl.num_programs(1) - 1)
    def _():
        hout_ref[...] = h_scr[...]
        csout_ref[...] = xp_scr[:, 5:8, :]


def _lru_scratch(bt, tt):
    return [
        pltpu.VMEM((bt, SUBLANES + tt, LRU_WIDTH), F32),
        pltpu.VMEM((bt, tt, LRU_WIDTH), F32),
        pltpu.VMEM((bt, tt, LRU_WIDTH), F32),
        pltpu.VMEM((bt, 1, LRU_WIDTH), F32),
    ]


def _lru(x2, gain, w_lru, conv_state, h0, cw, cb, wax, ba, bx, alog, *, nb, t, bt, tt, reset_first):
    nt = t // tt
    rows = bt * tt
    body = functools.partial(_lru_body, bt=bt, tt=tt, reset_first=reset_first)
    vec = lambda: pl.BlockSpec((1, LRU_WIDTH), lambda i, j: (0, 0))
    return pl.pallas_call(
        body,
        grid=(nb // bt, nt),
        in_specs=[
            pl.BlockSpec((rows, D_MODEL), lambda i, j: (i * nt + j, 0)),
            vec(),
            pl.BlockSpec((D_MODEL, LRU_Z_WIDTH), lambda i, j: (0, 0)),
            pl.BlockSpec((bt, CONV_W - 1, LRU_WIDTH), lambda i, j: (i, 0, 0)),
            pl.BlockSpec((bt, 1, LRU_WIDTH), lambda i, j: (i, 0, 0)),
            pl.BlockSpec((CONV_W, LRU_WIDTH), lambda i, j: (0, 0)),
            vec(),
            pl.BlockSpec((LRU_BLOCKS, LRU_BLOCK, 2 * LRU_BLOCK), lambda i, j: (0, 0, 0)),
            vec(), vec(), vec(),
        ],
        out_specs=[
            pl.BlockSpec((rows, LRU_WIDTH), lambda i, j: (i * nt + j, 0)),
            pl.BlockSpec((bt, 1, LRU_WIDTH), lambda i, j: (i, 0, 0)),
            pl.BlockSpec((bt, CONV_W - 1, LRU_WIDTH), lambda i, j: (i, 0, 0)),
        ],
        out_shape=[
            jax.ShapeDtypeStruct((nb * t, LRU_WIDTH), BF16),
            jax.ShapeDtypeStruct((nb, 1, LRU_WIDTH), F32),
            jax.ShapeDtypeStruct((nb, CONV_W - 1, LRU_WIDTH), F32),
        ],
        scratch_shapes=_lru_scratch(bt, tt),
        compiler_params=pltpu.CompilerParams(
            dimension_semantics=("arbitrary", "arbitrary"), vmem_limit_bytes=VMEM_LIMIT),
        name="lru",
    )(x2, gain, w_lru, conv_state, h0, cw, cb, wax, ba, bx, alog)


def _gdn_core(zq, zk, zv, zgate, ba, cw_refs, alog_ref, dtb_ref, nw_ref, out_ref, xp_scr, s_scr,
              h_grp, *, bt, tt, chunk, hg, seq_group, fill=lambda: None):
    rows = bt * tt
    width = LANES * hg
    n_chunks = tt // chunk

    def conv_silu(idx, cur2, cw_ref):
        cur = cur2.reshape(bt, tt, width)
        xp_scr[idx, :, 8:8 + tt, :] = cur
        cw = cw_ref[...]
        y = cur * cw[CONV_W - 1].reshape(1, 1, width)
        for i in range(CONV_W - 2, -1, -1):
            y = y + xp_scr[idx, :, 5 + i:5 + i + tt, :] * cw[i].reshape(1, 1, width)
        xp_scr[idx, :, 5:8, :] = xp_scr[idx, :, 5 + tt:8 + tt, :]
        return _silu(y.reshape(rows, width))

    q_all = conv_silu(0, zq(), cw_refs[0])
    k_all = conv_silu(1, zk(), cw_refs[1])
    v_all = conv_silu(2, zv(), cw_refs[2])
    ba = ba()

    beta_all = _sigmoid(ba)
    g_all = -jnp.exp(alog_ref[...]) * _softplus(ba + dtb_ref[...])
    row = lax.broadcasted_iota(jnp.int32, (rows, LANES), 0)
    row_in_chunk = row % chunk
    gc_all = g_all
    s = 1
    while s < chunk:
        gc_all = gc_all + jnp.where(row_in_chunk >= s, pltpu.roll(gc_all, s, axis=0), 0.0)
        s *= 2
    gct_all = gc_all.T

    lane = lax.broadcasted_iota(jnp.int32, (rows, LANES), 1)
    sub = lax.broadcasted_iota(jnp.int32, (LANES, rows), 0)
    ri = lax.broadcasted_iota(jnp.int32, (rows, rows), 0)
    ci = lax.broadcasted_iota(jnp.int32, (rows, rows), 1)
    same_chunk = (ri // chunk) == (ci // chunk)
    causal = jnp.logical_and(same_chunk, ri >= ci)
    strict = jnp.logical_and(same_chunk, ri > ci)
    n_steps = (chunk - 1).bit_length()

    hd = []
    for hh in range(hg):
        head = h_grp * hg + hh
        ls = slice(hh * LANES, (hh + 1) * LANES)
        qh, kh, vh = q_all[:, ls], k_all[:, ls], v_all[:, ls]
        qh = qh * lax.rsqrt(jnp.sum(qh * qh, axis=-1, keepdims=True) + NORM_EPS) * (GDN_DK ** -0.5)
        kh = kh * lax.rsqrt(jnp.sum(kh * kh, axis=-1, keepdims=True) + NORM_EPS)

        beta = jnp.sum(jnp.where(lane == BETA_LANE + head, beta_all, 0.0), axis=1, keepdims=True)
        gc = jnp.sum(jnp.where(lane == ALPHA_LANE + head, gc_all, 0.0), axis=1, keepdims=True)
        gc_row = jnp.sum(jnp.where(sub == ALPHA_LANE + head, gct_all, 0.0), axis=0, keepdims=True)

        decay = jnp.where(causal, jnp.exp(jnp.where(causal, gc - gc_row, 0.0)), 0.0)
        kb = kh * beta
        gram = _fdot(jnp.concatenate([kb, qh], axis=0), kh, ((1,), (1,)))
        e_gc = jnp.exp(gc)
        hd.append(dict(
            ls=ls, kh=kh, gc=gc,
            p=jnp.where(strict, -(gram[:rows] * decay), 0.0),
            a_qk=gram[rows:] * decay,
            x=jnp.concatenate([vh * beta, kb * e_gc], axis=1),
            q_dec=qh * e_gc))

    n_dots = 0
    for st in range(n_steps):
        for d in hd:
            if st == 0:
                d["n"] = d["p"]
                if n_steps > 1:
                    d["p"] = _fdot(d["p"], d["p"])
            elif st < n_steps - 1:
                res = _fdot(jnp.concatenate([d["n"], d["p"]], axis=0), d["p"])
                d["n"] = d["n"] + d["p"] + res[:rows]
                d["p"] = res[rows:]
            else:
                d["n"] = d["n"] + d["p"] + _fdot(d["n"], d["p"])
            n_dots += 1
            if n_dots % FILL_EVERY == 0:
                fill()
    for d in hd:
        d["x"] = d["x"] + _fdot(d["n"], d["x"])

    for d in hd:
        d["v_new"] = {}
        d["qs"] = {}
    for sq0 in range(0, bt, seq_group):
        pairs = [(sq, hh) for sq in range(sq0, min(sq0 + seq_group, bt)) for hh in range(hg)]
        state = {pr: s_scr[pr[0], pr[1]] for pr in pairs}
        for c in range(n_chunks):
            rsl = {sq: slice(sq * tt + c * chunk, sq * tt + (c + 1) * chunk) for sq, _ in pairs}
            ws, v_new, k_dec, e_last, upd = {}, {}, {}, {}, {}
            for pr in pairs:
                d, rs = hd[pr[1]], rsl[pr[0]]
                ws[pr] = _fdot(jnp.concatenate([d["x"][rs, LANES:], d["q_dec"][rs]], axis=0), state[pr])
            for pr in pairs:
                d, rs = hd[pr[1]], rsl[pr[0]]
                v_new[pr] = d["x"][rs, :LANES] - ws[pr][:chunk]
                g_last = d["gc"][rs.stop - 1:rs.stop]
                k_dec[pr] = d["kh"][rs] * jnp.exp(g_last - d["gc"][rs])
                e_last[pr] = jnp.exp(g_last)
                d["v_new"][(pr[0], c)] = v_new[pr]
                d["qs"][(pr[0], c)] = ws[pr][chunk:]
            for pr in pairs:
                upd[pr] = _fdot(k_dec[pr], v_new[pr], ((0,), (0,)))
            for pr in pairs:
                state[pr] = state[pr] * e_last[pr] + upd[pr]
            fill()
        for pr in pairs:
            s_scr[pr[0], pr[1]] = state[pr]

    order = [(sq, c) for sq in range(bt) for c in range(n_chunks)]
    gate = zgate()
    for d in hd:
        v_parts = [d["v_new"][k] for k in order]
        q_parts = [d["qs"][k] for k in order]
        v_new_all = jnp.concatenate(v_parts, axis=0) if len(v_parts) > 1 else v_parts[0]
        qs_all = jnp.concatenate(q_parts, axis=0) if len(q_parts) > 1 else q_parts[0]
        o = qs_all + _fdot(d["a_qk"], v_new_all)
        o = o * lax.rsqrt(jnp.mean(o * o, axis=-1, keepdims=True) + NORM_EPS) * nw_ref[...]
        o = o * _silu(gate[:, d["ls"]])
        out_ref[:, d["ls"]] = o.astype(out_ref.dtype)


def _gdn_init_state(j, cs_refs, s0_ref, xp_scr, s_scr):
    @pl.when(j == 0)
    def _():
        for idx, cs_ref in enumerate(cs_refs):
            xp_scr[idx, :, 5:8, :] = cs_ref[...]
        s_scr[...] = s0_ref[...]


def _gdn_z_body(zq_ref, zk_ref, zv_ref, zg_ref, ba_ref, csq_ref, csk_ref, csv_ref, s0_ref,
                cwq_ref, cwk_ref, cwv_ref, alog_ref, dtb_ref, nw_ref,
                out_ref, sout_ref, xp_scr, s_scr, *, bt, tt, chunk, hg):
    j = pl.program_id(2)
    _gdn_init_state(j, (csq_ref, csk_ref, csv_ref), s0_ref, xp_scr, s_scr)
    _gdn_core(lambda: zq_ref[...], lambda: zk_ref[...], lambda: zv_ref[...], lambda: zg_ref[...],
              lambda: ba_ref[...],
              (cwq_ref, cwk_ref, cwv_ref), alog_ref, dtb_ref, nw_ref, out_ref, xp_scr, s_scr,
              pl.program_id(1), bt=bt, tt=tt, chunk=chunk, hg=hg, seq_group=GDN_SEQ_GROUP)

    @pl.when(j == pl.num_programs(2) - 1)
    def _():
        sout_ref[...] = s_scr[...]


def _gdn_z(z, ba, conv_state, s0, cw, alog_pad, dtb_pad, nw, *, nb, t, bt, tt, chunk, hg):
    nt = t // tt
    rows = bt * tt
    width = LANES * hg
    per = SEG // width
    body = functools.partial(_gdn_z_body, bt=bt, tt=tt, chunk=chunk, hg=hg)

    def zspec(col):
        return pl.BlockSpec((rows, width), lambda i, h, j: (i * nt + j, col * per + h))

    def cs_spec(seg):
        return pl.BlockSpec((bt, CONV_W - 1, width), lambda i, h, j: (i, 0, seg * per + h))

    def cw_spec(seg):
        return pl.BlockSpec((CONV_W, width), lambda i, h, j: (0, seg * per + h))

    vec = lambda: pl.BlockSpec((1, LANES), lambda i, h, j: (0, 0))
    return pl.pallas_call(
        body,
        grid=(nb // bt, GDN_HEADS // hg, nt),
        in_specs=[
            zspec(COL_Q), zspec(COL_K), zspec(COL_V), zspec(COL_GDN_GATE),
            pl.BlockSpec((rows, LANES), lambda i, h, j: (i * nt + j, 0)),
            cs_spec(0), cs_spec(1), cs_spec(2),
            pl.BlockSpec((bt, hg, GDN_DK, GDN_DV), lambda i, h, j: (i, h, 0, 0)),
            cw_spec(0), cw_spec(1), cw_spec(2),
            vec(), vec(), vec(),
        ],
        out_specs=[
            pl.BlockSpec((rows, width), lambda i, h, j: (i * nt + j, h)),
            pl.BlockSpec((bt, hg, GDN_DK, GDN_DV), lambda i, h, j: (i, h, 0, 0)),
        ],
        out_shape=[
            jax.ShapeDtypeStruct((nb * t, GDN_VAL_W), BF16),
            jax.ShapeDtypeStruct((nb, GDN_HEADS, GDN_DK, GDN_DV), F32),
        ],
        scratch_shapes=[
            pltpu.VMEM((3, bt, SUBLANES + tt, width), F32),
            pltpu.VMEM((bt, hg, GDN_DK, GDN_DV), F32),
        ],
        compiler_params=pltpu.CompilerParams(
            dimension_semantics=("arbitrary", "arbitrary", "arbitrary"), vmem_limit_bytes=VMEM_LIMIT),
        name="gdn_z",
    )(z, z, z, z, ba, conv_state, conv_state, conv_state, s0, cw, cw, cw, alog_pad, dtb_pad, nw)


def _seq_body(x_ref, gain_ref, w_ref, wba_ref,
              lcs_ref, h0_ref, lcw_ref, lcb_ref, wax_ref, lba_ref, lbx_ref, lalog_ref,
              csq_ref, csk_ref, csv_ref, s0_ref, cwq_ref, cwk_ref, cwv_ref, galog_ref, dtb_ref, nw_ref,
              lru_out_ref, hout_ref, lcsout_ref, gdn_out_ref, sout_ref, gcsout_ref,
              lxp_scr, a_scr, b_scr, h_scr, gxp_scr, s_scr, *, bt, tt, chunk, reset_first):
    j = pl.program_id(1)

    @pl.when(j == 0)
    def _():
        lxp_scr[:, 5:8, :] = lcs_ref[...]
        h_scr[...] = h0_ref[...]

    _gdn_init_state(j, (csq_ref, csk_ref, csv_ref), s0_ref, gxp_scr, s_scr)

    u = _rms_norm_bf16(x_ref[...], gain_ref[...])

    def seg(c):
        return lambda: jnp.dot(u, w_ref[:, c * SEG:(c + 1) * SEG], preferred_element_type=F32)

    ba = lambda: jnp.dot(u, wba_ref[...], preferred_element_type=F32)

    stages = iter(_lru_stages(seg(0), seg(1), j, lcw_ref, lcb_ref, wax_ref, lba_ref, lbx_ref, lalog_ref,
                              lru_out_ref, lxp_scr, a_scr, b_scr, h_scr, bt=bt, tt=tt,
                              reset_first=reset_first))

    def fill():
        stage = next(stages, None)
        if stage is not None:
            stage()

    _gdn_core(seg(2 + COL_Q), seg(2 + COL_K), seg(2 + COL_V), seg(2 + COL_GDN_GATE), ba,
              (cwq_ref, cwk_ref, cwv_ref), galog_ref, dtb_ref, nw_ref, gdn_out_ref, gxp_scr, s_scr,
              0, bt=bt, tt=tt, chunk=chunk, hg=GDN_HEADS, seq_group=1, fill=fill)
    for stage in stages:
        stage()

    @pl.when(j == pl.num_programs(1) - 1)
    def _():
        hout_ref[...] = h_scr[...]
        lcsout_ref[...] = lxp_scr[:, 5:8, :]
        sout_ref[...] = s_scr[...]
        for idx in range(3):
            gcsout_ref[:, :, idx * SEG:(idx + 1) * SEG] = gxp_scr[idx, :, 5:8, :]


def _seq(x2, gain, w_seq, w_ba, lru_conv, h0, lcw, lcb, wax, lba, lbx, lalog,
         gdn_conv, s0, gcw, galog_pad, dtb_pad, nw, *, nb, t, bt, tt, chunk, reset_first):
    nt = t // tt
    rows = bt * tt
    hg = GDN_HEADS
    body = functools.partial(_seq_body, bt=bt, tt=tt, chunk=chunk, reset_first=reset_first)
    const2 = lambda i, j: (0, 0)
    vec = lambda n: pl.BlockSpec((1, n), const2)
    cs_spec = lambda seg: pl.BlockSpec((bt, CONV_W - 1, SEG), lambda i, j: (i, 0, seg))
    cw_spec = lambda seg: pl.BlockSpec((CONV_W, SEG), lambda i, j: (0, seg))
    state_spec = pl.BlockSpec((bt, hg, GDN_DK, GDN_DV), lambda i, j: (i, 0, 0, 0))
    return pl.pallas_call(
        body,
        grid=(nb // bt, nt),
        in_specs=[
            pl.BlockSpec((rows, D_MODEL), lambda i, j: (i * nt + j, 0)),
            vec(D_MODEL),
            _resident((D_MODEL, LRU_Z_WIDTH + GDN_Z_WIDTH), const2),
            _resident((D_MODEL, LANES), lambda i, j: (0, BA_COL_BLOCK)),
            pl.BlockSpec((bt, CONV_W - 1, LRU_WIDTH), lambda i, j: (i, 0, 0)),
            pl.BlockSpec((bt, 1, LRU_WIDTH), lambda i, j: (i, 0, 0)),
            pl.BlockSpec((CONV_W, LRU_WIDTH), const2),
            vec(LRU_WIDTH),
            pl.BlockSpec((LRU_BLOCKS, LRU_BLOCK, 2 * LRU_BLOCK), lambda i, j: (0, 0, 0)),
            vec(LRU_WIDTH), vec(LRU_WIDTH), vec(LRU_WIDTH),
            cs_spec(0), cs_spec(1), cs_spec(2),
            state_spec,
            cw_spec(0), cw_spec(1), cw_spec(2),
            vec(LANES), vec(LANES), vec(LANES),
        ],
        out_specs=[
            pl.BlockSpec((rows, LRU_WIDTH), lambda i, j: (i * nt + j, 0)),
            pl.BlockSpec((bt, 1, LRU_WIDTH), lambda i, j: (i, 0, 0)),
            pl.BlockSpec((bt, CONV_W - 1, LRU_WIDTH), lambda i, j: (i, 0, 0)),
            pl.BlockSpec((rows, GDN_VAL_W), lambda i, j: (i * nt + j, 0)),
            state_spec,
            pl.BlockSpec((bt, CONV_W - 1, GDN_QKV_W), lambda i, j: (i, 0, 0)),
        ],
        out_shape=[
            jax.ShapeDtypeStruct((nb * t, LRU_WIDTH), BF16),
            jax.ShapeDtypeStruct((nb, 1, LRU_WIDTH), F32),
            jax.ShapeDtypeStruct((nb, CONV_W - 1, LRU_WIDTH), F32),
            jax.ShapeDtypeStruct((nb * t, GDN_VAL_W), BF16),
            jax.ShapeDtypeStruct((nb, GDN_HEADS, GDN_DK, GDN_DV), F32),
            jax.ShapeDtypeStruct((nb, CONV_W - 1, GDN_QKV_W), F32),
        ],
        scratch_shapes=_lru_scratch(bt, tt) + [
            pltpu.VMEM((3, bt, SUBLANES + tt, SEG), F32),
            pltpu.VMEM((bt, hg, GDN_DK, GDN_DV), F32),
        ],
        compiler_params=pltpu.CompilerParams(
            dimension_semantics=("arbitrary", "arbitrary"), vmem_limit_bytes=SEQ_VMEM_LIMIT),
        name="seq",
    )(x2, gain, w_seq, w_ba, lru_conv, h0, lcw, lcb, wax, lba, lbx, lalog,
      gdn_conv, gdn_conv, gdn_conv, s0, gcw, gcw, gcw, galog_pad, dtb_pad, nw)


def _out_body(x_ref, gpre_ref, wm_ref, lru_ref, gdn_ref, wl_ref, wg_ref, wo_ref, gpost_ref, y_ref):
    x = x_ref[...]
    u = _rms_norm_bf16(x, gpre_ref[...])
    m = jnp.dot(u, wm_ref[...], preferred_element_type=F32)
    p_lru = jnp.dot(lru_ref[...], wl_ref[...], preferred_element_type=F32)
    p_gdn = jnp.dot(gdn_ref[...], wg_ref[...], preferred_element_type=F32)
    merged = _sigmoid(m[:, :D_MODEL]) * p_lru + _sigmoid(m[:, D_MODEL:]) * p_gdn
    y = jnp.dot(merged.astype(BF16), wo_ref[...], preferred_element_type=F32)
    ms = jnp.mean(y * y, axis=-1, keepdims=True)
    y_ref[...] = x + (y * lax.rsqrt(ms + NORM_EPS)) * gpost_ref[...]


def _out_proj(x2, gpre, w_m, lru_out, gdn_out, wl, wg, wo, gpost, *, tm):
    n = x2.shape[0]
    wspec = lambda: pl.BlockSpec((D_MODEL, D_MODEL), lambda i: (0, 0))
    vec = lambda: pl.BlockSpec((1, D_MODEL), lambda i: (0, 0))
    rowspec = lambda: pl.BlockSpec((tm, D_MODEL), lambda i: (i, 0))
    return pl.pallas_call(
        _out_body,
        grid=(n // tm,),
        in_specs=[
            rowspec(), vec(),
            pl.BlockSpec((D_MODEL, 2 * D_MODEL), lambda i: (0, 0)),
            rowspec(), rowspec(),
            wspec(), wspec(), wspec(),
            vec(),
        ],
        out_specs=rowspec(),
        out_shape=jax.ShapeDtypeStruct((n, D_MODEL), F32),
        compiler_params=pltpu.CompilerParams(
            dimension_semantics=("arbitrary",), vmem_limit_bytes=VMEM_LIMIT),
        name="out_proj",
    )(x2, gpre, w_m, lru_out, gdn_out, wl, wg, wo, gpost)


def _prep_weights(norm_pre, norm_post, w_in, lru_conv_w, lru_conv_b, lru_wa, lru_ba, lru_wx, lru_bx,
                  lru_a_logit, gdn_conv_w, gdn_A_log, gdn_dt_bias, gdn_norm_w, w_br_lru, w_br_gdn, w_out):
    c_m = BA_COL_BLOCK * LANES + 2 * GDN_HEADS
    wax = jnp.concatenate([lru_wa, lru_wx], axis=-1).astype(BF16)
    pad_alpha = lambda v: jnp.pad(v.reshape(1, GDN_HEADS), ((0, 0), (ALPHA_LANE, LANES - ALPHA_LANE - GDN_HEADS)))
    row = lambda v: v.reshape(1, -1)
    w_all = w_in.astype(BF16)
    return dict(
        norm_pre=row(norm_pre), norm_post=row(norm_post),
        w_all=w_all, w_m=w_all[:, c_m:],
        lru_cw=lru_conv_w, lru_cb=row(lru_conv_b), wax=wax, lru_ba=row(lru_ba), lru_bx=row(lru_bx),
        lru_alog=row(lru_a_logit), gdn_cw=gdn_conv_w, alog_pad=pad_alpha(gdn_A_log),
        dtb_pad=pad_alpha(gdn_dt_bias), gdn_nw=row(gdn_norm_w),
        wl=w_br_lru.astype(BF16), wg=w_br_gdn.astype(BF16), wo=w_out.astype(BF16))


def _layer(x, lru_conv, lru_h, gdn_conv, gdn_s, p, *, reset_first, fused, tm, lru_bt=None, lru_tt=None,
           gdn_bt=None, gdn_tt=None):
    nb, t, _ = x.shape
    x2 = x.reshape(nb * t, D_MODEL)
    h0 = lru_h.reshape(nb, 1, LRU_WIDTH)
    chunk = min(GDN_CHUNK, t)
    lru_w = (p["lru_cw"], p["lru_cb"], p["wax"], p["lru_ba"], p["lru_bx"], p["lru_alog"])
    gdn_w = (p["gdn_cw"], p["alog_pad"], p["dtb_pad"], p["gdn_nw"])
    if fused:
        lru_out, h_last, lru_conv_new, gdn_out, s_new, gdn_conv_new = _seq(
            x2, p["norm_pre"], p["w_all"], p["w_all"], lru_conv, h0, *lru_w, gdn_conv, gdn_s, *gdn_w,
            nb=nb, t=t, bt=1, tt=tm, chunk=chunk, reset_first=reset_first)
    else:
        lru_out, h_last, lru_conv_new = _lru(
            x2, p["norm_pre"], p["w_all"], lru_conv, h0, *lru_w, nb=nb, t=t, bt=lru_bt, tt=lru_tt,
            reset_first=reset_first)
        z, ba = _in_proj(x2, p["norm_pre"], p["w_all"], p["w_all"], tm=nb * t, tn=SEG,
                         col0=LRU_Z_WIDTH // SEG, width=GDN_Z_WIDTH)
        gdn_out, s_new = _gdn_z(z, ba, gdn_conv, gdn_s, *gdn_w, nb=nb, t=t, bt=gdn_bt, tt=gdn_tt, chunk=chunk,
                                hg=GDN_HEADS)
        gdn_conv_new = z.reshape(nb, t, GDN_Z_WIDTH)[:, t - (CONV_W - 1):, :GDN_QKV_W]
    y = _out_proj(x2, p["norm_pre"], p["w_m"], lru_out, gdn_out, p["wl"], p["wg"], p["wo"], p["norm_post"],
                  tm=min(512, nb * t))
    return y.reshape(nb, t, D_MODEL), lru_conv_new, h_last.reshape(nb, LRU_WIDTH), gdn_conv_new, s_new


def kernel(x_prompt, x_sample, state_lru_conv, state_lru_h, state_gdn_conv, state_gdn_S, norm_pre, norm_post, w_in, lru_conv_w, lru_conv_b, lru_wa, lru_ba, lru_wx, lru_bx, lru_a_logit, gdn_conv_w, gdn_A_log, gdn_dt_bias, gdn_norm_w, w_br_lru, w_br_gdn, w_out):
    depth = w_in.shape[0]
    assert depth == 1
    nb = x_prompt.shape[0]
    p = _prep_weights(norm_pre[0], norm_post[0], w_in[0], lru_conv_w[0], lru_conv_b[0], lru_wa[0], lru_ba[0],
                      lru_wx[0], lru_bx[0], lru_a_logit[0], gdn_conv_w[0], gdn_A_log[0], gdn_dt_bias[0],
                      gdn_norm_w[0], w_br_lru[0], w_br_gdn[0], w_out[0])
    yp, p_lc, p_lh, p_gc, p_gs = _layer(
        x_prompt,
        jnp.zeros((nb, CONV_W - 1, LRU_WIDTH), F32), jnp.zeros((nb, LRU_WIDTH), F32),
        jnp.zeros((nb, CONV_W - 1, GDN_QKV_W), F32), jnp.zeros((nb, GDN_HEADS, GDN_DK, GDN_DV), F32),
        p, reset_first=True, fused=True, tm=SEQ_TILE)
    ys, s_lc, s_lh, s_gc, s_gs = _layer(
        x_sample, state_lru_conv[0], state_lru_h[0], state_gdn_conv[0], state_gdn_S[0],
        p, reset_first=False, fused=False, tm=1024, lru_bt=32, lru_tt=8, gdn_bt=8, gdn_tt=8)
    return (yp, ys, p_lc[None], p_lh[None], p_gc[None], p_gs[None],
            s_lc[None], s_lh[None], s_gc[None], s_gs[None])
```

```python
import functools

import jax
import jax.numpy as jnp
from jax import lax
from jax.experimental import pallas as pl
from jax.experimental.pallas import tpu as pltpu

F32 = jnp.float32
BF16 = jnp.bfloat16

D_MODEL = 1024
CONV_W = 4
LRU_WIDTH = 1024
LRU_BLOCKS = 8
LRU_BLOCK = LRU_WIDTH // LRU_BLOCKS
LRU_C = 8.0
GDN_HEADS = 8
GDN_DK = 128
GDN_DV = 128
GDN_KEY_W = GDN_HEADS * GDN_DK
GDN_VAL_W = GDN_HEADS * GDN_DV
GDN_QKV_W = 2 * GDN_KEY_W + GDN_VAL_W
GDN_CHUNK = 64
NORM_EPS = 1e-6

LANES = 128
SUBLANES = 8
SEG = 1024
LRU_Z_WIDTH = 2 * SEG
GDN_Z_WIDTH = 4 * SEG
COL_Q, COL_K, COL_V, COL_GDN_GATE = range(4)
BETA_LANE = 0
ALPHA_LANE = GDN_HEADS
BA_COL_BLOCK = (LRU_Z_WIDTH + GDN_Z_WIDTH) // LANES
VMEM_LIMIT = 48 * 1024 * 1024
SEQ_VMEM_LIMIT = 60 * 1024 * 1024
SEQ_TILE = 128
SEQ_PROJ_ROWS = 512
GDN_SEQ_GROUP = 2
LRU_CARRY_PIECE = 8
FILL_EVERY = 6


def _softplus(x):
    return jnp.maximum(x, 0.0) + jnp.log1p(jnp.exp(-jnp.abs(x)))


def _sigmoid(x):
    return jax.nn.sigmoid(x)


def _silu(x):
    h = 0.5 * x
    return h + h * jnp.tanh(h)


def _fdot(a, b, dims=((1,), (0,))):
    return lax.dot_general(a, b, (dims, ((), ())), preferred_element_type=F32)


def _rms_norm_bf16(x, gain):
    ms = jnp.mean(x * x, axis=-1, keepdims=True)
    return ((x * lax.rsqrt(ms + NORM_EPS)) * gain).astype(BF16)


def _resident(shape, index_map):
    return pl.BlockSpec(shape, index_map, pipeline_mode=pl.Buffered(1))


def _in_proj_body(x_ref, gain_ref, w_ref, wba_ref, z_ref, ba_ref, u_scr):
    @pl.when(pl.program_id(1) == 0)
    def _():
        u = _rms_norm_bf16(x_ref[...], gain_ref[...])
        u_scr[...] = u
        ba_ref[...] = jnp.dot(u, wba_ref[...], preferred_element_type=F32)

    z_ref[...] = jnp.dot(u_scr[...], w_ref[...], preferred_element_type=F32)


def _in_proj(x2, gain, w, w_ba, *, tm, tn, col0, width):
    n = x2.shape[0]
    return pl.pallas_call(
        _in_proj_body,
        grid=(n // tm, width // tn),
        in_specs=[
            pl.BlockSpec((tm, D_MODEL), lambda i, j: (i, 0)),
            pl.BlockSpec((1, D_MODEL), lambda i, j: (0, 0)),
            pl.BlockSpec((D_MODEL, tn), lambda i, j: (0, col0 + j)),
            pl.BlockSpec((D_MODEL, LANES), lambda i, j: (0, BA_COL_BLOCK)),
        ],
        out_specs=[
            pl.BlockSpec((tm, tn), lambda i, j: (i, j)),
            pl.BlockSpec((tm, LANES), lambda i, j: (i, 0)),
        ],
        out_shape=[
            jax.ShapeDtypeStruct((n, width), F32),
            jax.ShapeDtypeStruct((n, LANES), F32),
        ],
        scratch_shapes=[pltpu.VMEM((tm, D_MODEL), BF16)],
        compiler_params=pltpu.CompilerParams(
            dimension_semantics=("arbitrary", "arbitrary"), vmem_limit_bytes=VMEM_LIMIT),
        name="in_proj",
    )(x2, gain, w, w_ba)


def _lru_stages(z_lru, gate, j, cw_ref, cb_ref, wax_ref, ba_ref, bx_ref, alog_ref, out_ref,
                xp_scr, a_scr, b_scr, h_scr, *, bt, tt, reset_first):
    rows = bt * tt
    width = LRU_WIDTH
    n_vreg_rows = rows // SUBLANES
    n_groups = tt // SUBLANES
    env = {}

    def conv():
        cur = z_lru().reshape(bt, tt, width)
        xp_scr[:, 8:8 + tt, :] = cur
        cw = cw_ref[...]
        xc = cur * cw[CONV_W - 1].reshape(1, 1, width)
        for i in range(CONV_W - 2, -1, -1):
            xc = xc + xp_scr[:, 5 + i:5 + i + tt, :] * cw[i].reshape(1, 1, width)
        xp_scr[:, 5:8, :] = xp_scr[:, 5 + tt:8 + tt, :]
        env["xc"] = (xc + cb_ref[...].reshape(1, 1, width)).reshape(rows, width)
        a_l = alog_ref[...]
        env["log_sig_a"] = jnp.minimum(a_l, 0.0) - jnp.log1p(jnp.exp(-jnp.abs(a_l)))
        env["sub3"] = lax.broadcasted_iota(jnp.int32, (n_vreg_rows, SUBLANES, LRU_BLOCK), 1)
        if reset_first:
            row = lax.broadcasted_iota(jnp.int32, (rows, LRU_BLOCK), 0)
            env["is_reset"] = jnp.logical_and(row % tt == 0, j == 0)

    def block(g):
        gs = slice(g * LRU_BLOCK, (g + 1) * LRU_BLOCK)
        xg = env["xc"][:, gs]
        pre = jnp.dot(xg.astype(BF16), wax_ref[g], preferred_element_type=F32)
        r = _sigmoid(pre[:, :LRU_BLOCK] + ba_ref[:, gs])
        ig = _sigmoid(pre[:, LRU_BLOCK:] + bx_ref[:, gs])
        log_a = (LRU_C * r) * env["log_sig_a"][:, gs]
        a = jnp.exp(log_a)
        t = jnp.tanh(log_a)
        m2 = (-2.0 * t) / (1.0 - t)
        mult = jnp.where(m2 > 0.0, m2 * lax.rsqrt(m2), 0.0)
        if reset_first:
            a = jnp.where(env["is_reset"], 0.0, a)
            mult = jnp.where(env["is_reset"], 1.0, mult)
        b = mult * ig * xg
        a = a.reshape(n_vreg_rows, SUBLANES, LRU_BLOCK)
        b = b.reshape(n_vreg_rows, SUBLANES, LRU_BLOCK)
        s = 1
        while s < SUBLANES:
            keep = env["sub3"] >= s
            a_sh = jnp.where(keep, pltpu.roll(a, s, axis=1), 1.0)
            b_sh = jnp.where(keep, pltpu.roll(b, s, axis=1), 0.0)
            b = a * b_sh + b
            a = a * a_sh
            s *= 2
        a_scr[:, :, gs] = a.reshape(bt, tt, LRU_BLOCK)
        b_scr[:, :, gs] = b.reshape(bt, tt, LRU_BLOCK)

    def carry_groups(g0, g1):
        carry = env.get("carry")
        if carry is None:
            carry = h_scr[...]
        for i in range(g0, g1):
            rs = slice(i * SUBLANES, (i + 1) * SUBLANES)
            h = b_scr[:, rs, :] + a_scr[:, rs, :] * carry
            b_scr[:, rs, :] = h
            carry = h[:, SUBLANES - 1:SUBLANES, :]
        env["carry"] = carry
        if g1 == n_groups:
            h_scr[...] = carry

    def finish():
        h = b_scr[...].reshape(rows, width)
        out_ref[...] = (h * _silu(gate())).astype(out_ref.dtype)

    per_piece = min(n_groups, LRU_CARRY_PIECE)
    pieces = [functools.partial(carry_groups, g0, min(g0 + per_piece, n_groups))
              for g0 in range(0, n_groups, per_piece)]
    return [conv] + [functools.partial(block, g) for g in range(LRU_BLOCKS)] + pieces + [finish]


def _lru_body(x_ref, gain_ref, w_ref, cs_ref, h0_ref, cw_ref, cb_ref, wax_ref, ba_ref, bx_ref, alog_ref,
              out_ref, hout_ref, csout_ref, xp_scr, a_scr, b_scr, h_scr, *, bt, tt, reset_first):
    j = pl.program_id(1)

    @pl.when(j == 0)
    def _():
        xp_scr[:, 5:8, :] = cs_ref[...]
        h_scr[...] = h0_ref[...]

    u = _rms_norm_bf16(x_ref[...], gain_ref[...])
    z = jnp.dot(u, w_ref[...], preferred_element_type=F32)
    for stage in _lru_stages(lambda: z[:, :LRU_WIDTH], lambda: z[:, LRU_WIDTH:], j,
                             cw_ref, cb_ref, wax_ref, ba_ref, bx_ref, alog_ref, out_ref, xp_scr, a_scr, b_scr, h_scr, bt=bt, tt=tt,
                             reset_first=reset_first):
        stage()

    @pl.when(j == pl.num_programs(1) - 1)
    def _():
        hout_ref[...] = h_scr[...]
        csout_ref[...] = xp_scr[:, 5:8, :]


def _lru_scratch(bt, tt):
    return [
        pltpu.VMEM((bt, SUBLANES + tt, LRU_WIDTH), F32),
        pltpu.VMEM((bt, tt, LRU_WIDTH), F32),
        pltpu.VMEM((bt, tt, LRU_WIDTH), F32),
        pltpu.VMEM((bt, 1, LRU_WIDTH), F32),
    ]


def _lru(x2, gain, w_lru, conv_state, h0, cw, cb, wax, ba, bx, alog, *, nb, t, bt, tt, reset_first):
    nt = t // tt
    rows = bt * tt
    body = functools.partial(_lru_body, bt=bt, tt=tt, reset_first=reset_first)
    vec = lambda: pl.BlockSpec((1, LRU_WIDTH), lambda i, j: (0, 0))
    return pl.pallas_call(
        body,
        grid=(nb // bt, nt),
        in_specs=[
            pl.BlockSpec((rows, D_MODEL), lambda i, j: (i * nt + j, 0)),
            vec(),
            pl.BlockSpec((D_MODEL, LRU_Z_WIDTH), lambda i, j: (0, 0)),
            pl.BlockSpec((bt, CONV_W - 1, LRU_WIDTH), lambda i, j: (i, 0, 0)),
            pl.BlockSpec((bt, 1, LRU_WIDTH), lambda i, j: (i, 0, 0)),
            pl.BlockSpec((CONV_W, LRU_WIDTH), lambda i, j: (0, 0)),
            vec(),
            pl.BlockSpec((LRU_BLOCKS, LRU_BLOCK, 2 * LRU_BLOCK), lambda i, j: (0, 0, 0)),
            vec(), vec(), vec(),
        ],
        out_specs=[
            pl.BlockSpec((rows, LRU_WIDTH), lambda i, j: (i * nt + j, 0)),
            pl.BlockSpec((bt, 1, LRU_WIDTH), lambda i, j: (i, 0, 0)),
            pl.BlockSpec((bt, CONV_W - 1, LRU_WIDTH), lambda i, j: (i, 0, 0)),
        ],
        out_shape=[
            jax.ShapeDtypeStruct((nb * t, LRU_WIDTH), BF16),
            jax.ShapeDtypeStruct((nb, 1, LRU_WIDTH), F32),
            jax.ShapeDtypeStruct((nb, CONV_W - 1, LRU_WIDTH), F32),
        ],
        scratch_shapes=_lru_scratch(bt, tt),
        compiler_params=pltpu.CompilerParams(
            dimension_semantics=("arbitrary", "arbitrary"), vmem_limit_bytes=VMEM_LIMIT),
        name="lru",
    )(x2, gain, w_lru, conv_state, h0, cw, cb, wax, ba, bx, alog)


def _gdn_core(zq, zk, zv, zgate, ba, cw_refs, alog_ref, dtb_ref, nw_ref, out_ref, xp_scr, s_scr,
              h_grp, *, bt, tt, chunk, hg, seq_group, fill=lambda: None):
    rows = bt * tt
    width = LANES * hg
    n_chunks = tt // chunk

    def conv_silu(idx, cur2, cw_ref):
        cur = cur2.reshape(bt, tt, width)
        xp_scr[idx, :, 8:8 + tt, :] = cur
        cw = cw_ref[...]
        y = cur * cw[CONV_W - 1].reshape(1, 1, width)
        for i in range(CONV_W - 2, -1, -1):
            y = y + xp_scr[idx, :, 5 + i:5 + i + tt, :] * cw[i].reshape(1, 1, width)
        xp_scr[idx, :, 5:8, :] = xp_scr[idx, :, 5 + tt:8 + tt, :]
        return _silu(y.reshape(rows, width))

    q_all = conv_silu(0, zq(), cw_refs[0])
    k_all = conv_silu(1, zk(), cw_refs[1])
    v_all = conv_silu(2, zv(), cw_refs[2])
    ba = ba()

    beta_all = _sigmoid(ba)
    g_all = -jnp.exp(alog_ref[...]) * _softplus(ba + dtb_ref[...])
    row = lax.broadcasted_iota(jnp.int32, (rows, LANES), 0)
    row_in_chunk = row % chunk
    gc_all = g_all
    s = 1
    while s < chunk:
        gc_all = gc_all + jnp.where(row_in_chunk >= s, pltpu.roll(gc_all, s, axis=0), 0.0)
        s *= 2
    gct_all = gc_all.T

    lane = lax.broadcasted_iota(jnp.int32, (rows, LANES), 1)
    sub = lax.broadcasted_iota(jnp.int32, (LANES, rows), 0)
    ri = lax.broadcasted_iota(jnp.int32, (rows, rows), 0)
    ci = lax.broadcasted_iota(jnp.int32, (rows, rows), 1)
    same_chunk = (ri // chunk) == (ci // chunk)
    causal = jnp.logical_and(same_chunk, ri >= ci)
    strict = jnp.logical_and(same_chunk, ri > ci)
    n_steps = (chunk - 1).bit_length()

    hd = []
    for hh in range(hg):
        head = h_grp * hg + hh
        ls = slice(hh * LANES, (hh + 1) * LANES)
        qh, kh, vh = q_all[:, ls], k_all[:, ls], v_all[:, ls]
        qh = qh * lax.rsqrt(jnp.sum(qh * qh, axis=-1, keepdims=True) + NORM_EPS) * (GDN_DK ** -0.5)
        kh = kh * lax.rsqrt(jnp.sum(kh * kh, axis=-1, keepdims=True) + NORM_EPS)

        beta = jnp.sum(jnp.where(lane == BETA_LANE + head, beta_all, 0.0), axis=1, keepdims=True)
        gc = jnp.sum(jnp.where(lane == ALPHA_LANE + head, gc_all, 0.0), axis=1, keepdims=True)
        gc_row = jnp.sum(jnp.where(sub == ALPHA_LANE + head, gct_all, 0.0), axis=0, keepdims=True)

        decay = jnp.where(causal, jnp.exp(jnp.where(causal, gc - gc_row, 0.0)), 0.0)
        kb = kh * beta
        gram = _fdot(jnp.concatenate([kb, qh], axis=0), kh, ((1,), (1,)))
        e_gc = jnp.exp(gc)
        hd.append(dict(
            ls=ls, kh=kh, gc=gc,
            p=jnp.where(strict, -(gram[:rows] * decay), 0.0),
            a_qk=gram[rows:] * decay,
            x=jnp.concatenate([vh * beta, kb * e_gc], axis=1),
            q_dec=qh * e_gc))

    n_dots = 0
    for st in range(n_steps):
        for d in hd:
            if st == 0:
                d["n"] = d["p"]
                if n_steps > 1:
                    d["p"] = _fdot(d["p"], d["p"])
            elif st < n_steps - 1:
                res = _fdot(jnp.concatenate([d["n"], d["p"]], axis=0), d["p"])
                d["n"] = d["n"] + d["p"] + res[:rows]
                d["p"] = res[rows:]
            else:
                d["n"] = d["n"] + d["p"] + _fdot(d["n"], d["p"])
            n_dots += 1
            if n_dots % FILL_EVERY == 0:
                fill()
    for d in hd:
        d["x"] = d["x"] + _fdot(d["n"], d["x"])

    for d in hd:
        d["v_new"] = {}
        d["qs"] = {}
    for sq0 in range(0, bt, seq_group):
        pairs = [(sq, hh) for sq in range(sq0, min(sq0 + seq_group, bt)) for hh in range(hg)]
        state = {pr: s_scr[pr[0], pr[1]] for pr in pairs}
        for c in range(n_chunks):
            rsl = {sq: slice(sq * tt + c * chunk, sq * tt + (c + 1) * chunk) for sq, _ in pairs}
            ws, v_new, k_dec, e_last, upd = {}, {}, {}, {}, {}
            for pr in pairs:
                d, rs = hd[pr[1]], rsl[pr[0]]
                ws[pr] = _fdot(jnp.concatenate([d["x"][rs, LANES:], d["q_dec"][rs]], axis=0), state[pr])
            for pr in pairs:
                d, rs = hd[pr[1]], rsl[pr[0]]
                v_new[pr] = d["x"][rs, :LANES] - ws[pr][:chunk]
                g_last = d["gc"][rs.stop - 1:rs.stop]
                k_dec[pr] = d["kh"][rs] * jnp.exp(g_last - d["gc"][rs])
                e_last[pr] = jnp.exp(g_last)
                d["v_new"][(pr[0], c)] = v_new[pr]
                d["qs"][(pr[0], c)] = ws[pr][chunk:]
            for pr in pairs:
                upd[pr] = _fdot(k_dec[pr], v_new[pr], ((0,), (0,)))
            for pr in pairs:
                state[pr] = state[pr] * e_last[pr] + upd[pr]
            fill()
        for pr in pairs:
            s_scr[pr[0], pr[1]] = state[pr]

    order = [(sq, c) for sq in range(bt) for c in range(n_chunks)]
    gate = zgate()
    for d in hd:
        v_parts = [d["v_new"][k] for k in order]
        q_parts = [d["qs"][k] for k in order]
        v_new_all = jnp.concatenate(v_parts, axis=0) if len(v_parts) > 1 else v_parts[0]
        qs_all = jnp.concatenate(q_parts, axis=0) if len(q_parts) > 1 else q_parts[0]
        o = qs_all + _fdot(d["a_qk"], v_new_all)
        o = o * lax.rsqrt(jnp.mean(o * o, axis=-1, keepdims=True) + NORM_EPS) * nw_ref[...]
        o = o * _silu(gate[:, d["ls"]])
        out_ref[:, d["ls"]] = o.astype(out_ref.dtype)


def _gdn_init_state(j, cs_refs, s0_ref, xp_scr, s_scr):
    @pl.when(j == 0)
    def _():
        for idx, cs_ref in enumerate(cs_refs):
            xp_scr[idx, :, 5:8, :] = cs_ref[...]
        s_scr[...] = s0_ref[...]


def _gdn_z_body(zq_ref, zk_ref, zv_ref, zg_ref, ba_ref, csq_ref, csk_ref, csv_ref, s0_ref,
                cwq_ref, cwk_ref, cwv_ref, alog_ref, dtb_ref, nw_ref,
                out_ref, sout_ref, xp_scr, s_scr, *, bt, tt, chunk, hg):
    j = pl.program_id(2)
    _gdn_init_state(j, (csq_ref, csk_ref, csv_ref), s0_ref, xp_scr, s_scr)
    _gdn_core(lambda: zq_ref[...], lambda: zk_ref[...], lambda: zv_ref[...], lambda: zg_ref[...],
              lambda: ba_ref[...],
              (cwq_ref, cwk_ref, cwv_ref), alog_ref, dtb_ref, nw_ref, out_ref, xp_scr, s_scr,
              pl.program_id(1), bt=bt, tt=tt, chunk=chunk, hg=hg, seq_group=GDN_SEQ_GROUP)

    @pl.when(j == pl.num_programs(2) - 1)
    def _():
        sout_ref[...] = s_scr[...]


def _gdn_z(z, ba, conv_state, s0, cw, alog_pad, dtb_pad, nw, *, nb, t, bt, tt, chunk, hg):
    nt = t // tt
    rows = bt * tt
    width = LANES * hg
    per = SEG // width
    body = functools.partial(_gdn_z_body, bt=bt, tt=tt, chunk=chunk, hg=hg)

    def zspec(col):
        return pl.BlockSpec((rows, width), lambda i, h, j: (i * nt + j, col * per + h))

    def cs_spec(seg):
        return pl.BlockSpec((bt, CONV_W - 1, width), lambda i, h, j: (i, 0, seg * per + h))

    def cw_spec(seg):
        return pl.BlockSpec((CONV_W, width), lambda i, h, j: (0, seg * per + h))

    vec = lambda: pl.BlockSpec((1, LANES), lambda i, h, j: (0, 0))
    return pl.pallas_call(
        body,
        grid=(nb // bt, GDN_HEADS // hg, nt),
        in_specs=[
            zspec(COL_Q), zspec(COL_K), zspec(COL_V), zspec(COL_GDN_GATE),
            pl.BlockSpec((rows, LANES), lambda i, h, j: (i * nt + j, 0)),
            cs_spec(0), cs_spec(1), cs_spec(2),
            pl.BlockSpec((bt, hg, GDN_DK, GDN_DV), lambda i, h, j: (i, h, 0, 0)),
            cw_spec(0), cw_spec(1), cw_spec(2),
            vec(), vec(), vec(),
        ],
        out_specs=[
            pl.BlockSpec((rows, width), lambda i, h, j: (i * nt + j, h)),
            pl.BlockSpec((bt, hg, GDN_DK, GDN_DV), lambda i, h, j: (i, h, 0, 0)),
        ],
        out_shape=[
            jax.ShapeDtypeStruct((nb * t, GDN_VAL_W), BF16),
            jax.ShapeDtypeStruct((nb, GDN_HEADS, GDN_DK, GDN_DV), F32),
        ],
        scratch_shapes=[
            pltpu.VMEM((3, bt, SUBLANES + tt, width), F32),
            pltpu.VMEM((bt, hg, GDN_DK, GDN_DV), F32),
        ],
        compiler_params=pltpu.CompilerParams(
            dimension_semantics=("arbitrary", "arbitrary", "arbitrary"), vmem_limit_bytes=VMEM_LIMIT),
        name="gdn_z",
    )(z, z, z, z, ba, conv_state, conv_state, conv_state, s0, cw, cw, cw, alog_pad, dtb_pad, nw)


def _seq_body(x_ref, gain_ref, w_ref, wba_ref,
              lcs_ref, h0_ref, lcw_ref, lcb_ref, wax_ref, lba_ref, lbx_ref, lalog_ref,
              csq_ref, csk_ref, csv_ref, s0_ref, cwq_ref, cwk_ref, cwv_ref, galog_ref, dtb_ref, nw_ref, wm_ref,
              lru_out_ref, hout_ref, lcsout_ref, gdn_out_ref, sout_ref, gcsout_ref, m_ref,
              lxp_scr, a_scr, b_scr, h_scr, gxp_scr, s_scr, *, tt, sub_tt, chunk, reset_first):
    j = pl.program_id(1)
    n_sub = tt // sub_tt

    @pl.when(j == 0)
    def _():
        lxp_scr[:, 5:8, :] = lcs_ref[...]
        h_scr[...] = h0_ref[...]

    _gdn_init_state(j, (csq_ref, csk_ref, csv_ref), s0_ref, gxp_scr, s_scr)

    u = _rms_norm_bf16(x_ref[...], gain_ref[...])
    projected = {}

    def seg(c, sub):
        def get():
            if c not in projected:
                w = wba_ref[...] if c == "ba" else w_ref[:, c * SEG:(c + 1) * SEG]
                projected[c] = jnp.dot(u, w, preferred_element_type=F32)
            return projected[c][sub * sub_tt:(sub + 1) * sub_tt]
        return get

    for sub in range(n_sub):
        rows = slice(sub * sub_tt, (sub + 1) * sub_tt)
        stages = iter(_lru_stages(seg(0, sub), seg(1, sub), j * n_sub + sub, lcw_ref, lcb_ref, wax_ref, lba_ref,
                                  lbx_ref, lalog_ref, lru_out_ref.at[rows], lxp_scr, a_scr, b_scr, h_scr,
                                  bt=1, tt=sub_tt, reset_first=reset_first))

        def fill():
            stage = next(stages, None)
            if stage is not None:
                stage()

        m_ref[rows, :] = jnp.dot(u[rows], wm_ref[...], preferred_element_type=F32)
        _gdn_core(seg(2 + COL_Q, sub), seg(2 + COL_K, sub), seg(2 + COL_V, sub), seg(2 + COL_GDN_GATE, sub),
                  seg("ba", sub), (cwq_ref, cwk_ref, cwv_ref), galog_ref, dtb_ref, nw_ref, gdn_out_ref.at[rows],
                  gxp_scr, s_scr, 0, bt=1, tt=sub_tt, chunk=chunk, hg=GDN_HEADS, seq_group=1, fill=fill)
        for stage in stages:
            stage()

    @pl.when(j == pl.num_programs(1) - 1)
    def _():
        hout_ref[...] = h_scr[...]
        lcsout_ref[...] = lxp_scr[:, 5:8, :]
        sout_ref[...] = s_scr[...]
        for idx in range(3):
            gcsout_ref[:, :, idx * SEG:(idx + 1) * SEG] = gxp_scr[idx, :, 5:8, :]


def _seq(x2, gain, w_seq, w_ba, lru_conv, h0, lcw, lcb, wax, lba, lbx, lalog,
         gdn_conv, s0, gcw, galog_pad, dtb_pad, nw, w_m, *, nb, t, tt, sub_tt, chunk, reset_first):
    nt = t // tt
    rows = tt
    bt = 1
    hg = GDN_HEADS
    body = functools.partial(_seq_body, tt=tt, sub_tt=sub_tt, chunk=chunk, reset_first=reset_first)
    const2 = lambda i, j: (0, 0)
    vec = lambda n: pl.BlockSpec((1, n), const2)
    cs_spec = lambda seg: pl.BlockSpec((bt, CONV_W - 1, SEG), lambda i, j: (i, 0, seg))
    cw_spec = lambda seg: pl.BlockSpec((CONV_W, SEG), lambda i, j: (0, seg))
    state_spec = pl.BlockSpec((bt, hg, GDN_DK, GDN_DV), lambda i, j: (i, 0, 0, 0))
    return pl.pallas_call(
        body,
        grid=(nb // bt, nt),
        in_specs=[
            pl.BlockSpec((rows, D_MODEL), lambda i, j: (i * nt + j, 0)),
            vec(D_MODEL),
            _resident((D_MODEL, LRU_Z_WIDTH + GDN_Z_WIDTH), const2),
            _resident((D_MODEL, LANES), lambda i, j: (0, BA_COL_BLOCK)),
            pl.BlockSpec((bt, CONV_W - 1, LRU_WIDTH), lambda i, j: (i, 0, 0)),
            pl.BlockSpec((bt, 1, LRU_WIDTH), lambda i, j: (i, 0, 0)),
            pl.BlockSpec((CONV_W, LRU_WIDTH), const2),
            vec(LRU_WIDTH),
            pl.BlockSpec((LRU_BLOCKS, LRU_BLOCK, 2 * LRU_BLOCK), lambda i, j: (0, 0, 0)),
            vec(LRU_WIDTH), vec(LRU_WIDTH), vec(LRU_WIDTH),
            cs_spec(0), cs_spec(1), cs_spec(2),
            state_spec,
            cw_spec(0), cw_spec(1), cw_spec(2),
            vec(LANES), vec(LANES), vec(LANES),
            _resident((D_MODEL, 2 * D_MODEL), const2),
        ],
        out_specs=[
            pl.BlockSpec((rows, LRU_WIDTH), lambda i, j: (i * nt + j, 0)),
            pl.BlockSpec((bt, 1, LRU_WIDTH), lambda i, j: (i, 0, 0)),
            pl.BlockSpec((bt, CONV_W - 1, LRU_WIDTH), lambda i, j: (i, 0, 0)),
            pl.BlockSpec((rows, GDN_VAL_W), lambda i, j: (i * nt + j, 0)),
            state_spec,
            pl.BlockSpec((bt, CONV_W - 1, GDN_QKV_W), lambda i, j: (i, 0, 0)),
            pl.BlockSpec((rows, 2 * D_MODEL), lambda i, j: (i * nt + j, 0)),
        ],
        out_shape=[
            jax.ShapeDtypeStruct((nb * t, LRU_WIDTH), BF16),
            jax.ShapeDtypeStruct((nb, 1, LRU_WIDTH), F32),
            jax.ShapeDtypeStruct((nb, CONV_W - 1, LRU_WIDTH), F32),
            jax.ShapeDtypeStruct((nb * t, GDN_VAL_W), BF16),
            jax.ShapeDtypeStruct((nb, GDN_HEADS, GDN_DK, GDN_DV), F32),
            jax.ShapeDtypeStruct((nb, CONV_W - 1, GDN_QKV_W), F32),
            jax.ShapeDtypeStruct((nb * t, 2 * D_MODEL), F32),
        ],
        scratch_shapes=_lru_scratch(bt, sub_tt) + [
            pltpu.VMEM((3, bt, SUBLANES + sub_tt, SEG), F32),
            pltpu.VMEM((bt, hg, GDN_DK, GDN_DV), F32),
        ],
        compiler_params=pltpu.CompilerParams(
            dimension_semantics=("arbitrary", "arbitrary"), vmem_limit_bytes=SEQ_VMEM_LIMIT),
        name="seq",
    )(x2, gain, w_seq, w_ba, lru_conv, h0, lcw, lcb, wax, lba, lbx, lalog,
      gdn_conv, gdn_conv, gdn_conv, s0, gcw, gcw, gcw, galog_pad, dtb_pad, nw, w_m)


def _out_body(x_ref, gpre_ref, wm_ref, lru_ref, gdn_ref, wl_ref, wg_ref, wo_ref, gpost_ref, y_ref, *, m_given):
    x = x_ref[...]
    if m_given:
        m = wm_ref[...]
    else:
        u = _rms_norm_bf16(x, gpre_ref[...])
        m = jnp.dot(u, wm_ref[...], preferred_element_type=F32)
    p_lru = jnp.dot(lru_ref[...], wl_ref[...], preferred_element_type=F32)
    p_gdn = jnp.dot(gdn_ref[...], wg_ref[...], preferred_element_type=F32)
    merged = _sigmoid(m[:, :D_MODEL]) * p_lru + _sigmoid(m[:, D_MODEL:]) * p_gdn
    y = jnp.dot(merged.astype(BF16), wo_ref[...], preferred_element_type=F32)
    ms = jnp.mean(y * y, axis=-1, keepdims=True)
    y_ref[...] = x + (y * lax.rsqrt(ms + NORM_EPS)) * gpost_ref[...]


def _out_proj(x2, gpre, w_m, lru_out, gdn_out, wl, wg, wo, gpost, *, tm, m_given=False):
    n = x2.shape[0]
    wspec = lambda: pl.BlockSpec((D_MODEL, D_MODEL), lambda i: (0, 0))
    vec = lambda: pl.BlockSpec((1, D_MODEL), lambda i: (0, 0))
    rowspec = lambda: pl.BlockSpec((tm, D_MODEL), lambda i: (i, 0))
    m_spec = (pl.BlockSpec((tm, 2 * D_MODEL), lambda i: (i, 0)) if m_given
              else pl.BlockSpec((D_MODEL, 2 * D_MODEL), lambda i: (0, 0)))
    return pl.pallas_call(
        functools.partial(_out_body, m_given=m_given),
        grid=(n // tm,),
        in_specs=[
            rowspec(), vec(),
            m_spec,
            rowspec(), rowspec(),
            wspec(), wspec(), wspec(),
            vec(),
        ],
        out_specs=rowspec(),
        out_shape=jax.ShapeDtypeStruct((n, D_MODEL), F32),
        compiler_params=pltpu.CompilerParams(
            dimension_semantics=("arbitrary",), vmem_limit_bytes=VMEM_LIMIT),
        name="out_proj",
    )(x2, gpre, w_m, lru_out, gdn_out, wl, wg, wo, gpost)


def _prep_weights(norm_pre, norm_post, w_in, lru_conv_w, lru_conv_b, lru_wa, lru_ba, lru_wx, lru_bx,
                  lru_a_logit, gdn_conv_w, gdn_A_log, gdn_dt_bias, gdn_norm_w, w_br_lru, w_br_gdn, w_out):
    c_m = BA_COL_BLOCK * LANES + 2 * GDN_HEADS
    wax = jnp.concatenate([lru_wa, lru_wx], axis=-1).astype(BF16)
    pad_alpha = lambda v: jnp.pad(v.reshape(1, GDN_HEADS), ((0, 0), (ALPHA_LANE, LANES - ALPHA_LANE - GDN_HEADS)))
    row = lambda v: v.reshape(1, -1)
    w_all = w_in.astype(BF16)
    return dict(
        norm_pre=row(norm_pre), norm_post=row(norm_post),
        w_all=w_all, w_m=w_all[:, c_m:],
        lru_cw=lru_conv_w, lru_cb=row(lru_conv_b), wax=wax, lru_ba=row(lru_ba), lru_bx=row(lru_bx),
        lru_alog=row(lru_a_logit), gdn_cw=gdn_conv_w, alog_pad=pad_alpha(gdn_A_log),
        dtb_pad=pad_alpha(gdn_dt_bias), gdn_nw=row(gdn_norm_w),
        wl=w_br_lru.astype(BF16), wg=w_br_gdn.astype(BF16), wo=w_out.astype(BF16))


def _layer(x, lru_conv, lru_h, gdn_conv, gdn_s, p, *, reset_first, fused, tm, lru_bt=None, lru_tt=None,
           gdn_bt=None, gdn_tt=None):
    nb, t, _ = x.shape
    x2 = x.reshape(nb * t, D_MODEL)
    h0 = lru_h.reshape(nb, 1, LRU_WIDTH)
    chunk = min(GDN_CHUNK, t)
    lru_w = (p["lru_cw"], p["lru_cb"], p["wax"], p["lru_ba"], p["lru_bx"], p["lru_alog"])
    gdn_w = (p["gdn_cw"], p["alog_pad"], p["dtb_pad"], p["gdn_nw"])
    w_m = p["w_m"]
    if fused:
        lru_out, h_last, lru_conv_new, gdn_out, s_new, gdn_conv_new, w_m = _seq(
            x2, p["norm_pre"], p["w_all"], p["w_all"], lru_conv, h0, *lru_w, gdn_conv, gdn_s, *gdn_w, w_m,
            nb=nb, t=t, tt=tm, sub_tt=SEQ_TILE, chunk=chunk, reset_first=reset_first)
    else:
        lru_out, h_last, lru_conv_new = _lru(
            x2, p["norm_pre"], p["w_all"], lru_conv, h0, *lru_w, nb=nb, t=t, bt=lru_bt, tt=lru_tt,
            reset_first=reset_first)
        z, ba = _in_proj(x2, p["norm_pre"], p["w_all"], p["w_all"], tm=nb * t, tn=SEG,
                         col0=LRU_Z_WIDTH // SEG, width=GDN_Z_WIDTH)
        gdn_out, s_new = _gdn_z(z, ba, gdn_conv, gdn_s, *gdn_w, nb=nb, t=t, bt=gdn_bt, tt=gdn_tt, chunk=chunk,
                                hg=GDN_HEADS)
        gdn_conv_new = z.reshape(nb, t, GDN_Z_WIDTH)[:, t - (CONV_W - 1):, :GDN_QKV_W]
    y = _out_proj(x2, p["norm_pre"], w_m, lru_out, gdn_out, p["wl"], p["wg"], p["wo"], p["norm_post"],
                  tm=min(512, nb * t), m_given=fused)
    return y.reshape(nb, t, D_MODEL), lru_conv_new, h_last.reshape(nb, LRU_WIDTH), gdn_conv_new, s_new


def kernel(x_prompt, x_sample, state_lru_conv, state_lru_h, state_gdn_conv, state_gdn_S, norm_pre, norm_post, w_in, lru_conv_w, lru_conv_b, lru_wa, lru_ba, lru_wx, lru_bx, lru_a_logit, gdn_conv_w, gdn_A_log, gdn_dt_bias, gdn_norm_w, w_br_lru, w_br_gdn, w_out):
    depth = w_in.shape[0]
    assert depth == 1
    nb = x_prompt.shape[0]
    p = _prep_weights(norm_pre[0], norm_post[0], w_in[0], lru_conv_w[0], lru_conv_b[0], lru_wa[0], lru_ba[0],
                      lru_wx[0], lru_bx[0], lru_a_logit[0], gdn_conv_w[0], gdn_A_log[0], gdn_dt_bias[0],
                      gdn_norm_w[0], w_br_lru[0], w_br_gdn[0], w_out[0])
    yp, p_lc, p_lh, p_gc, p_gs = _layer(
        x_prompt,
        jnp.zeros((nb, CONV_W - 1, LRU_WIDTH), F32), jnp.zeros((nb, LRU_WIDTH), F32),
        jnp.zeros((nb, CONV_W - 1, GDN_QKV_W), F32), jnp.zeros((nb, GDN_HEADS, GDN_DK, GDN_DV), F32),
        p, reset_first=True, fused=True, tm=SEQ_PROJ_ROWS)
    ys, s_lc, s_lh, s_gc, s_gs = _layer(
        x_sample, state_lru_conv[0], state_lru_h[0], state_gdn_conv[0], state_gdn_S[0],
        p, reset_first=False, fused=False, tm=1024, lru_bt=32, lru_tt=8, gdn_bt=8, gdn_tt=8)
    return (yp, ys, p_lc[None], p_lh[None], p_gc[None], p_gs[None],
            s_lc[None], s_lh[None], s_gc[None], s_gs[None])
```

```python
import functools

import jax
import jax.numpy as jnp
from jax import lax
from jax.experimental import pallas as pl
from jax.experimental.pallas import tpu as pltpu

F32 = jnp.float32
BF16 = jnp.bfloat16

D_MODEL = 1024
CONV_W = 4
LRU_WIDTH = 1024
LRU_BLOCKS = 8
LRU_BLOCK = LRU_WIDTH // LRU_BLOCKS
LRU_C = 8.0
GDN_HEADS = 8
GDN_DK = 128
GDN_DV = 128
GDN_KEY_W = GDN_HEADS * GDN_DK
GDN_VAL_W = GDN_HEADS * GDN_DV
GDN_QKV_W = 2 * GDN_KEY_W + GDN_VAL_W
GDN_CHUNK = 64
NORM_EPS = 1e-6

LANES = 128
SUBLANES = 8
SEG = 1024
LRU_Z_WIDTH = 2 * SEG
GDN_Z_WIDTH = 4 * SEG
COL_Q, COL_K, COL_V, COL_GDN_GATE = range(4)
BETA_LANE = 0
ALPHA_LANE = GDN_HEADS
BA_COL_BLOCK = (LRU_Z_WIDTH + GDN_Z_WIDTH) // LANES
VMEM_LIMIT = 48 * 1024 * 1024
SEQ_VMEM_LIMIT = 56 * 1024 * 1024
SEQ_TILE = 128
SEQ_PROJ_ROWS = 512
GDN_SEQ_GROUP = 2
LRU_CARRY_PIECE = 8
FILL_EVERY = 6


def _softplus(x):
    return jnp.maximum(x, 0.0) + jnp.log1p(jnp.exp(-jnp.abs(x)))


def _sigmoid(x):
    return jax.nn.sigmoid(x)


def _silu(x):
    h = 0.5 * x
    return h + h * jnp.tanh(h)


def _fdot(a, b, dims=((1,), (0,))):
    return lax.dot_general(a, b, (dims, ((), ())), preferred_element_type=F32)


def _rms_norm_bf16(x, gain):
    ms = jnp.mean(x * x, axis=-1, keepdims=True)
    return ((x * lax.rsqrt(ms + NORM_EPS)) * gain).astype(BF16)


def _resident(shape, index_map):
    return pl.BlockSpec(shape, index_map, pipeline_mode=pl.Buffered(1))


def _in_proj_body(x_ref, gain_ref, w_ref, wba_ref, z_ref, ba_ref, u_scr):
    @pl.when(pl.program_id(1) == 0)
    def _():
        u = _rms_norm_bf16(x_ref[...], gain_ref[...])
        u_scr[...] = u
        ba_ref[...] = jnp.dot(u, wba_ref[...], preferred_element_type=F32)

    z_ref[...] = jnp.dot(u_scr[...], w_ref[...], preferred_element_type=F32)


def _in_proj(x2, gain, w, w_ba, *, tm, tn, col0, width):
    n = x2.shape[0]
    return pl.pallas_call(
        _in_proj_body,
        grid=(n // tm, width // tn),
        in_specs=[
            pl.BlockSpec((tm, D_MODEL), lambda i, j: (i, 0)),
            pl.BlockSpec((1, D_MODEL), lambda i, j: (0, 0)),
            pl.BlockSpec((D_MODEL, tn), lambda i, j: (0, col0 + j)),
            pl.BlockSpec((D_MODEL, LANES), lambda i, j: (0, BA_COL_BLOCK)),
        ],
        out_specs=[
            pl.BlockSpec((tm, tn), lambda i, j: (i, j)),
            pl.BlockSpec((tm, LANES), lambda i, j: (i, 0)),
        ],
        out_shape=[
            jax.ShapeDtypeStruct((n, width), F32),
            jax.ShapeDtypeStruct((n, LANES), F32),
        ],
        scratch_shapes=[pltpu.VMEM((tm, D_MODEL), BF16)],
        compiler_params=pltpu.CompilerParams(
            dimension_semantics=("arbitrary", "arbitrary"), vmem_limit_bytes=VMEM_LIMIT),
        name="in_proj",
    )(x2, gain, w, w_ba)


def _lru_stages(z_lru, gate, j, cw_ref, cb_ref, wax_ref, ba_ref, bx_ref, alog_ref, out_ref,
                xp_scr, a_scr, b_scr, h_scr, *, bt, tt, reset_first):
    rows = bt * tt
    width = LRU_WIDTH
    n_vreg_rows = rows // SUBLANES
    n_groups = tt // SUBLANES
    env = {}

    def conv():
        cur = z_lru().reshape(bt, tt, width)
        xp_scr[:, 8:8 + tt, :] = cur
        cw = cw_ref[...]
        xc = cur * cw[CONV_W - 1].reshape(1, 1, width)
        for i in range(CONV_W - 2, -1, -1):
            xc = xc + xp_scr[:, 5 + i:5 + i + tt, :] * cw[i].reshape(1, 1, width)
        xp_scr[:, 5:8, :] = xp_scr[:, 5 + tt:8 + tt, :]
        env["xc"] = (xc + cb_ref[...].reshape(1, 1, width)).reshape(rows, width)
        a_l = alog_ref[...]
        env["log_sig_a"] = jnp.minimum(a_l, 0.0) - jnp.log1p(jnp.exp(-jnp.abs(a_l)))
        env["sub3"] = lax.broadcasted_iota(jnp.int32, (n_vreg_rows, SUBLANES, LRU_BLOCK), 1)
        if reset_first:
            row = lax.broadcasted_iota(jnp.int32, (rows, LRU_BLOCK), 0)
            env["is_reset"] = jnp.logical_and(row % tt == 0, j == 0)

    def block(g):
        gs = slice(g * LRU_BLOCK, (g + 1) * LRU_BLOCK)
        xg = env["xc"][:, gs]
        pre = jnp.dot(xg.astype(BF16), wax_ref[g], preferred_element_type=F32)
        r = _sigmoid(pre[:, :LRU_BLOCK] + ba_ref[:, gs])
        ig = _sigmoid(pre[:, LRU_BLOCK:] + bx_ref[:, gs])
        log_a = (LRU_C * r) * env["log_sig_a"][:, gs]
        a = jnp.exp(log_a)
        t = jnp.tanh(log_a)
        m2 = (-2.0 * t) / (1.0 - t)
        mult = jnp.where(m2 > 0.0, m2 * lax.rsqrt(m2), 0.0)
        if reset_first:
            a = jnp.where(env["is_reset"], 0.0, a)
            mult = jnp.where(env["is_reset"], 1.0, mult)
        b = mult * ig * xg
        a = a.reshape(n_vreg_rows, SUBLANES, LRU_BLOCK)
        b = b.reshape(n_vreg_rows, SUBLANES, LRU_BLOCK)
        s = 1
        while s < SUBLANES:
            keep = env["sub3"] >= s
            a_sh = jnp.where(keep, pltpu.roll(a, s, axis=1), 1.0)
            b_sh = jnp.where(keep, pltpu.roll(b, s, axis=1), 0.0)
            b = a * b_sh + b
            a = a * a_sh
            s *= 2
        a_scr[:, :, gs] = a.reshape(bt, tt, LRU_BLOCK)
        b_scr[:, :, gs] = b.reshape(bt, tt, LRU_BLOCK)

    def carry_groups(g0, g1):
        carry = env.get("carry")
        if carry is None:
            carry = h_scr[...]
        for i in range(g0, g1):
            rs = slice(i * SUBLANES, (i + 1) * SUBLANES)
            h = b_scr[:, rs, :] + a_scr[:, rs, :] * carry
            b_scr[:, rs, :] = h
            carry = h[:, SUBLANES - 1:SUBLANES, :]
        env["carry"] = carry
        if g1 == n_groups:
            h_scr[...] = carry

    def finish():
        h = b_scr[...].reshape(rows, width)
        out_ref[...] = (h * _silu(gate())).astype(out_ref.dtype)

    per_piece = min(n_groups, LRU_CARRY_PIECE)
    pieces = [functools.partial(carry_groups, g0, min(g0 + per_piece, n_groups))
              for g0 in range(0, n_groups, per_piece)]
    return [conv] + [functools.partial(block, g) for g in range(LRU_BLOCKS)] + pieces + [finish]


def _lru_body(x_ref, gain_ref, w_ref, cs_ref, h0_ref, cw_ref, cb_ref, wax_ref, ba_ref, bx_ref, alog_ref,
              out_ref, hout_ref, csout_ref, xp_scr, a_scr, b_scr, h_scr, *, bt, tt, reset_first):
    j = pl.program_id(1)

    @pl.when(j == 0)
    def _():
        xp_scr[:, 5:8, :] = cs_ref[...]
        h_scr[...] = h0_ref[...]

    u = _rms_norm_bf16(x_ref[...], gain_ref[...])
    z = jnp.dot(u, w_ref[...], preferred_element_type=F32)
    for stage in _lru_stages(lambda: z[:, :LRU_WIDTH], lambda: z[:, LRU_WIDTH:], j,
                             cw_ref, cb_ref, wax_ref, ba_ref, bx_ref, alog_ref, out_ref, xp_scr, a_scr, b_scr, h_scr, bt=bt, tt=tt,
                             reset_first=reset_first):
        stage()

    @pl.when(j == pl.num_programs(1) - 1)
    def _():
        hout_ref[...] = h_scr[...]
        csout_ref[...] = xp_scr[:, 5:8, :]


def _lru_scratch(bt, tt):
    return [
        pltpu.VMEM((bt, SUBLANES + tt, LRU_WIDTH), F32),
        pltpu.VMEM((bt, tt, LRU_WIDTH), F32),
        pltpu.VMEM((bt, tt, LRU_WIDTH), F32),
        pltpu.VMEM((bt, 1, LRU_WIDTH), F32),
    ]


def _lru(x2, gain, w_lru, conv_state, h0, cw, cb, wax, ba, bx, alog, *, nb, t, bt, tt, reset_first):
    nt = t // tt
    rows = bt * tt
    body = functools.partial(_lru_body, bt=bt, tt=tt, reset_first=reset_first)
    vec = lambda: pl.BlockSpec((1, LRU_WIDTH), lambda i, j: (0, 0))
    return pl.pallas_call(
        body,
        grid=(nb // bt, nt),
        in_specs=[
            pl.BlockSpec((rows, D_MODEL), lambda i, j: (i * nt + j, 0)),
            vec(),
            pl.BlockSpec((D_MODEL, LRU_Z_WIDTH), lambda i, j: (0, 0)),
            pl.BlockSpec((bt, CONV_W - 1, LRU_WIDTH), lambda i, j: (i, 0, 0)),
            pl.BlockSpec((bt, 1, LRU_WIDTH), lambda i, j: (i, 0, 0)),
            pl.BlockSpec((CONV_W, LRU_WIDTH), lambda i, j: (0, 0)),
            vec(),
            pl.BlockSpec((LRU_BLOCKS, LRU_BLOCK, 2 * LRU_BLOCK), lambda i, j: (0, 0, 0)),
            vec(), vec(), vec(),
        ],
        out_specs=[
            pl.BlockSpec((rows, LRU_WIDTH), lambda i, j: (i * nt + j, 0)),
            pl.BlockSpec((bt, 1, LRU_WIDTH), lambda i, j: (i, 0, 0)),
            pl.BlockSpec((bt, CONV_W - 1, LRU_WIDTH), lambda i, j: (i, 0, 0)),
        ],
        out_shape=[
            jax.ShapeDtypeStruct((nb * t, LRU_WIDTH), BF16),
            jax.ShapeDtypeStruct((nb, 1, LRU_WIDTH), F32),
            jax.ShapeDtypeStruct((nb, CONV_W - 1, LRU_WIDTH), F32),
        ],
        scratch_shapes=_lru_scratch(bt, tt),
        compiler_params=pltpu.CompilerParams(
            dimension_semantics=("arbitrary", "arbitrary"), vmem_limit_bytes=VMEM_LIMIT),
        name="lru",
    )(x2, gain, w_lru, conv_state, h0, cw, cb, wax, ba, bx, alog)


def _gdn_core(zq, zk, zv, zgate, ba, cw_refs, alog_ref, dtb_ref, nw_ref, out_ref, xp_scr, s_scr,
              h_grp, *, bt, tt, chunk, hg, seq_group, fill=lambda: None):
    rows = bt * tt
    width = LANES * hg
    n_chunks = tt // chunk

    def conv_silu(idx, cur2, cw_ref):
        cur = cur2.reshape(bt, tt, width)
        xp_scr[idx, :, 8:8 + tt, :] = cur
        cw = cw_ref[...]
        y = cur * cw[CONV_W - 1].reshape(1, 1, width)
        for i in range(CONV_W - 2, -1, -1):
            y = y + xp_scr[idx, :, 5 + i:5 + i + tt, :] * cw[i].reshape(1, 1, width)
        xp_scr[idx, :, 5:8, :] = xp_scr[idx, :, 5 + tt:8 + tt, :]
        return _silu(y.reshape(rows, width))

    q_all = conv_silu(0, zq(), cw_refs[0])
    k_all = conv_silu(1, zk(), cw_refs[1])
    v_all = conv_silu(2, zv(), cw_refs[2])
    ba = ba()

    beta_all = _sigmoid(ba)
    g_all = -jnp.exp(alog_ref[...]) * _softplus(ba + dtb_ref[...])
    row = lax.broadcasted_iota(jnp.int32, (rows, LANES), 0)
    row_in_chunk = row % chunk
    gc_all = g_all
    s = 1
    while s < chunk:
        gc_all = gc_all + jnp.where(row_in_chunk >= s, pltpu.roll(gc_all, s, axis=0), 0.0)
        s *= 2
    gct_all = gc_all.T

    lane = lax.broadcasted_iota(jnp.int32, (rows, LANES), 1)
    sub = lax.broadcasted_iota(jnp.int32, (LANES, rows), 0)
    ri = lax.broadcasted_iota(jnp.int32, (rows, rows), 0)
    ci = lax.broadcasted_iota(jnp.int32, (rows, rows), 1)
    same_chunk = (ri // chunk) == (ci // chunk)
    causal = jnp.logical_and(same_chunk, ri >= ci)
    strict = jnp.logical_and(same_chunk, ri > ci)
    n_steps = (chunk - 1).bit_length()

    hd = []
    for hh in range(hg):
        head = h_grp * hg + hh
        ls = slice(hh * LANES, (hh + 1) * LANES)
        qh, kh, vh = q_all[:, ls], k_all[:, ls], v_all[:, ls]
        qh = qh * lax.rsqrt(jnp.sum(qh * qh, axis=-1, keepdims=True) + NORM_EPS) * (GDN_DK ** -0.5)
        kh = kh * lax.rsqrt(jnp.sum(kh * kh, axis=-1, keepdims=True) + NORM_EPS)

        beta = jnp.sum(jnp.where(lane == BETA_LANE + head, beta_all, 0.0), axis=1, keepdims=True)
        gc = jnp.sum(jnp.where(lane == ALPHA_LANE + head, gc_all, 0.0), axis=1, keepdims=True)
        gc_row = jnp.sum(jnp.where(sub == ALPHA_LANE + head, gct_all, 0.0), axis=0, keepdims=True)

        decay = jnp.where(causal, jnp.exp(jnp.where(causal, gc - gc_row, 0.0)), 0.0)
        kb = kh * beta
        gram = _fdot(jnp.concatenate([kb, qh], axis=0), kh, ((1,), (1,)))
        e_gc = jnp.exp(gc)
        hd.append(dict(
            ls=ls, kh=kh, gc=gc,
            p=jnp.where(strict, -(gram[:rows] * decay), 0.0),
            a_qk=gram[rows:] * decay,
            x=jnp.concatenate([vh * beta, kb * e_gc], axis=1),
            q_dec=qh * e_gc))

    n_dots = 0
    for st in range(n_steps):
        for d in hd:
            if st == 0:
                d["n"] = d["p"]
                if n_steps > 1:
                    d["p"] = _fdot(d["p"], d["p"])
            elif st < n_steps - 1:
                res = _fdot(jnp.concatenate([d["n"], d["p"]], axis=0), d["p"])
                d["n"] = d["n"] + d["p"] + res[:rows]
                d["p"] = res[rows:]
            else:
                d["n"] = d["n"] + d["p"] + _fdot(d["n"], d["p"])
            n_dots += 1
            if n_dots % FILL_EVERY == 0:
                fill()
    for d in hd:
        d["x"] = d["x"] + _fdot(d["n"], d["x"])

    for d in hd:
        d["v_new"] = {}
        d["qs"] = {}
    for sq0 in range(0, bt, seq_group):
        pairs = [(sq, hh) for sq in range(sq0, min(sq0 + seq_group, bt)) for hh in range(hg)]
        state = {pr: s_scr[pr[0], pr[1]] for pr in pairs}
        for c in range(n_chunks):
            rsl = {sq: slice(sq * tt + c * chunk, sq * tt + (c + 1) * chunk) for sq, _ in pairs}
            ws, v_new, k_dec, e_last, upd = {}, {}, {}, {}, {}
            for pr in pairs:
                d, rs = hd[pr[1]], rsl[pr[0]]
                ws[pr] = _fdot(jnp.concatenate([d["x"][rs, LANES:], d["q_dec"][rs]], axis=0), state[pr])
            for pr in pairs:
                d, rs = hd[pr[1]], rsl[pr[0]]
                v_new[pr] = d["x"][rs, :LANES] - ws[pr][:chunk]
                g_last = d["gc"][rs.stop - 1:rs.stop]
                k_dec[pr] = d["kh"][rs] * jnp.exp(g_last - d["gc"][rs])
                e_last[pr] = jnp.exp(g_last)
                d["v_new"][(pr[0], c)] = v_new[pr]
                d["qs"][(pr[0], c)] = ws[pr][chunk:]
            for pr in pairs:
                upd[pr] = _fdot(k_dec[pr], v_new[pr], ((0,), (0,)))
            for pr in pairs:
                state[pr] = state[pr] * e_last[pr] + upd[pr]
            fill()
        for pr in pairs:
            s_scr[pr[0], pr[1]] = state[pr]

    order = [(sq, c) for sq in range(bt) for c in range(n_chunks)]
    gate = zgate()
    for d in hd:
        v_parts = [d["v_new"][k] for k in order]
        q_parts = [d["qs"][k] for k in order]
        v_new_all = jnp.concatenate(v_parts, axis=0) if len(v_parts) > 1 else v_parts[0]
        qs_all = jnp.concatenate(q_parts, axis=0) if len(q_parts) > 1 else q_parts[0]
        o = qs_all + _fdot(d["a_qk"], v_new_all)
        o = o * lax.rsqrt(jnp.mean(o * o, axis=-1, keepdims=True) + NORM_EPS) * nw_ref[...]
        o = o * _silu(gate[:, d["ls"]])
        out_ref[:, d["ls"]] = o.astype(out_ref.dtype)


def _gdn_init_state(j, cs_refs, s0_ref, xp_scr, s_scr):
    @pl.when(j == 0)
    def _():
        for idx, cs_ref in enumerate(cs_refs):
            xp_scr[idx, :, 5:8, :] = cs_ref[...]
        s_scr[...] = s0_ref[...]


def _gdn_z_body(zq_ref, zk_ref, zv_ref, zg_ref, ba_ref, csq_ref, csk_ref, csv_ref, s0_ref,
                cwq_ref, cwk_ref, cwv_ref, alog_ref, dtb_ref, nw_ref,
                out_ref, sout_ref, xp_scr, s_scr, *, bt, tt, chunk, hg):
    j = pl.program_id(2)
    _gdn_init_state(j, (csq_ref, csk_ref, csv_ref), s0_ref, xp_scr, s_scr)
    _gdn_core(lambda: zq_ref[...], lambda: zk_ref[...], lambda: zv_ref[...], lambda: zg_ref[...],
              lambda: ba_ref[...],
              (cwq_ref, cwk_ref, cwv_ref), alog_ref, dtb_ref, nw_ref, out_ref, xp_scr, s_scr,
              pl.program_id(1), bt=bt, tt=tt, chunk=chunk, hg=hg, seq_group=GDN_SEQ_GROUP)

    @pl.when(j == pl.num_programs(2) - 1)
    def _():
        sout_ref[...] = s_scr[...]


def _gdn_z(z, ba, conv_state, s0, cw, alog_pad, dtb_pad, nw, *, nb, t, bt, tt, chunk, hg):
    nt = t // tt
    rows = bt * tt
    width = LANES * hg
    per = SEG // width
    body = functools.partial(_gdn_z_body, bt=bt, tt=tt, chunk=chunk, hg=hg)

    def zspec(col):
        return pl.BlockSpec((rows, width), lambda i, h, j: (i * nt + j, col * per + h))

    def cs_spec(seg):
        return pl.BlockSpec((bt, CONV_W - 1, width), lambda i, h, j: (i, 0, seg * per + h))

    def cw_spec(seg):
        return pl.BlockSpec((CONV_W, width), lambda i, h, j: (0, seg * per + h))

    vec = lambda: pl.BlockSpec((1, LANES), lambda i, h, j: (0, 0))
    return pl.pallas_call(
        body,
        grid=(nb // bt, GDN_HEADS // hg, nt),
        in_specs=[
            zspec(COL_Q), zspec(COL_K), zspec(COL_V), zspec(COL_GDN_GATE),
            pl.BlockSpec((rows, LANES), lambda i, h, j: (i * nt + j, 0)),
            cs_spec(0), cs_spec(1), cs_spec(2),
            pl.BlockSpec((bt, hg, GDN_DK, GDN_DV), lambda i, h, j: (i, h, 0, 0)),
            cw_spec(0), cw_spec(1), cw_spec(2),
            vec(), vec(), vec(),
        ],
        out_specs=[
            pl.BlockSpec((rows, width), lambda i, h, j: (i * nt + j, h)),
            pl.BlockSpec((bt, hg, GDN_DK, GDN_DV), lambda i, h, j: (i, h, 0, 0)),
        ],
        out_shape=[
            jax.ShapeDtypeStruct((nb * t, GDN_VAL_W), BF16),
            jax.ShapeDtypeStruct((nb, GDN_HEADS, GDN_DK, GDN_DV), F32),
        ],
        scratch_shapes=[
            pltpu.VMEM((3, bt, SUBLANES + tt, width), F32),
            pltpu.VMEM((bt, hg, GDN_DK, GDN_DV), F32),
        ],
        compiler_params=pltpu.CompilerParams(
            dimension_semantics=("arbitrary", "arbitrary", "arbitrary"), vmem_limit_bytes=VMEM_LIMIT),
        name="gdn_z",
    )(z, z, z, z, ba, conv_state, conv_state, conv_state, s0, cw, cw, cw, alog_pad, dtb_pad, nw)


def _seq_body(x_ref, gain_ref, w_ref, wba_ref,
              lcs_ref, h0_ref, lcw_ref, lcb_ref, wax_ref, lba_ref, lbx_ref, lalog_ref,
              csq_ref, csk_ref, csv_ref, s0_ref, cwq_ref, cwk_ref, cwv_ref, galog_ref, dtb_ref, nw_ref,
              lru_out_ref, hout_ref, lcsout_ref, gdn_out_ref, sout_ref, gcsout_ref,
              lxp_scr, a_scr, b_scr, h_scr, gxp_scr, s_scr, *, tt, sub_tt, chunk, reset_first):
    j = pl.program_id(1)
    n_sub = tt // sub_tt

    @pl.when(j == 0)
    def _():
        lxp_scr[:, 5:8, :] = lcs_ref[...]
        h_scr[...] = h0_ref[...]

    _gdn_init_state(j, (csq_ref, csk_ref, csv_ref), s0_ref, gxp_scr, s_scr)

    u = _rms_norm_bf16(x_ref[...], gain_ref[...])
    projected = {}

    def seg(c, sub):
        def get():
            if c not in projected:
                w = wba_ref[...] if c == "ba" else w_ref[:, c * SEG:(c + 1) * SEG]
                projected[c] = jnp.dot(u, w, preferred_element_type=F32)
            return projected[c][sub * sub_tt:(sub + 1) * sub_tt]
        return get

    for sub in range(n_sub):
        rows = slice(sub * sub_tt, (sub + 1) * sub_tt)
        stages = iter(_lru_stages(seg(0, sub), seg(1, sub), j * n_sub + sub, lcw_ref, lcb_ref, wax_ref, lba_ref,
                                  lbx_ref, lalog_ref, lru_out_ref.at[rows], lxp_scr, a_scr, b_scr, h_scr,
                                  bt=1, tt=sub_tt, reset_first=reset_first))

        def fill():
            stage = next(stages, None)
            if stage is not None:
                stage()

        _gdn_core(seg(2 + COL_Q, sub), seg(2 + COL_K, sub), seg(2 + COL_V, sub), seg(2 + COL_GDN_GATE, sub),
                  seg("ba", sub), (cwq_ref, cwk_ref, cwv_ref), galog_ref, dtb_ref, nw_ref, gdn_out_ref.at[rows],
                  gxp_scr, s_scr, 0, bt=1, tt=sub_tt, chunk=chunk, hg=GDN_HEADS, seq_group=1, fill=fill)
        for stage in stages:
            stage()

    @pl.when(j == pl.num_programs(1) - 1)
    def _():
        hout_ref[...] = h_scr[...]
        lcsout_ref[...] = lxp_scr[:, 5:8, :]
        sout_ref[...] = s_scr[...]
        for idx in range(3):
            gcsout_ref[:, :, idx * SEG:(idx + 1) * SEG] = gxp_scr[idx, :, 5:8, :]


def _seq(x2, gain, w_seq, w_ba, lru_conv, h0, lcw, lcb, wax, lba, lbx, lalog,
         gdn_conv, s0, gcw, galog_pad, dtb_pad, nw, *, nb, t, tt, sub_tt, chunk, reset_first):
    nt = t // tt
    rows = tt
    bt = 1
    hg = GDN_HEADS
    body = functools.partial(_seq_body, tt=tt, sub_tt=sub_tt, chunk=chunk, reset_first=reset_first)
    const2 = lambda i, j: (0, 0)
    vec = lambda n: pl.BlockSpec((1, n), const2)
    cs_spec = lambda seg: pl.BlockSpec((bt, CONV_W - 1, SEG), lambda i, j: (i, 0, seg))
    cw_spec = lambda seg: pl.BlockSpec((CONV_W, SEG), lambda i, j: (0, seg))
    state_spec = pl.BlockSpec((bt, hg, GDN_DK, GDN_DV), lambda i, j: (i, 0, 0, 0))
    return pl.pallas_call(
        body,
        grid=(nb // bt, nt),
        in_specs=[
            pl.BlockSpec((rows, D_MODEL), lambda i, j: (i * nt + j, 0)),
            vec(D_MODEL),
            _resident((D_MODEL, LRU_Z_WIDTH + GDN_Z_WIDTH), const2),
            _resident((D_MODEL, LANES), lambda i, j: (0, BA_COL_BLOCK)),
            pl.BlockSpec((bt, CONV_W - 1, LRU_WIDTH), lambda i, j: (i, 0, 0)),
            pl.BlockSpec((bt, 1, LRU_WIDTH), lambda i, j: (i, 0, 0)),
            pl.BlockSpec((CONV_W, LRU_WIDTH), const2),
            vec(LRU_WIDTH),
            pl.BlockSpec((LRU_BLOCKS, LRU_BLOCK, 2 * LRU_BLOCK), lambda i, j: (0, 0, 0)),
            vec(LRU_WIDTH), vec(LRU_WIDTH), vec(LRU_WIDTH),
            cs_spec(0), cs_spec(1), cs_spec(2),
            state_spec,
            cw_spec(0), cw_spec(1), cw_spec(2),
            vec(LANES), vec(LANES), vec(LANES),
        ],
        out_specs=[
            pl.BlockSpec((rows, LRU_WIDTH), lambda i, j: (i * nt + j, 0)),
            pl.BlockSpec((bt, 1, LRU_WIDTH), lambda i, j: (i, 0, 0)),
            pl.BlockSpec((bt, CONV_W - 1, LRU_WIDTH), lambda i, j: (i, 0, 0)),
            pl.BlockSpec((rows, GDN_VAL_W), lambda i, j: (i * nt + j, 0)),
            state_spec,
            pl.BlockSpec((bt, CONV_W - 1, GDN_QKV_W), lambda i, j: (i, 0, 0)),
        ],
        out_shape=[
            jax.ShapeDtypeStruct((nb * t, LRU_WIDTH), BF16),
            jax.ShapeDtypeStruct((nb, 1, LRU_WIDTH), F32),
            jax.ShapeDtypeStruct((nb, CONV_W - 1, LRU_WIDTH), F32),
            jax.ShapeDtypeStruct((nb * t, GDN_VAL_W), BF16),
            jax.ShapeDtypeStruct((nb, GDN_HEADS, GDN_DK, GDN_DV), F32),
            jax.ShapeDtypeStruct((nb, CONV_W - 1, GDN_QKV_W), F32),
        ],
        scratch_shapes=_lru_scratch(bt, sub_tt) + [
            pltpu.VMEM((3, bt, SUBLANES + sub_tt, SEG), F32),
            pltpu.VMEM((bt, hg, GDN_DK, GDN_DV), F32),
        ],
        compiler_params=pltpu.CompilerParams(
            dimension_semantics=("arbitrary", "arbitrary"), vmem_limit_bytes=SEQ_VMEM_LIMIT),
        name="seq",
    )(x2, gain, w_seq, w_ba, lru_conv, h0, lcw, lcb, wax, lba, lbx, lalog,
      gdn_conv, gdn_conv, gdn_conv, s0, gcw, gcw, gcw, galog_pad, dtb_pad, nw)


def _out_body(x_ref, gpre_ref, wm_ref, lru_ref, gdn_ref, wl_ref, wg_ref, wo_ref, gpost_ref, y_ref):
    x = x_ref[...]
    u = _rms_norm_bf16(x, gpre_ref[...])
    m = jnp.dot(u, wm_ref[...], preferred_element_type=F32)
    p_lru = jnp.dot(lru_ref[...], wl_ref[...], preferred_element_type=F32)
    p_gdn = jnp.dot(gdn_ref[...], wg_ref[...], preferred_element_type=F32)
    merged = _sigmoid(m[:, :D_MODEL]) * p_lru + _sigmoid(m[:, D_MODEL:]) * p_gdn
    y = jnp.dot(merged.astype(BF16), wo_ref[...], preferred_element_type=F32)
    ms = jnp.mean(y * y, axis=-1, keepdims=True)
    y_ref[...] = x + (y * lax.rsqrt(ms + NORM_EPS)) * gpost_ref[...]


def _out_proj(x2, gpre, w_m, lru_out, gdn_out, wl, wg, wo, gpost, *, tm):
    n = x2.shape[0]
    wspec = lambda: _resident((D_MODEL, D_MODEL), lambda i: (0, 0))
    vec = lambda: pl.BlockSpec((1, D_MODEL), lambda i: (0, 0))
    rowspec = lambda: pl.BlockSpec((tm, D_MODEL), lambda i: (i, 0))
    return pl.pallas_call(
        _out_body,
        grid=(n // tm,),
        in_specs=[
            rowspec(), vec(),
            _resident((D_MODEL, 2 * D_MODEL), lambda i: (0, 0)),
            rowspec(), rowspec(),
            wspec(), wspec(), wspec(),
            vec(),
        ],
        out_specs=rowspec(),
        out_shape=jax.ShapeDtypeStruct((n, D_MODEL), F32),
        compiler_params=pltpu.CompilerParams(
            dimension_semantics=("arbitrary",), vmem_limit_bytes=VMEM_LIMIT),
        name="out_proj",
    )(x2, gpre, w_m, lru_out, gdn_out, wl, wg, wo, gpost)


def _prep_weights(norm_pre, norm_post, w_in, lru_conv_w, lru_conv_b, lru_wa, lru_ba, lru_wx, lru_bx,
                  lru_a_logit, gdn_conv_w, gdn_A_log, gdn_dt_bias, gdn_norm_w, w_br_lru, w_br_gdn, w_out):
    c_m = BA_COL_BLOCK * LANES + 2 * GDN_HEADS
    wax = jnp.concatenate([lru_wa, lru_wx], axis=-1).astype(BF16)
    pad_alpha = lambda v: jnp.pad(v.reshape(1, GDN_HEADS), ((0, 0), (ALPHA_LANE, LANES - ALPHA_LANE - GDN_HEADS)))
    row = lambda v: v.reshape(1, -1)
    w_all = w_in.astype(BF16)
    return dict(
        norm_pre=row(norm_pre), norm_post=row(norm_post),
        w_all=w_all, w_m=w_all[:, c_m:],
        lru_cw=lru_conv_w, lru_cb=row(lru_conv_b), wax=wax, lru_ba=row(lru_ba), lru_bx=row(lru_bx),
        lru_alog=row(lru_a_logit), gdn_cw=gdn_conv_w, alog_pad=pad_alpha(gdn_A_log),
        dtb_pad=pad_alpha(gdn_dt_bias), gdn_nw=row(gdn_norm_w),
        wl=w_br_lru.astype(BF16), wg=w_br_gdn.astype(BF16), wo=w_out.astype(BF16))


def _layer(x, lru_conv, lru_h, gdn_conv, gdn_s, p, *, reset_first, fused, tm, lru_bt=None, lru_tt=None,
           gdn_bt=None, gdn_tt=None):
    nb, t, _ = x.shape
    x2 = x.reshape(nb * t, D_MODEL)
    h0 = lru_h.reshape(nb, 1, LRU_WIDTH)
    chunk = min(GDN_CHUNK, t)
    lru_w = (p["lru_cw"], p["lru_cb"], p["wax"], p["lru_ba"], p["lru_bx"], p["lru_alog"])
    gdn_w = (p["gdn_cw"], p["alog_pad"], p["dtb_pad"], p["gdn_nw"])
    if fused:
        lru_out, h_last, lru_conv_new, gdn_out, s_new, gdn_conv_new = _seq(
            x2, p["norm_pre"], p["w_all"], p["w_all"], lru_conv, h0, *lru_w, gdn_conv, gdn_s, *gdn_w,
            nb=nb, t=t, tt=tm, sub_tt=SEQ_TILE, chunk=chunk, reset_first=reset_first)
    else:
        lru_out, h_last, lru_conv_new = _lru(
            x2, p["norm_pre"], p["w_all"], lru_conv, h0, *lru_w, nb=nb, t=t, bt=lru_bt, tt=lru_tt,
            reset_first=reset_first)
        z, ba = _in_proj(x2, p["norm_pre"], p["w_all"], p["w_all"], tm=nb * t, tn=SEG,
                         col0=LRU_Z_WIDTH // SEG, width=GDN_Z_WIDTH)
        gdn_out, s_new = _gdn_z(z, ba, gdn_conv, gdn_s, *gdn_w, nb=nb, t=t, bt=gdn_bt, tt=gdn_tt, chunk=chunk,
                                hg=GDN_HEADS)
        gdn_conv_new = z.reshape(nb, t, GDN_Z_WIDTH)[:, t - (CONV_W - 1):, :GDN_QKV_W]
    y = _out_proj(x2, p["norm_pre"], p["w_m"], lru_out, gdn_out, p["wl"], p["wg"], p["wo"], p["norm_post"],
                  tm=min(1024, nb * t))
    return y.reshape(nb, t, D_MODEL), lru_conv_new, h_last.reshape(nb, LRU_WIDTH), gdn_conv_new, s_new


def kernel(x_prompt, x_sample, state_lru_conv, state_lru_h, state_gdn_conv, state_gdn_S, norm_pre, norm_post, w_in, lru_conv_w, lru_conv_b, lru_wa, lru_ba, lru_wx, lru_bx, lru_a_logit, gdn_conv_w, gdn_A_log, gdn_dt_bias, gdn_norm_w, w_br_lru, w_br_gdn, w_out):
    depth = w_in.shape[0]
    assert depth == 1
    nb = x_prompt.shape[0]
    p = _prep_weights(norm_pre[0], norm_post[0], w_in[0], lru_conv_w[0], lru_conv_b[0], lru_wa[0], lru_ba[0],
                      lru_wx[0], lru_bx[0], lru_a_logit[0], gdn_conv_w[0], gdn_A_log[0], gdn_dt_bias[0],
                      gdn_norm_w[0], w_br_lru[0], w_br_gdn[0], w_out[0])
    yp, p_lc, p_lh, p_gc, p_gs = _layer(
        x_prompt,
        jnp.zeros((nb, CONV_W - 1, LRU_WIDTH), F32), jnp.zeros((nb, LRU_WIDTH), F32),
        jnp.zeros((nb, CONV_W - 1, GDN_QKV_W), F32), jnp.zeros((nb, GDN_HEADS, GDN_DK, GDN_DV), F32),
        p, reset_first=True, fused=True, tm=SEQ_PROJ_ROWS)
    ys, s_lc, s_lh, s_gc, s_gs = _layer(
        x_sample, state_lru_conv[0], state_lru_h[0], state_gdn_conv[0], state_gdn_S[0],
        p, reset_first=False, fused=False, tm=1024, lru_bt=32, lru_tt=8, gdn_bt=8, gdn_tt=8)
    return (yp, ys, p_lc[None], p_lh[None], p_gc[None], p_gs[None],
            s_lc[None], s_lh[None], s_gc[None], s_gs[None])
```

```python
import functools

import jax
import jax.numpy as jnp
from jax import lax
from jax.experimental import pallas as pl
from jax.experimental.pallas import tpu as pltpu

F32 = jnp.float32
BF16 = jnp.bfloat16

D_MODEL = 1024
CONV_W = 4
LRU_WIDTH = 1024
LRU_BLOCKS = 8
LRU_BLOCK = LRU_WIDTH // LRU_BLOCKS
LRU_C = 8.0
GDN_HEADS = 8
GDN_DK = 128
GDN_DV = 128
GDN_KEY_W = GDN_HEADS * GDN_DK
GDN_VAL_W = GDN_HEADS * GDN_DV
GDN_QKV_W = 2 * GDN_KEY_W + GDN_VAL_W
GDN_CHUNK = 64
NORM_EPS = 1e-6

LANES = 128
SUBLANES = 8
SEG = 1024
LRU_Z_WIDTH = 2 * SEG
GDN_Z_WIDTH = 4 * SEG
COL_Q, COL_K, COL_V, COL_GDN_GATE = range(4)
BETA_LANE = 0
ALPHA_LANE = GDN_HEADS
BA_COL_BLOCK = (LRU_Z_WIDTH + GDN_Z_WIDTH) // LANES
VMEM_LIMIT = 48 * 1024 * 1024
SEQ_VMEM_LIMIT = 56 * 1024 * 1024
SEQ_TILE = 128
SEQ_PROJ_ROWS = 512
GDN_SEQ_GROUP = 2
LRU_CARRY_PIECE = 8
FILL_EVERY = 6


def _softplus(x):
    return jnp.maximum(x, 0.0) + jnp.log1p(jnp.exp(-jnp.abs(x)))


def _sigmoid(x):
    return jax.nn.sigmoid(x)


def _silu(x):
    h = 0.5 * x
    return h + h * jnp.tanh(h)


def _fdot(a, b, dims=((1,), (0,))):
    return lax.dot_general(a, b, (dims, ((), ())), preferred_element_type=F32)


def _dot_nt(a, w_t):
    return lax.dot_general(a, w_t, (((1,), (1,)), ((), ())), preferred_element_type=F32)


def _rms_norm_bf16(x, gain):
    ms = jnp.mean(x * x, axis=-1, keepdims=True)
    return ((x * lax.rsqrt(ms + NORM_EPS)) * gain).astype(BF16)


def _resident(shape, index_map):
    return pl.BlockSpec(shape, index_map, pipeline_mode=pl.Buffered(1))


def _in_proj_body(x_ref, gain_ref, w_ref, wba_ref, z_ref, ba_ref, u_scr):
    @pl.when(pl.program_id(1) == 0)
    def _():
        u = _rms_norm_bf16(x_ref[...], gain_ref[...])
        u_scr[...] = u
        ba_ref[...] = _dot_nt(u, wba_ref[...])

    z_ref[...] = _dot_nt(u_scr[...], w_ref[...])


def _in_proj(x2, gain, w, w_ba, *, tm, tn, col0, width):
    n = x2.shape[0]
    return pl.pallas_call(
        _in_proj_body,
        grid=(n // tm, width // tn),
        in_specs=[
            pl.BlockSpec((tm, D_MODEL), lambda i, j: (i, 0)),
            pl.BlockSpec((1, D_MODEL), lambda i, j: (0, 0)),
            pl.BlockSpec((tn, D_MODEL), lambda i, j: (col0 + j, 0)),
            pl.BlockSpec((LANES, D_MODEL), lambda i, j: (BA_COL_BLOCK, 0)),
        ],
        out_specs=[
            pl.BlockSpec((tm, tn), lambda i, j: (i, j)),
            pl.BlockSpec((tm, LANES), lambda i, j: (i, 0)),
        ],
        out_shape=[
            jax.ShapeDtypeStruct((n, width), F32),
            jax.ShapeDtypeStruct((n, LANES), F32),
        ],
        scratch_shapes=[pltpu.VMEM((tm, D_MODEL), BF16)],
        compiler_params=pltpu.CompilerParams(
            dimension_semantics=("arbitrary", "arbitrary"), vmem_limit_bytes=VMEM_LIMIT),
        name="in_proj",
    )(x2, gain, w, w_ba)


def _lru_stages(z_lru, gate, j, cw_ref, cb_ref, wax_ref, ba_ref, bx_ref, alog_ref, out_ref,
                xp_scr, a_scr, b_scr, h_scr, *, bt, tt, reset_first):
    rows = bt * tt
    width = LRU_WIDTH
    n_vreg_rows = rows // SUBLANES
    n_groups = tt // SUBLANES
    env = {}

    def conv():
        cur = z_lru().reshape(bt, tt, width)
        xp_scr[:, 8:8 + tt, :] = cur
        cw = cw_ref[...]
        xc = cur * cw[CONV_W - 1].reshape(1, 1, width)
        for i in range(CONV_W - 2, -1, -1):
            xc = xc + xp_scr[:, 5 + i:5 + i + tt, :] * cw[i].reshape(1, 1, width)
        xp_scr[:, 5:8, :] = xp_scr[:, 5 + tt:8 + tt, :]
        env["xc"] = (xc + cb_ref[...].reshape(1, 1, width)).reshape(rows, width)
        a_l = alog_ref[...]
        env["log_sig_a"] = jnp.minimum(a_l, 0.0) - jnp.log1p(jnp.exp(-jnp.abs(a_l)))
        env["sub3"] = lax.broadcasted_iota(jnp.int32, (n_vreg_rows, SUBLANES, LRU_BLOCK), 1)
        if reset_first:
            row = lax.broadcasted_iota(jnp.int32, (rows, LRU_BLOCK), 0)
            env["is_reset"] = jnp.logical_and(row % tt == 0, j == 0)

    def block(g):
        gs = slice(g * LRU_BLOCK, (g + 1) * LRU_BLOCK)
        xg = env["xc"][:, gs]
        pre = jnp.dot(xg.astype(BF16), wax_ref[g], preferred_element_type=F32)
        r = _sigmoid(pre[:, :LRU_BLOCK] + ba_ref[:, gs])
        ig = _sigmoid(pre[:, LRU_BLOCK:] + bx_ref[:, gs])
        log_a = (LRU_C * r) * env["log_sig_a"][:, gs]
        a = jnp.exp(log_a)
        t = jnp.tanh(log_a)
        m2 = (-2.0 * t) / (1.0 - t)
        mult = jnp.where(m2 > 0.0, m2 * lax.rsqrt(m2), 0.0)
        if reset_first:
            a = jnp.where(env["is_reset"], 0.0, a)
            mult = jnp.where(env["is_reset"], 1.0, mult)
        b = mult * ig * xg
        a = a.reshape(n_vreg_rows, SUBLANES, LRU_BLOCK)
        b = b.reshape(n_vreg_rows, SUBLANES, LRU_BLOCK)
        s = 1
        while s < SUBLANES:
            keep = env["sub3"] >= s
            a_sh = jnp.where(keep, pltpu.roll(a, s, axis=1), 1.0)
            b_sh = jnp.where(keep, pltpu.roll(b, s, axis=1), 0.0)
            b = a * b_sh + b
            a = a * a_sh
            s *= 2
        a_scr[:, :, gs] = a.reshape(bt, tt, LRU_BLOCK)
        b_scr[:, :, gs] = b.reshape(bt, tt, LRU_BLOCK)

    def carry_groups(g0, g1):
        carry = env.get("carry")
        if carry is None:
            carry = h_scr[...]
        for i in range(g0, g1):
            rs = slice(i * SUBLANES, (i + 1) * SUBLANES)
            h = b_scr[:, rs, :] + a_scr[:, rs, :] * carry
            b_scr[:, rs, :] = h
            carry = h[:, SUBLANES - 1:SUBLANES, :]
        env["carry"] = carry
        if g1 == n_groups:
            h_scr[...] = carry

    def finish():
        h = b_scr[...].reshape(rows, width)
        out_ref[...] = (h * _silu(gate())).astype(out_ref.dtype)

    per_piece = min(n_groups, LRU_CARRY_PIECE)
    pieces = [functools.partial(carry_groups, g0, min(g0 + per_piece, n_groups))
              for g0 in range(0, n_groups, per_piece)]
    return [conv] + [functools.partial(block, g) for g in range(LRU_BLOCKS)] + pieces + [finish]


def _lru_body(x_ref, gain_ref, w_ref, cs_ref, h0_ref, cw_ref, cb_ref, wax_ref, ba_ref, bx_ref, alog_ref,
              out_ref, hout_ref, csout_ref, xp_scr, a_scr, b_scr, h_scr, *, bt, tt, reset_first):
    j = pl.program_id(1)

    @pl.when(j == 0)
    def _():
        xp_scr[:, 5:8, :] = cs_ref[...]
        h_scr[...] = h0_ref[...]

    u = _rms_norm_bf16(x_ref[...], gain_ref[...])
    z = _dot_nt(u, w_ref[...])
    for stage in _lru_stages(lambda: z[:, :LRU_WIDTH], lambda: z[:, LRU_WIDTH:], j,
                             cw_ref, cb_ref, wax_ref, ba_ref, bx_ref, alog_ref, out_ref, xp_scr, a_scr, b_scr, h_scr, bt=bt, tt=tt,
                             reset_first=reset_first):
        stage()

    @pl.when(j == pl.num_programs(1) - 1)
    def _():
        hout_ref[...] = h_scr[...]
        csout_ref[...] = xp_scr[:, 5:8, :]


def _lru_scratch(bt, tt):
    return [
        pltpu.VMEM((bt, SUBLANES + tt, LRU_WIDTH), F32),
        pltpu.VMEM((bt, tt, LRU_WIDTH), F32),
        pltpu.VMEM((bt, tt, LRU_WIDTH), F32),
        pltpu.VMEM((bt, 1, LRU_WIDTH), F32),
    ]


def _lru(x2, gain, w_lru, conv_state, h0, cw, cb, wax, ba, bx, alog, *, nb, t, bt, tt, reset_first):
    nt = t // tt
    rows = bt * tt
    body = functools.partial(_lru_body, bt=bt, tt=tt, reset_first=reset_first)
    vec = lambda: pl.BlockSpec((1, LRU_WIDTH), lambda i, j: (0, 0))
    return pl.pallas_call(
        body,
        grid=(nb // bt, nt),
        in_specs=[
            pl.BlockSpec((rows, D_MODEL), lambda i, j: (i * nt + j, 0)),
            vec(),
            pl.BlockSpec((LRU_Z_WIDTH, D_MODEL), lambda i, j: (0, 0)),
            pl.BlockSpec((bt, CONV_W - 1, LRU_WIDTH), lambda i, j: (i, 0, 0)),
            pl.BlockSpec((bt, 1, LRU_WIDTH), lambda i, j: (i, 0, 0)),
            pl.BlockSpec((CONV_W, LRU_WIDTH), lambda i, j: (0, 0)),
            vec(),
            pl.BlockSpec((LRU_BLOCKS, LRU_BLOCK, 2 * LRU_BLOCK), lambda i, j: (0, 0, 0)),
            vec(), vec(), vec(),
        ],
        out_specs=[
            pl.BlockSpec((rows, LRU_WIDTH), lambda i, j: (i * nt + j, 0)),
            pl.BlockSpec((bt, 1, LRU_WIDTH), lambda i, j: (i, 0, 0)),
            pl.BlockSpec((bt, CONV_W - 1, LRU_WIDTH), lambda i, j: (i, 0, 0)),
        ],
        out_shape=[
            jax.ShapeDtypeStruct((nb * t, LRU_WIDTH), BF16),
            jax.ShapeDtypeStruct((nb, 1, LRU_WIDTH), F32),
            jax.ShapeDtypeStruct((nb, CONV_W - 1, LRU_WIDTH), F32),
        ],
        scratch_shapes=_lru_scratch(bt, tt),
        compiler_params=pltpu.CompilerParams(
            dimension_semantics=("arbitrary", "arbitrary"), vmem_limit_bytes=VMEM_LIMIT),
        name="lru",
    )(x2, gain, w_lru, conv_state, h0, cw, cb, wax, ba, bx, alog)


def _gdn_core(zq, zk, zv, zgate, ba, cw_refs, alog_ref, dtb_ref, nw_ref, out_ref, xp_scr, s_scr,
              h_grp, *, bt, tt, chunk, hg, seq_group, fill=lambda: None):
    rows = bt * tt
    width = LANES * hg
    n_chunks = tt // chunk

    def conv_silu(idx, cur2, cw_ref):
        cur = cur2.reshape(bt, tt, width)
        xp_scr[idx, :, 8:8 + tt, :] = cur
        cw = cw_ref[...]
        y = cur * cw[CONV_W - 1].reshape(1, 1, width)
        for i in range(CONV_W - 2, -1, -1):
            y = y + xp_scr[idx, :, 5 + i:5 + i + tt, :] * cw[i].reshape(1, 1, width)
        xp_scr[idx, :, 5:8, :] = xp_scr[idx, :, 5 + tt:8 + tt, :]
        return _silu(y.reshape(rows, width))

    q_all = conv_silu(0, zq(), cw_refs[0])
    k_all = conv_silu(1, zk(), cw_refs[1])
    v_all = conv_silu(2, zv(), cw_refs[2])
    ba = ba()

    beta_all = _sigmoid(ba)
    g_all = -jnp.exp(alog_ref[...]) * _softplus(ba + dtb_ref[...])
    row = lax.broadcasted_iota(jnp.int32, (rows, LANES), 0)
    row_in_chunk = row % chunk
    gc_all = g_all
    s = 1
    while s < chunk:
        gc_all = gc_all + jnp.where(row_in_chunk >= s, pltpu.roll(gc_all, s, axis=0), 0.0)
        s *= 2
    gct_all = gc_all.T

    lane = lax.broadcasted_iota(jnp.int32, (rows, LANES), 1)
    sub = lax.broadcasted_iota(jnp.int32, (LANES, rows), 0)
    ri = lax.broadcasted_iota(jnp.int32, (rows, rows), 0)
    ci = lax.broadcasted_iota(jnp.int32, (rows, rows), 1)
    same_chunk = (ri // chunk) == (ci // chunk)
    causal = jnp.logical_and(same_chunk, ri >= ci)
    strict = jnp.logical_and(same_chunk, ri > ci)
    n_steps = (chunk - 1).bit_length()

    hd = []
    for hh in range(hg):
        head = h_grp * hg + hh
        ls = slice(hh * LANES, (hh + 1) * LANES)
        qh, kh, vh = q_all[:, ls], k_all[:, ls], v_all[:, ls]
        qh = qh * lax.rsqrt(jnp.sum(qh * qh, axis=-1, keepdims=True) + NORM_EPS) * (GDN_DK ** -0.5)
        kh = kh * lax.rsqrt(jnp.sum(kh * kh, axis=-1, keepdims=True) + NORM_EPS)

        beta = jnp.sum(jnp.where(lane == BETA_LANE + head, beta_all, 0.0), axis=1, keepdims=True)
        gc = jnp.sum(jnp.where(lane == ALPHA_LANE + head, gc_all, 0.0), axis=1, keepdims=True)
        gc_row = jnp.sum(jnp.where(sub == ALPHA_LANE + head, gct_all, 0.0), axis=0, keepdims=True)

        decay = jnp.where(causal, jnp.exp(jnp.where(causal, gc - gc_row, 0.0)), 0.0)
        kb = kh * beta
        gram = _fdot(jnp.concatenate([kb, qh], axis=0), kh, ((1,), (1,)))
        e_gc = jnp.exp(gc)
        hd.append(dict(
            ls=ls, kh=kh, gc=gc,
            p=jnp.where(strict, -(gram[:rows] * decay), 0.0),
            a_qk=gram[rows:] * decay,
            x=jnp.concatenate([vh * beta, kb * e_gc], axis=1),
            q_dec=qh * e_gc))

    n_dots = 0
    for st in range(n_steps):
        for d in hd:
            if st == 0:
                d["n"] = d["p"]
                if n_steps > 1:
                    d["p"] = _fdot(d["p"], d["p"])
            elif st < n_steps - 1:
                res = _fdot(jnp.concatenate([d["n"], d["p"]], axis=0), d["p"])
                d["n"] = d["n"] + d["p"] + res[:rows]
                d["p"] = res[rows:]
            else:
                d["n"] = d["n"] + d["p"] + _fdot(d["n"], d["p"])
            n_dots += 1
            if n_dots % FILL_EVERY == 0:
                fill()
    for d in hd:
        d["x"] = d["x"] + _fdot(d["n"], d["x"])

    for d in hd:
        d["v_new"] = {}
        d["qs"] = {}
    for sq0 in range(0, bt, seq_group):
        pairs = [(sq, hh) for sq in range(sq0, min(sq0 + seq_group, bt)) for hh in range(hg)]
        state = {pr: s_scr[pr[0], pr[1]] for pr in pairs}
        for c in range(n_chunks):
            rsl = {sq: slice(sq * tt + c * chunk, sq * tt + (c + 1) * chunk) for sq, _ in pairs}
            ws, v_new, k_dec, e_last, upd = {}, {}, {}, {}, {}
            for pr in pairs:
                d, rs = hd[pr[1]], rsl[pr[0]]
                ws[pr] = _fdot(jnp.concatenate([d["x"][rs, LANES:], d["q_dec"][rs]], axis=0), state[pr])
            for pr in pairs:
                d, rs = hd[pr[1]], rsl[pr[0]]
                v_new[pr] = d["x"][rs, :LANES] - ws[pr][:chunk]
                g_last = d["gc"][rs.stop - 1:rs.stop]
                k_dec[pr] = d["kh"][rs] * jnp.exp(g_last - d["gc"][rs])
                e_last[pr] = jnp.exp(g_last)
                d["v_new"][(pr[0], c)] = v_new[pr]
                d["qs"][(pr[0], c)] = ws[pr][chunk:]
            for pr in pairs:
                upd[pr] = _fdot(k_dec[pr], v_new[pr], ((0,), (0,)))
            for pr in pairs:
                state[pr] = state[pr] * e_last[pr] + upd[pr]
            fill()
        for pr in pairs:
            s_scr[pr[0], pr[1]] = state[pr]

    order = [(sq, c) for sq in range(bt) for c in range(n_chunks)]
    gate = zgate()
    for d in hd:
        v_parts = [d["v_new"][k] for k in order]
        q_parts = [d["qs"][k] for k in order]
        v_new_all = jnp.concatenate(v_parts, axis=0) if len(v_parts) > 1 else v_parts[0]
        qs_all = jnp.concatenate(q_parts, axis=0) if len(q_parts) > 1 else q_parts[0]
        o = qs_all + _fdot(d["a_qk"], v_new_all)
        o = o * lax.rsqrt(jnp.mean(o * o, axis=-1, keepdims=True) + NORM_EPS) * nw_ref[...]
        o = o * _silu(gate[:, d["ls"]])
        out_ref[:, d["ls"]] = o.astype(out_ref.dtype)


def _gdn_init_state(j, cs_refs, s0_ref, xp_scr, s_scr):
    @pl.when(j == 0)
    def _():
        for idx, cs_ref in enumerate(cs_refs):
            xp_scr[idx, :, 5:8, :] = cs_ref[...]
        s_scr[...] = s0_ref[...]


def _gdn_z_body(zq_ref, zk_ref, zv_ref, zg_ref, ba_ref, csq_ref, csk_ref, csv_ref, s0_ref,
                cwq_ref, cwk_ref, cwv_ref, alog_ref, dtb_ref, nw_ref,
                out_ref, sout_ref, xp_scr, s_scr, *, bt, tt, chunk, hg):
    j = pl.program_id(2)
    _gdn_init_state(j, (csq_ref, csk_ref, csv_ref), s0_ref, xp_scr, s_scr)
    _gdn_core(lambda: zq_ref[...], lambda: zk_ref[...], lambda: zv_ref[...], lambda: zg_ref[...],
              lambda: ba_ref[...],
              (cwq_ref, cwk_ref, cwv_ref), alog_ref, dtb_ref, nw_ref, out_ref, xp_scr, s_scr,
              pl.program_id(1), bt=bt, tt=tt, chunk=chunk, hg=hg, seq_group=GDN_SEQ_GROUP)

    @pl.when(j == pl.num_programs(2) - 1)
    def _():
        sout_ref[...] = s_scr[...]


def _gdn_z(z, ba, conv_state, s0, cw, alog_pad, dtb_pad, nw, *, nb, t, bt, tt, chunk, hg):
    nt = t // tt
    rows = bt * tt
    width = LANES * hg
    per = SEG // width
    body = functools.partial(_gdn_z_body, bt=bt, tt=tt, chunk=chunk, hg=hg)

    def zspec(col):
        return pl.BlockSpec((rows, width), lambda i, h, j: (i * nt + j, col * per + h))

    def cs_spec(seg):
        return pl.BlockSpec((bt, CONV_W - 1, width), lambda i, h, j: (i, 0, seg * per + h))

    def cw_spec(seg):
        return pl.BlockSpec((CONV_W, width), lambda i, h, j: (0, seg * per + h))

    vec = lambda: pl.BlockSpec((1, LANES), lambda i, h, j: (0, 0))
    return pl.pallas_call(
        body,
        grid=(nb // bt, GDN_HEADS // hg, nt),
        in_specs=[
            zspec(COL_Q), zspec(COL_K), zspec(COL_V), zspec(COL_GDN_GATE),
            pl.BlockSpec((rows, LANES), lambda i, h, j: (i * nt + j, 0)),
            cs_spec(0), cs_spec(1), cs_spec(2),
            pl.BlockSpec((bt, hg, GDN_DK, GDN_DV), lambda i, h, j: (i, h, 0, 0)),
            cw_spec(0), cw_spec(1), cw_spec(2),
            vec(), vec(), vec(),
        ],
        out_specs=[
            pl.BlockSpec((rows, width), lambda i, h, j: (i * nt + j, h)),
            pl.BlockSpec((bt, hg, GDN_DK, GDN_DV), lambda i, h, j: (i, h, 0, 0)),
        ],
        out_shape=[
            jax.ShapeDtypeStruct((nb * t, GDN_VAL_W), BF16),
            jax.ShapeDtypeStruct((nb, GDN_HEADS, GDN_DK, GDN_DV), F32),
        ],
        scratch_shapes=[
            pltpu.VMEM((3, bt, SUBLANES + tt, width), F32),
            pltpu.VMEM((bt, hg, GDN_DK, GDN_DV), F32),
        ],
        compiler_params=pltpu.CompilerParams(
            dimension_semantics=("arbitrary", "arbitrary", "arbitrary"), vmem_limit_bytes=VMEM_LIMIT),
        name="gdn_z",
    )(z, z, z, z, ba, conv_state, conv_state, conv_state, s0, cw, cw, cw, alog_pad, dtb_pad, nw)


def _seq_body(x_ref, gain_ref, w_ref, wba_ref,
              lcs_ref, h0_ref, lcw_ref, lcb_ref, wax_ref, lba_ref, lbx_ref, lalog_ref,
              csq_ref, csk_ref, csv_ref, s0_ref, cwq_ref, cwk_ref, cwv_ref, galog_ref, dtb_ref, nw_ref,
              lru_out_ref, hout_ref, lcsout_ref, gdn_out_ref, sout_ref, gcsout_ref,
              lxp_scr, a_scr, b_scr, h_scr, gxp_scr, s_scr, *, tt, sub_tt, chunk, reset_first):
    j = pl.program_id(1)
    n_sub = tt // sub_tt

    @pl.when(j == 0)
    def _():
        lxp_scr[:, 5:8, :] = lcs_ref[...]
        h_scr[...] = h0_ref[...]

    _gdn_init_state(j, (csq_ref, csk_ref, csv_ref), s0_ref, gxp_scr, s_scr)

    u = _rms_norm_bf16(x_ref[...], gain_ref[...])
    projected = {}

    def seg(c, sub):
        def get():
            if c not in projected:
                w_t = wba_ref[...] if c == "ba" else w_ref[c * SEG:(c + 1) * SEG, :]
                projected[c] = _dot_nt(u, w_t)
            return projected[c][sub * sub_tt:(sub + 1) * sub_tt]
        return get

    for sub in range(n_sub):
        rows = slice(sub * sub_tt, (sub + 1) * sub_tt)
        stages = iter(_lru_stages(seg(0, sub), seg(1, sub), j * n_sub + sub, lcw_ref, lcb_ref, wax_ref, lba_ref,
                                  lbx_ref, lalog_ref, lru_out_ref.at[rows], lxp_scr, a_scr, b_scr, h_scr,
                                  bt=1, tt=sub_tt, reset_first=reset_first))

        def fill():
            stage = next(stages, None)
            if stage is not None:
                stage()

        _gdn_core(seg(2 + COL_Q, sub), seg(2 + COL_K, sub), seg(2 + COL_V, sub), seg(2 + COL_GDN_GATE, sub),
                  seg("ba", sub), (cwq_ref, cwk_ref, cwv_ref), galog_ref, dtb_ref, nw_ref, gdn_out_ref.at[rows],
                  gxp_scr, s_scr, 0, bt=1, tt=sub_tt, chunk=chunk, hg=GDN_HEADS, seq_group=1, fill=fill)
        for stage in stages:
            stage()

    @pl.when(j == pl.num_programs(1) - 1)
    def _():
        hout_ref[...] = h_scr[...]
        lcsout_ref[...] = lxp_scr[:, 5:8, :]
        sout_ref[...] = s_scr[...]
        for idx in range(3):
            gcsout_ref[:, :, idx * SEG:(idx + 1) * SEG] = gxp_scr[idx, :, 5:8, :]


def _seq(x2, gain, w_seq, w_ba, lru_conv, h0, lcw, lcb, wax, lba, lbx, lalog,
         gdn_conv, s0, gcw, galog_pad, dtb_pad, nw, *, nb, t, tt, sub_tt, chunk, reset_first):
    nt = t // tt
    rows = tt
    bt = 1
    hg = GDN_HEADS
    body = functools.partial(_seq_body, tt=tt, sub_tt=sub_tt, chunk=chunk, reset_first=reset_first)
    const2 = lambda i, j: (0, 0)
    vec = lambda n: pl.BlockSpec((1, n), const2)
    cs_spec = lambda seg: pl.BlockSpec((bt, CONV_W - 1, SEG), lambda i, j: (i, 0, seg))
    cw_spec = lambda seg: pl.BlockSpec((CONV_W, SEG), lambda i, j: (0, seg))
    state_spec = pl.BlockSpec((bt, hg, GDN_DK, GDN_DV), lambda i, j: (i, 0, 0, 0))
    return pl.pallas_call(
        body,
        grid=(nb // bt, nt),
        in_specs=[
            pl.BlockSpec((rows, D_MODEL), lambda i, j: (i * nt + j, 0)),
            vec(D_MODEL),
            _resident((LRU_Z_WIDTH + GDN_Z_WIDTH, D_MODEL), const2),
            _resident((LANES, D_MODEL), lambda i, j: (BA_COL_BLOCK, 0)),
            pl.BlockSpec((bt, CONV_W - 1, LRU_WIDTH), lambda i, j: (i, 0, 0)),
            pl.BlockSpec((bt, 1, LRU_WIDTH), lambda i, j: (i, 0, 0)),
            pl.BlockSpec((CONV_W, LRU_WIDTH), const2),
            vec(LRU_WIDTH),
            pl.BlockSpec((LRU_BLOCKS, LRU_BLOCK, 2 * LRU_BLOCK), lambda i, j: (0, 0, 0)),
            vec(LRU_WIDTH), vec(LRU_WIDTH), vec(LRU_WIDTH),
            cs_spec(0), cs_spec(1), cs_spec(2),
            state_spec,
            cw_spec(0), cw_spec(1), cw_spec(2),
            vec(LANES), vec(LANES), vec(LANES),
        ],
        out_specs=[
            pl.BlockSpec((rows, LRU_WIDTH), lambda i, j: (i * nt + j, 0)),
            pl.BlockSpec((bt, 1, LRU_WIDTH), lambda i, j: (i, 0, 0)),
            pl.BlockSpec((bt, CONV_W - 1, LRU_WIDTH), lambda i, j: (i, 0, 0)),
            pl.BlockSpec((rows, GDN_VAL_W), lambda i, j: (i * nt + j, 0)),
            state_spec,
            pl.BlockSpec((bt, CONV_W - 1, GDN_QKV_W), lambda i, j: (i, 0, 0)),
        ],
        out_shape=[
            jax.ShapeDtypeStruct((nb * t, LRU_WIDTH), BF16),
            jax.ShapeDtypeStruct((nb, 1, LRU_WIDTH), F32),
            jax.ShapeDtypeStruct((nb, CONV_W - 1, LRU_WIDTH), F32),
            jax.ShapeDtypeStruct((nb * t, GDN_VAL_W), BF16),
            jax.ShapeDtypeStruct((nb, GDN_HEADS, GDN_DK, GDN_DV), F32),
            jax.ShapeDtypeStruct((nb, CONV_W - 1, GDN_QKV_W), F32),
        ],
        scratch_shapes=_lru_scratch(bt, sub_tt) + [
            pltpu.VMEM((3, bt, SUBLANES + sub_tt, SEG), F32),
            pltpu.VMEM((bt, hg, GDN_DK, GDN_DV), F32),
        ],
        compiler_params=pltpu.CompilerParams(
            dimension_semantics=("arbitrary", "arbitrary"), vmem_limit_bytes=SEQ_VMEM_LIMIT),
        name="seq",
    )(x2, gain, w_seq, w_ba, lru_conv, h0, lcw, lcb, wax, lba, lbx, lalog,
      gdn_conv, gdn_conv, gdn_conv, s0, gcw, gcw, gcw, galog_pad, dtb_pad, nw)


def _out_body(x_ref, gpre_ref, wm_ref, lru_ref, gdn_ref, wl_ref, wg_ref, wo_ref, gpost_ref, y_ref):
    x = x_ref[...]
    u = _rms_norm_bf16(x, gpre_ref[...])
    m = _dot_nt(u, wm_ref[...])
    p_lru = jnp.dot(lru_ref[...], wl_ref[...], preferred_element_type=F32)
    p_gdn = jnp.dot(gdn_ref[...], wg_ref[...], preferred_element_type=F32)
    merged = _sigmoid(m[:, :D_MODEL]) * p_lru + _sigmoid(m[:, D_MODEL:]) * p_gdn
    y = jnp.dot(merged.astype(BF16), wo_ref[...], preferred_element_type=F32)
    ms = jnp.mean(y * y, axis=-1, keepdims=True)
    y_ref[...] = x + (y * lax.rsqrt(ms + NORM_EPS)) * gpost_ref[...]


def _out_proj(x2, gpre, w_m, lru_out, gdn_out, wl, wg, wo, gpost, *, tm):
    n = x2.shape[0]
    wspec = lambda: _resident((D_MODEL, D_MODEL), lambda i: (0, 0))
    vec = lambda: pl.BlockSpec((1, D_MODEL), lambda i: (0, 0))
    rowspec = lambda: pl.BlockSpec((tm, D_MODEL), lambda i: (i, 0))
    return pl.pallas_call(
        _out_body,
        grid=(n // tm,),
        in_specs=[
            rowspec(), vec(),
            _resident((2 * D_MODEL, D_MODEL), lambda i: (0, 0)),
            rowspec(), rowspec(),
            wspec(), wspec(), wspec(),
            vec(),
        ],
        out_specs=rowspec(),
        out_shape=jax.ShapeDtypeStruct((n, D_MODEL), F32),
        compiler_params=pltpu.CompilerParams(
            dimension_semantics=("arbitrary",), vmem_limit_bytes=VMEM_LIMIT),
        name="out_proj",
    )(x2, gpre, w_m, lru_out, gdn_out, wl, wg, wo, gpost)


def _prep_weights(norm_pre, norm_post, w_in, lru_conv_w, lru_conv_b, lru_wa, lru_ba, lru_wx, lru_bx,
                  lru_a_logit, gdn_conv_w, gdn_A_log, gdn_dt_bias, gdn_norm_w, w_br_lru, w_br_gdn, w_out):
    c_m = BA_COL_BLOCK * LANES + 2 * GDN_HEADS
    wax = jnp.concatenate([lru_wa, lru_wx], axis=-1).astype(BF16)
    pad_alpha = lambda v: jnp.pad(v.reshape(1, GDN_HEADS), ((0, 0), (ALPHA_LANE, LANES - ALPHA_LANE - GDN_HEADS)))
    row = lambda v: v.reshape(1, -1)
    w_all = w_in.T.astype(BF16)
    return dict(
        norm_pre=row(norm_pre), norm_post=row(norm_post),
        w_all=w_all, w_m=w_all[c_m:],
        lru_cw=lru_conv_w, lru_cb=row(lru_conv_b), wax=wax, lru_ba=row(lru_ba), lru_bx=row(lru_bx),
        lru_alog=row(lru_a_logit), gdn_cw=gdn_conv_w, alog_pad=pad_alpha(gdn_A_log),
        dtb_pad=pad_alpha(gdn_dt_bias), gdn_nw=row(gdn_norm_w),
        wl=w_br_lru.astype(BF16), wg=w_br_gdn.astype(BF16), wo=w_out.astype(BF16))


def _layer(x, lru_conv, lru_h, gdn_conv, gdn_s, p, *, reset_first, fused, tm, lru_bt=None, lru_tt=None,
           gdn_bt=None, gdn_tt=None):
    nb, t, _ = x.shape
    x2 = x.reshape(nb * t, D_MODEL)
    h0 = lru_h.reshape(nb, 1, LRU_WIDTH)
    chunk = min(GDN_CHUNK, t)
    lru_w = (p["lru_cw"], p["lru_cb"], p["wax"], p["lru_ba"], p["lru_bx"], p["lru_alog"])
    gdn_w = (p["gdn_cw"], p["alog_pad"], p["dtb_pad"], p["gdn_nw"])
    if fused:
        lru_out, h_last, lru_conv_new, gdn_out, s_new, gdn_conv_new = _seq(
            x2, p["norm_pre"], p["w_all"], p["w_all"], lru_conv, h0, *lru_w, gdn_conv, gdn_s, *gdn_w,
            nb=nb, t=t, tt=tm, sub_tt=SEQ_TILE, chunk=chunk, reset_first=reset_first)
    else:
        lru_out, h_last, lru_conv_new = _lru(
            x2, p["norm_pre"], p["w_all"], lru_conv, h0, *lru_w, nb=nb, t=t, bt=lru_bt, tt=lru_tt,
            reset_first=reset_first)
        z, ba = _in_proj(x2, p["norm_pre"], p["w_all"], p["w_all"], tm=nb * t, tn=SEG,
                         col0=LRU_Z_WIDTH // SEG, width=GDN_Z_WIDTH)
        gdn_out, s_new = _gdn_z(z, ba, gdn_conv, gdn_s, *gdn_w, nb=nb, t=t, bt=gdn_bt, tt=gdn_tt, chunk=chunk,
                                hg=GDN_HEADS)
        gdn_conv_new = z.reshape(nb, t, GDN_Z_WIDTH)[:, t - (CONV_W - 1):, :GDN_QKV_W]
    y = _out_proj(x2, p["norm_pre"], p["w_m"], lru_out, gdn_out, p["wl"], p["wg"], p["wo"], p["norm_post"],
                  tm=min(1024, nb * t))
    return y.reshape(nb, t, D_MODEL), lru_conv_new, h_last.reshape(nb, LRU_WIDTH), gdn_conv_new, s_new


def kernel(x_prompt, x_sample, state_lru_conv, state_lru_h, state_gdn_conv, state_gdn_S, norm_pre, norm_post, w_in, lru_conv_w, lru_conv_b, lru_wa, lru_ba, lru_wx, lru_bx, lru_a_logit, gdn_conv_w, gdn_A_log, gdn_dt_bias, gdn_norm_w, w_br_lru, w_br_gdn, w_out):
    depth = w_in.shape[0]
    assert depth == 1
    nb = x_prompt.shape[0]
    p = _prep_weights(norm_pre[0], norm_post[0], w_in[0], lru_conv_w[0], lru_conv_b[0], lru_wa[0], lru_ba[0],
                      lru_wx[0], lru_bx[0], lru_a_logit[0], gdn_conv_w[0], gdn_A_log[0], gdn_dt_bias[0],
                      gdn_norm_w[0], w_br_lru[0], w_br_gdn[0], w_out[0])
    yp, p_lc, p_lh, p_gc, p_gs = _layer(
        x_prompt,
        jnp.zeros((nb, CONV_W - 1, LRU_WIDTH), F32), jnp.zeros((nb, LRU_WIDTH), F32),
        jnp.zeros((nb, CONV_W - 1, GDN_QKV_W), F32), jnp.zeros((nb, GDN_HEADS, GDN_DK, GDN_DV), F32),
        p, reset_first=True, fused=True, tm=SEQ_PROJ_ROWS)
    ys, s_lc, s_lh, s_gc, s_gs = _layer(
        x_sample, state_lru_conv[0], state_lru_h[0], state_gdn_conv[0], state_gdn_S[0],
        p, reset_first=False, fused=False, tm=1024, lru_bt=32, lru_tt=8, gdn_bt=8, gdn_tt=8)
    return (yp, ys, p_lc[None], p_lh[None], p_gc[None], p_gs[None],
            s_lc[None], s_lh[None], s_gc[None], s_gs[None])
```
